```python
import math
import jax, jax.numpy as jnp
from jax import lax
import numpy as np

D_MODEL = 1024
BATCH = 8
SEQ = 2048
DEPTH = 4

HD = 64
N_HEADS = D_MODEL // HD
N_KV = 4
Q_PER_KV = N_HEADS // N_KV
D_A = D_MODEL
D_B = N_HEADS * HD
D_C = D_MODEL
SGU_GROUPS = 8
CHUNK = 128
CONV_W = 3
CMP_LEN = 32
CMP_STRIDE = 16
CMP_HID = 2 * HD
SEL_BLOCK = 64
SEL_TOP_N = 8
WINDOW = 512
Q_BLOCK = 128
D_FF = 3 * D_MODEL
N_BUCKETS = 32
MAX_EXACT = N_BUCKETS // 2
REL_MAX_DIST = 128
N_IN = 3 * D_A + D_B + 6 * N_KV * HD + 3 * N_HEADS + 2 * D_C + 3 * D_MODEL
SCALE = HD ** -0.5
EPS = 1e-6
NEG_INF = -1e30
FORCE = 1e9

kernel_name = 'hybrid_conv_nsa_sgu_block'


def rms_norm(x, g):
    x32 = x.astype(jnp.float32)
    y = x32 * lax.rsqrt(jnp.mean(x32 * x32, axis=-1, keepdims=True) + EPS)
    return (y * g.astype(jnp.float32)).astype(x.dtype)


def layer_norm(x, g, b):
    x32 = x.astype(jnp.float32)
    mu = jnp.mean(x32, axis=-1, keepdims=True)
    var = jnp.mean(jnp.square(x32 - mu), axis=-1, keepdims=True)
    y = (x32 - mu) * lax.rsqrt(var + EPS)
    return (y * g.astype(jnp.float32) + b.astype(jnp.float32)).astype(x.dtype)


def causal_dwconv3(x, w):
    s = x.shape[1]
    xp = jnp.pad(x, ((0, 0), (CONV_W - 1, 0), (0, 0)))
    return sum(w[k] * xp[:, k:k + s] for k in range(CONV_W))


def masked_softmax(logits, mask):
    lg = jnp.where(mask, logits, NEG_INF)
    return jnp.where(mask, jax.nn.softmax(lg, axis=-1), 0.0)


def rel_bucket(dist):
    dist = jnp.maximum(dist, 0)
    log_ratio = jnp.log(jnp.maximum(dist, 1).astype(jnp.float32) / MAX_EXACT) / math.log(REL_MAX_DIST / MAX_EXACT)
    large = MAX_EXACT + (log_ratio * (N_BUCKETS - MAX_EXACT)).astype(jnp.int32)
    return jnp.where(dist < MAX_EXACT, dist, jnp.minimum(large, N_BUCKETS - 1))


def compress_blocks(k, pos_emb, w1, w2):
    b_, g_, s_, _ = k.shape
    n_cmp = (s_ - CMP_LEN) // CMP_STRIDE + 1
    idx = CMP_STRIDE * jnp.arange(n_cmp)[:, None] + jnp.arange(CMP_LEN)[None]
    blocks = (k[:, :, idx] + pos_emb).reshape(b_, g_, n_cmp, CMP_LEN * HD)
    return jax.nn.gelu(blocks @ w1) @ w2


def nsa_mixer(q, kc, vc, ks, vs, kw, vw, gates, rel_bias, cmp_pos, cmp_w1, cmp_w2):
    b_, s_ = q.shape[0], q.shape[1]
    n_qb = s_ // Q_BLOCK
    q = q.reshape(b_, s_, N_KV, Q_PER_KV, HD).transpose(0, 2, 3, 1, 4)
    kc, vc, ks, vs, kw, vw = [a.transpose(0, 2, 1, 3) for a in (kc, vc, ks, vs, kw, vw)]
    k_cmp = compress_blocks(kc, cmp_pos[0], cmp_w1[0], cmp_w2[0])
    v_cmp = compress_blocks(vc, cmp_pos[1], cmp_w1[1], cmp_w2[1])
    n_cmp = k_cmp.shape[2]
    cmp_start = jnp.arange(n_cmp) * CMP_STRIDE
    cmp_end = cmp_start + CMP_LEN - 1
    n_blk = s_ // SEL_BLOCK
    n_sel = min(SEL_TOP_N, n_blk)
    blk_start = jnp.arange(n_blk) * SEL_BLOCK
    overlap = ((cmp_start[:, None] < blk_start[None] + SEL_BLOCK)
               & (cmp_end[:, None] >= blk_start[None])).astype(jnp.float32)
    k_blk = ks.reshape(b_, N_KV, n_blk, SEL_BLOCK, HD)
    v_blk = vs.reshape(b_, N_KV, n_blk, SEL_BLOCK, HD)
    kw_pad = jnp.pad(kw, ((0, 0), (0, 0), (WINDOW, 0), (0, 0)))
    vw_pad = jnp.pad(vw, ((0, 0), (0, 0), (WINDOW, 0), (0, 0)))
    table = rel_bias.astype(jnp.float32)
    table_g = table.reshape(N_BUCKETS, N_KV, Q_PER_KV).transpose(1, 0, 2)
    bi = jnp.arange(b_)[:, None, None, None]
    gi = jnp.arange(N_KV)[None, :, None, None]

    def static_bias(dist):
        return table[rel_bucket(dist)].reshape(dist.shape + (N_KV, Q_PER_KV)).transpose(2, 3, 0, 1)

    def block(args):
        qb, gb, i = args
        t0 = i * Q_BLOCK
        t = t0 + jnp.arange(Q_BLOCK)
        d_c = t[:, None] - cmp_end[None]
        ok_c = d_c >= 0
        lg = jnp.einsum('bgrqd,bgnd->bgrqn', qb, k_cmp).astype(jnp.float32) * SCALE + static_bias(d_c)
        p_c = masked_softmax(lg, ok_c)
        o_c = jnp.einsum('bgrqn,bgnd->bgrqd', p_c.astype(v_cmp.dtype), v_cmp)
        imp = jnp.einsum('bgrqn,nj->bgqj', p_c, overlap)
        cur = t // SEL_BLOCK
        j = jnp.arange(n_blk)
        forced = (j[None] == 0) | (j[None] == cur[:, None]) | (j[None] == cur[:, None] - 1)
        valid = blk_start[None] <= t[:, None]
        score = jnp.where(forced, FORCE, jnp.where(valid, imp, -FORCE))
        top_s, idx = lax.top_k(score, n_sel)
        n_key = n_sel * SEL_BLOCK
        k_g = k_blk[bi, gi, idx].reshape(b_, N_KV, Q_BLOCK, n_key, HD)
        v_g = v_blk[bi, gi, idx].reshape(b_, N_KV, Q_BLOCK, n_key, HD)
        pos = (idx[..., None] * SEL_BLOCK + jnp.arange(SEL_BLOCK)).reshape(b_, N_KV, Q_BLOCK, n_key)
        d_s = t[:, None] - pos
        ok_s = jnp.repeat(top_s >= 0, SEL_BLOCK, axis=-1) & (d_s >= 0)
        bias_s = jnp.moveaxis(table_g[gi, rel_bucket(d_s)], -1, 2)
        lg = jnp.einsum('bgrqd,bgqkd->bgrqk', qb, k_g).astype(jnp.float32) * SCALE + bias_s
        p_s = masked_softmax(lg, ok_s[:, :, None])
        o_s = jnp.einsum('bgrqk,bgqkd->bgrqd', p_s.astype(v_g.dtype), v_g)
        kwb = lax.dynamic_slice_in_dim(kw_pad, t0, WINDOW + Q_BLOCK, axis=2)
        vwb = lax.dynamic_slice_in_dim(vw_pad, t0, WINDOW + Q_BLOCK, axis=2)
        s_pos = t0 - WINDOW + jnp.arange(WINDOW + Q_BLOCK)
        d_w = t[:, None] - s_pos[None]
        ok_w = (d_w >= 0) & (d_w < WINDOW) & (s_pos[None] >= 0)
        lg = jnp.einsum('bgrqd,bgkd->bgrqk', qb, kwb).astype(jnp.float32) * SCALE + static_bias(d_w)
        p_w = masked_softmax(lg, ok_w)
        o_w = jnp.einsum('bgrqk,bgkd->bgrqd', p_w.astype(vwb.dtype), vwb)
        g = gb.reshape(b_, Q_BLOCK, 3, N_KV, Q_PER_KV).transpose(2, 0, 3, 4, 1)[..., None]
        return g[0] * o_c + g[1] * o_s + g[2] * o_w

    q_blocks = q.reshape(b_, N_KV, Q_PER_KV, n_qb, Q_BLOCK, HD).transpose(3, 0, 1, 2, 4, 5)
    g_blocks = gates.reshape(b_, n_qb, Q_BLOCK, 3, N_HEADS).transpose(1, 0, 2, 3, 4)
    out = lax.map(block, (q_blocks, g_blocks, jnp.arange(n_qb)))
    return out.transpose(1, 0, 4, 2, 3, 5).reshape(b_, s_, D_B)


def spatial_gating(u, v, w_s, b_s, g, bn):
    b_, s_, _ = v.shape
    v = layer_norm(v, g, bn).reshape(b_, s_ // CHUNK, CHUNK, SGU_GROUPS, D_C // SGU_GROUPS)
    w = jnp.where(jnp.tril(jnp.ones((CHUNK, CHUNK), dtype=bool)), w_s, 0)
    mixed = jnp.einsum('gts,bcsgk->bctgk', w, v) + b_s.T[:, :, None]
    return u * mixed.reshape(b_, s_, D_C)


def hybrid_mixer(h, rel_bias, w_in, conv_a, cmp_pos, cmp_w1, cmp_w2, sgu_w, sgu_b,
                 sgu_norm_g, sgu_norm_b, w_br_a, w_br_b, w_br_c, w_o):
    b_, s_, _ = h.shape
    proj = h @ w_in
    sizes = [D_A, D_A, D_A, D_B] + [N_KV * HD] * 6 + [3 * N_HEADS, D_C, D_C, D_MODEL, D_MODEL, D_MODEL]
    splits, acc = [], 0
    for sz in sizes[:-1]:
        acc += sz
        splits.append(acc)
    (gate_b, gate_c, val_a, q, kc, vc, ks, vs, kw, vw, g_nsa,
     u_c, v_c, g0, g1, g2) = jnp.split(proj, splits, axis=-1)
    out_a = gate_b * causal_dwconv3(gate_c * val_a, conv_a)
    kv = [a.reshape(b_, s_, N_KV, HD) for a in (kc, vc, ks, vs, kw, vw)]
    out_b = nsa_mixer(q.reshape(b_, s_, N_HEADS, HD), *kv,
                      jax.nn.sigmoid(g_nsa).reshape(b_, s_, 3, N_HEADS),
                      rel_bias, cmp_pos, cmp_w1, cmp_w2)
    out_c = spatial_gating(jax.nn.gelu(u_c), jax.nn.gelu(v_c), sgu_w, sgu_b, sgu_norm_g, sgu_norm_b)
    merged = (jax.nn.sigmoid(g0) * (out_a @ w_br_a)
              + jax.nn.sigmoid(g1) * (out_b @ w_br_b)
              + jax.nn.sigmoid(g2) * (out_c @ w_br_c))
    return merged @ w_o


def conv_ffn(h, w_up, conv_w, w_down):
    gate, val = jnp.split(h @ w_up, 2, axis=-1)
    return (jax.nn.gelu(causal_dwconv3(gate, conv_w)) * val) @ w_down


def setup_inputs(seed: int = 0) -> dict:
    key = jax.random.key(seed)
    ks = jax.random.split(key, 22)
    nrm = lambda k, shape, scale: scale * jax.random.normal(k, shape, jnp.float32)
    L = DEPTH
    return {
        'x': nrm(ks[0], (BATCH, SEQ, D_MODEL), 1.0),
        'rel_bias': nrm(ks[1], (N_BUCKETS, N_HEADS), 0.1),
        'norm_mix': 1.0 + nrm(ks[2], (L, D_MODEL), 0.01),
        'w_in': nrm(ks[3], (L, D_MODEL, N_IN), D_MODEL ** -0.5),
        'conv_a': nrm(ks[4], (L, CONV_W, D_A), CONV_W ** -0.5),
        'cmp_pos': nrm(ks[5], (L, 2, CMP_LEN, HD), 0.1),
        'cmp_w1': nrm(ks[6], (L, 2, CMP_LEN * HD, CMP_HID), (CMP_LEN * HD) ** -0.5),
        'cmp_w2': nrm(ks[7], (L, 2, CMP_HID, HD), CMP_HID ** -0.5),
        'sgu_w': nrm(ks[8], (L, SGU_GROUPS, CHUNK, CHUNK), CHUNK ** -0.5),
        'sgu_b': 1.0 + nrm(ks[9], (L, SGU_GROUPS, CHUNK), 0.01),
        'sgu_norm_g': 1.0 + nrm(ks[10], (L, D_C), 0.01),
        'sgu_norm_b': nrm(ks[11], (L, D_C), 0.01),
        'w_br_a': nrm(ks[12], (L, D_A, D_MODEL), D_A ** -0.5),
        'w_br_b': nrm(ks[13], (L, D_B, D_MODEL), D_B ** -0.5),
        'w_br_c': nrm(ks[14], (L, D_C, D_MODEL), D_C ** -0.5),
        'w_o': nrm(ks[15], (L, D_MODEL, D_MODEL), D_MODEL ** -0.5),
        'norm_ffn': 1.0 + nrm(ks[16], (L, D_MODEL), 0.01),
        'ffn_w_up': nrm(ks[17], (L, D_MODEL, 2 * D_FF), D_MODEL ** -0.5),
        'ffn_conv': nrm(ks[18], (L, CONV_W, D_FF), CONV_W ** -0.5),
        'ffn_w_down': nrm(ks[19], (L, D_FF, D_MODEL), D_FF ** -0.5),
        'norm_final': 1.0 + nrm(ks[20], (D_MODEL,), 0.01),
    }


def reference(x, rel_bias, norm_mix, w_in, conv_a, cmp_pos, cmp_w1, cmp_w2, sgu_w, sgu_b,
              sgu_norm_g, sgu_norm_b, w_br_a, w_br_b, w_br_c, w_o, norm_ffn, ffn_w_up,
              ffn_conv, ffn_w_down, norm_final):
    for l in range(DEPTH):
        h = rms_norm(x, norm_mix[l])
        x = x + hybrid_mixer(h, rel_bias, w_in[l], conv_a[l], cmp_pos[l], cmp_w1[l], cmp_w2[l],
                             sgu_w[l], sgu_b[l], sgu_norm_g[l], sgu_norm_b[l],
                             w_br_a[l], w_br_b[l], w_br_c[l], w_o[l])
        h = rms_norm(x, norm_ffn[l])
        x = x + conv_ffn(h, ffn_w_up[l], ffn_conv[l], ffn_w_down[l])
    return rms_norm(x, norm_final)
```

```python
import functools
import math

import jax
import jax.numpy as jnp
from jax import lax
from jax.experimental import pallas as pl
from jax.experimental.pallas import tpu as pltpu

D_MODEL = 1024
HD = 64
N_HEADS = D_MODEL // HD
N_KV = 4
Q_PER_KV = N_HEADS // N_KV
SGU_GROUPS = 8
CHUNK = 128
CONV_W = 3
CMP_LEN = 32
CMP_STRIDE = 16
CMP_HID = 2 * HD
SEL_BLOCK = 64
SEL_TOP_N = 8
WINDOW = 512
Q_BLOCK = 128
D_FF = 3 * D_MODEL
N_BUCKETS = 32
MAX_EXACT = N_BUCKETS // 2
REL_MAX_DIST = 128
SCALE = HD ** -0.5
EPS = 1e-6
NEG_INF = -1e30
FORCE = 1e9

LANES = 128
HALO = 16
N_SEG = 9
N_MAIN = N_SEG * D_MODEL + 6 * N_KV * HD
KV_OFF = N_SEG * D_MODEL
VMEM_LIMIT = 56 * 1024 * 1024

BF16 = jnp.bfloat16
F32 = jnp.float32
_NT = (((1,), (1,)), ((), ()))


def _params(sem):
    return pltpu.CompilerParams(dimension_semantics=sem, vmem_limit_bytes=VMEM_LIMIT)


def _rms(x, g):
    return x * lax.rsqrt(jnp.mean(x * x, axis=-1, keepdims=True) + EPS) * g


def _in_proj_body(x_ref, g_ref, w_ref, wg_ref, o_ref, og_ref, h_ref):
    @pl.when(pl.program_id(1) == 0)
    def _():
        hb = _rms(x_ref[...], g_ref[...]).astype(BF16)
        h_ref[...] = hb
        og_ref[...] = jnp.dot(hb, wg_ref[...], preferred_element_type=F32)

    o_ref[...] = jnp.dot(h_ref[...], w_ref[...], preferred_element_type=F32).astype(o_ref.dtype)


def _in_proj(x2, g, w_main, w_gate, tm=1024, tn=512):
    m = x2.shape[0]
    return pl.pallas_call(
        _in_proj_body,
        grid=(m // tm, N_MAIN // tn),
        in_specs=[
            pl.BlockSpec((tm, D_MODEL), lambda i, j: (i, 0)),
            pl.BlockSpec((1, D_MODEL), lambda i, j: (0, 0)),
            pl.BlockSpec((D_MODEL, tn), lambda i, j: (0, j)),
            pl.BlockSpec((D_MODEL, LANES), lambda i, j: (0, 0)),
        ],
        out_specs=[
            pl.BlockSpec((tm, tn), lambda i, j: (i, j)),
            pl.BlockSpec((tm, LANES), lambda i, j: (i, 0)),
        ],
        out_shape=[
            jax.ShapeDtypeStruct((m, N_MAIN), BF16),
            jax.ShapeDtypeStruct((m, LANES), F32),
        ],
        scratch_shapes=[pltpu.VMEM((tm, D_MODEL), BF16)],
        compiler_params=_params(("parallel", "arbitrary")),
        name="in_proj",
    )(x2, g, w_main, w_gate)


def _rel_bucket(dist):
    dist = jnp.maximum(dist, 0)
    log_ratio = jnp.log(jnp.maximum(dist, 1).astype(F32) / MAX_EXACT) / math.log(REL_MAX_DIST / MAX_EXACT)
    large = MAX_EXACT + (log_ratio * (N_BUCKETS - MAX_EXACT)).astype(jnp.int32)
    return jnp.where(dist < MAX_EXACT, dist, jnp.minimum(large, N_BUCKETS - 1))


def _bias_body(tab_ref, bt_ref, bc_ref, tt_ref, bco_ref):
    h = pl.program_id(0)

    def lookup(bk):
        acc = jnp.zeros(bk.shape, F32)
        for b in range(N_BUCKETS):
            acc = jnp.where(bk == b, tab_ref[b, h], acc)
        return acc

    for d in range(3):
        tt_ref[d] = lookup(bt_ref[d])
    for i in range(bc_ref.shape[0]):
        bco_ref[i] = lookup(bc_ref[i])


def _bias_tiles(rel_bias, n_qb):
    kk = jnp.arange(LANES)[:, None]
    qq = jnp.arange(LANES)[None, :]
    bt = jnp.stack([_rel_bucket(LANES * d + qq - kk) for d in range(3)]).astype(jnp.int32)
    cmp_end = CMP_STRIDE * jnp.arange(LANES) + CMP_LEN - 1
    t = (Q_BLOCK * jnp.arange(n_qb))[:, None, None] + qq[None]
    bc = _rel_bucket(t - cmp_end[None, :, None]).astype(jnp.int32)
    return pl.pallas_call(
        _bias_body,
        grid=(N_HEADS,),
        in_specs=[
            pl.BlockSpec(memory_space=pltpu.SMEM),
            pl.BlockSpec((3, LANES, LANES), lambda h: (0, 0, 0)),
            pl.BlockSpec((n_qb, LANES, LANES), lambda h: (0, 0, 0)),
        ],
        out_specs=[
            pl.BlockSpec((None, 3, LANES, LANES), lambda h: (h // Q_PER_KV, 0, 0, h % Q_PER_KV)),
            pl.BlockSpec((None, n_qb, LANES, LANES), lambda h: (h // Q_PER_KV, 0, 0, h % Q_PER_KV)),
        ],
        out_shape=[
            jax.ShapeDtypeStruct((N_KV, 3, LANES, Q_PER_KV * LANES), F32),
            jax.ShapeDtypeStruct((N_KV, n_qb, LANES, Q_PER_KV * LANES), F32),
        ],
        compiler_params=_params(("arbitrary",)),
        name="bias_tiles",
    )(rel_bias.astype(F32), bt, bc)


def _compress_body(k3_ref, v3_ref, pos_ref, w1_ref, w2_ref, ko_ref, vo_ref):
    half = CMP_STRIDE * HD
    for which, (src, dst) in enumerate(((k3_ref, ko_ref), (v3_ref, vo_ref))):
        x = src[...].astype(F32)
        pos = pos_ref[which]
        xa = (x + pos[:, :half]).astype(BF16)
        xb = (x + pos[:, half:]).astype(BF16)
        a = jnp.dot(xa, w1_ref[which, :half, :], preferred_element_type=F32)
        b = jnp.dot(xb, w1_ref[which, half:, :], preferred_element_type=F32)
        hid = jax.nn.gelu(a + pltpu.roll(b, LANES - 1, 0))
        dst[...] = jnp.dot(hid.astype(BF16), w2_ref[which], preferred_element_type=F32).astype(dst.dtype)


def _compress(k3, v3, pos, w1, w2):
    b_, g_, nm, width = k3.shape
    spec3 = pl.BlockSpec((None, None, nm, width), lambda b, g: (b, g, 0, 0))
    ospec = pl.BlockSpec((None, None, nm, HD), lambda b, g: (b, g, 0, 0))
    return pl.pallas_call(
        _compress_body,
        grid=(b_, g_),
        in_specs=[
            spec3, spec3,
            pl.BlockSpec((2, 1, CMP_LEN * HD), lambda b, g: (0, 0, 0)),
            pl.BlockSpec((2, CMP_LEN * HD, CMP_HID), lambda b, g: (0, 0, 0)),
            pl.BlockSpec((2, CMP_HID, HD), lambda b, g: (0, 0, 0)),
        ],
        out_specs=[ospec, ospec],
        out_shape=[jax.ShapeDtypeStruct((b_, g_, nm, HD), BF16)] * 2,
        compiler_params=_params(("parallel", "parallel")),
        name="compress",
    )(k3, v3, pos, w1, w2)


def _softmax_cols(s, ok):
    s = jnp.where(ok, s, NEG_INF)
    m = jnp.max(s, axis=0, keepdims=True)
    p = jnp.where(ok, jnp.exp(s - m), 0.0)
    l = jnp.sum(p, axis=0, keepdims=True)
    return p / jnp.where(l > 0.0, l, 1.0)


def _nsa_body(q_ref, kc_ref, vct_ref, ks_ref, vst_ref, kw_ref, vwt_ref, gt_ref, tt_ref, bc_ref, ovt_ref,
              o_ref, sel_ref):
    i = pl.program_id(2)
    t0 = i * Q_BLOCK
    nl = Q_PER_KV * LANES
    qt = q_ref[...]
    q = jnp.concatenate([qt[:, r * HD:(r + 1) * HD] for r in range(Q_PER_KV)], axis=0) * SCALE

    s = lax.dot_general(kc_ref[...], q, _NT, preferred_element_type=F32) + bc_ref[...]
    n_io = lax.broadcasted_iota(jnp.int32, (LANES, nl), 0)
    tq = t0 + (lax.broadcasted_iota(jnp.int32, (LANES, nl), 1) & (LANES - 1))
    n_cmp_pad = kc_ref.shape[0]
    n_cmp = (ks_ref.shape[0] - CMP_LEN) // CMP_STRIDE + 1
    ok_c = (tq - (n_io * CMP_STRIDE + CMP_LEN - 1) >= 0) & (n_io < n_cmp)
    p_c = _softmax_cols(s, ok_c)
    o_c = jnp.dot(vct_ref[...], p_c.astype(BF16), preferred_element_type=F32)

    p_sum = p_c[:, 0:LANES]
    for r in range(1, Q_PER_KV):
        p_sum = p_sum + p_c[:, r * LANES:(r + 1) * LANES]
    imp = jnp.dot(ovt_ref[...], p_sum, precision=lax.Precision.HIGHEST, preferred_element_type=F32)
    n_blk = imp.shape[0]
    j_io = lax.broadcasted_iota(jnp.int32, (n_blk, LANES), 0)
    t_b = t0 + lax.broadcasted_iota(jnp.int32, (n_blk, LANES), 1)
    cur = t_b >> 6
    forced = (j_io == 0) | (j_io == cur) | (j_io == cur - 1)
    valid = j_io * SEL_BLOCK <= t_b
    score = jnp.where(forced, FORCE, jnp.where(valid, imp, -FORCE))
    rank = jnp.zeros((n_blk, LANES), jnp.int32)
    for ii in range(n_blk):
        row = score[ii:ii + 1, :]
        beats = (row > score) | ((row == score) & (j_io > ii))
        rank = rank + beats.astype(jnp.int32)
    keep = ((rank < min(SEL_TOP_N, n_blk)) & (score >= 0.0)).astype(F32)
    sel_ref[...] = jnp.concatenate([keep] * Q_PER_KV, axis=1)

    ik = n_io
    iq = tq - t0

    def attend(k_ref, vt_ref, mask_fn, lo, hi):
        def step(kt, carry):
            m, l, acc = carry
            k0 = pl.multiple_of(kt * LANES, LANES)
            s = lax.dot_general(k_ref[pl.ds(k0, LANES), :], q, _NT, preferred_element_type=F32)
            s = s + tt_ref[jnp.minimum(i - kt, 2)]
            ok = mask_fn(kt, (t0 + iq) - (k0 + ik))
            s = jnp.where(ok, s, NEG_INF)
            m_new = jnp.maximum(m, jnp.max(s, axis=0, keepdims=True))
            alpha = jnp.exp(m - m_new)
            p = jnp.where(ok, jnp.exp(s - m_new), 0.0)
            l = alpha * l + jnp.sum(p, axis=0, keepdims=True)
            acc = alpha * acc + jnp.dot(vt_ref[:, pl.ds(k0, LANES)], p.astype(BF16),
                                        preferred_element_type=F32)
            return m_new, l, acc

        init = (jnp.full((1, nl), NEG_INF, F32), jnp.zeros((1, nl), F32), jnp.zeros((HD, nl), F32))
        _, l, acc = lax.fori_loop(lo, hi, step, init)
        return acc / jnp.where(l > 0.0, l, 1.0)

    def sel_mask(kt, d):
        ra = sel_ref[pl.ds(2 * kt, 1), :]
        rb = sel_ref[pl.ds(2 * kt + 1, 1), :]
        return (jnp.where(ik < SEL_BLOCK, ra, rb) > 0.5) & (d >= 0)

    def win_mask(kt, d):
        return (d >= 0) & (d < WINDOW)

    o_s = attend(ks_ref, vst_ref, sel_mask, 0, i + 1)
    o_w = attend(kw_ref, vwt_ref, win_mask, jnp.maximum(i - WINDOW // LANES, 0), i + 1)

    g = jax.nn.sigmoid(gt_ref[...])
    o_t = g[0:1] * o_c + g[1:2] * o_s + g[2:3] * o_w
    for pair in range(Q_PER_KV // 2):
        blk = jnp.concatenate([o_t[:, (2 * pair) * LANES:(2 * pair + 1) * LANES],
                               o_t[:, (2 * pair + 1) * LANES:(2 * pair + 2) * LANES]], axis=0)
        o_ref[:, pair * LANES:(pair + 1) * LANES] = blk.T.astype(o_ref.dtype)


def _nsa(proj, kc, vct, ks, vst, kw, vwt, gt, tt, bc, ovt):
    b_, g_, s_, _ = ks.shape
    n_qb = s_ // Q_BLOCK
    m = proj.shape[0]
    nl = Q_PER_KV * LANES
    qcol = 3 * D_MODEL // (Q_PER_KV * HD)

    def bg(shape):
        return pl.BlockSpec((None, None) + shape, lambda b, g, i: (b, g, 0, 0))

    return pl.pallas_call(
        _nsa_body,
        grid=(b_, g_, n_qb),
        in_specs=[
            pl.BlockSpec((Q_BLOCK, Q_PER_KV * HD), lambda b, g, i: (b * n_qb + i, qcol + g)),
            bg((LANES, HD)), bg((HD, LANES)),
            bg((s_, HD)), bg((HD, s_)),
            bg((s_, HD)), bg((HD, s_)),
            pl.BlockSpec((None, None, None, 3, nl), lambda b, g, i: (b, g, i, 0, 0)),
            pl.BlockSpec((None, 3, LANES, nl), lambda b, g, i: (g, 0, 0, 0)),
            pl.BlockSpec((None, None, LANES, nl), lambda b, g, i: (g, i, 0, 0)),
            pl.BlockSpec((s_ // SEL_BLOCK, LANES), lambda b, g, i: (0, 0)),
        ],
        out_specs=pl.BlockSpec((Q_BLOCK, Q_PER_KV * HD), lambda b, g, i: (b * n_qb + i, g)),
        out_shape=jax.ShapeDtypeStruct((m, D_MODEL), BF16),
        scratch_shapes=[pltpu.VMEM((s_ // SEL_BLOCK, nl), F32)],
        compiler_params=_params(("parallel", "parallel", "arbitrary")),
        name="nsa",
    )(proj, kc, vct, ks, vst, kw, vwt, gt, tt, bc, ovt)


def _mix_body(gb_ref, gc_ref, va_ref, gch_ref, vah_ref, uc_ref, vc_ref, g0_ref, g1_ref, g2_ref, ob_ref, x_ref,
              ca_ref, sw_ref, sb_ref, lng_ref, lnb_ref, wa_ref, wb_ref, wc_ref, wo_ref, o_ref, xa_ref, mx_ref):
    tm = gb_ref.shape[0]
    first = pl.program_id(1) == 0

    xa = gc_ref[...].astype(F32) * va_ref[...].astype(F32)
    halo = gch_ref[...].astype(F32) * vah_ref[...].astype(F32)
    xa_ref[0:HALO, :] = jnp.where(first, 0.0, halo)
    xa_ref[HALO:, :] = xa
    ca = ca_ref[...]
    conv = (ca[0:1] * xa_ref[pl.ds(HALO - 2, tm), :] + ca[1:2] * xa_ref[pl.ds(HALO - 1, tm), :]
            + ca[2:3] * xa)
    out_a = gb_ref[...].astype(F32) * conv
    y_a = jnp.dot(out_a.astype(BF16), wa_ref[...], preferred_element_type=F32)

    v = jax.nn.gelu(vc_ref[...].astype(F32))
    mu = jnp.mean(v, axis=-1, keepdims=True)
    vz = v - mu
    var = jnp.mean(vz * vz, axis=-1, keepdims=True)
    vn = (vz * lax.rsqrt(var + EPS) * lng_ref[...] + lnb_ref[...]).astype(BF16)
    row = lax.broadcasted_iota(jnp.int32, (CHUNK, CHUNK), 0)
    col = lax.broadcasted_iota(jnp.int32, (CHUNK, CHUNK), 1)
    for gi in range(SGU_GROUPS):
        w = jnp.where(row >= col, sw_ref[gi], 0.0).astype(BF16)
        for c in range(tm // CHUNK):
            blk = vn[c * CHUNK:(c + 1) * CHUNK, gi * LANES:(gi + 1) * LANES]
            mx_ref[c * CHUNK:(c + 1) * CHUNK, gi * LANES:(gi + 1) * LANES] = (
                jnp.dot(w, blk, preferred_element_type=F32) + sb_ref[:, gi * LANES:(gi + 1) * LANES])
    out_c = jax.nn.gelu(uc_ref[...].astype(F32)) * mx_ref[...]
    y_c = jnp.dot(out_c.astype(BF16), wc_ref[...], preferred_element_type=F32)

    y_b = jnp.dot(ob_ref[...], wb_ref[...], preferred_element_type=F32)

    merged = (jax.nn.sigmoid(g0_ref[...].astype(F32)) * y_a
              + jax.nn.sigmoid(g1_ref[...].astype(F32)) * y_b
              + jax.nn.sigmoid(g2_ref[...].astype(F32)) * y_c)
    o_ref[...] = x_ref[...] + jnp.dot(merged.astype(BF16), wo_ref[...], preferred_element_type=F32)


def _mix(proj, out_b, x2, seq, conv_a, sgu_w, sgu_b2, ln_g, ln_b, w_a, w_b, w_c, w_o, tm=256):
    m = x2.shape[0]
    n_t = seq // tm
    hb = tm // HALO

    def seg(k):
        return pl.BlockSpec((tm, D_MODEL), lambda b, i: (b * n_t + i, k))

    def halo(k):
        return pl.BlockSpec((HALO, D_MODEL), lambda b, i: (jnp.maximum((b * n_t + i) * hb - 1, 0), k))

    def full(shape):
        return pl.BlockSpec(shape, lambda b, i: (0,) * len(shape))

    row = pl.BlockSpec((tm, D_MODEL), lambda b, i: (b * n_t + i, 0))
    wspec = full((D_MODEL, D_MODEL))
    return pl.pallas_call(
        _mix_body,
        grid=(m // seq, n_t),
        in_specs=[
            seg(0), seg(1), seg(2), halo(1), halo(2), seg(4), seg(5), seg(6), seg(7), seg(8), row, row,
            full((CONV_W, D_MODEL)), full((SGU_GROUPS, CHUNK, CHUNK)), full((CHUNK, D_MODEL)),
            full((1, D_MODEL)), full((1, D_MODEL)), wspec, wspec, wspec, wspec,
        ],
        out_specs=row,
        out_shape=jax.ShapeDtypeStruct((m, D_MODEL), F32),
        scratch_shapes=[pltpu.VMEM((tm + HALO, D_MODEL), F32), pltpu.VMEM((tm, D_MODEL), F32)],
        compiler_params=_params(("parallel", "arbitrary")),
        name="mix",
    )(proj, proj, proj, proj, proj, proj, proj, proj, proj, proj, out_b, x2,
      conv_a, sgu_w, sgu_b2, ln_g, ln_b, w_a, w_b, w_c, w_o)


def _ffn_body(x_ref, xh_ref, g_ref, wg_ref, wv_ref, cw_ref, wd_ref, o_ref, h_ref, gt_ref, acc_ref, *, tiles_per_seq):
    i = pl.program_id(0)
    j = pl.program_id(1)
    tm = x_ref.shape[0]

    @pl.when(j == 0)
    def _():
        h_ref[0:HALO, :] = _rms(xh_ref[...], g_ref[...]).astype(BF16)
        h_ref[HALO:, :] = _rms(x_ref[...], g_ref[...]).astype(BF16)
        acc_ref[...] = jnp.zeros_like(acc_ref)

    gate = jnp.dot(h_ref[...], wg_ref[...], preferred_element_type=F32)
    rows = lax.broadcasted_iota(jnp.int32, gate.shape, 0)
    seq_start = (i % tiles_per_seq) == 0
    gt_ref[...] = jnp.where(seq_start & (rows < HALO), 0.0, gate)
    cw = cw_ref[...]
    conv = (cw[0:1] * gt_ref[pl.ds(HALO - 2, tm), :] + cw[1:2] * gt_ref[pl.ds(HALO - 1, tm), :]
            + cw[2:3] * gt_ref[pl.ds(HALO, tm), :])
    val = jnp.dot(h_ref[HALO:, :], wv_ref[...], preferred_element_type=F32)
    act = (jax.nn.gelu(conv) * val).astype(BF16)
    acc_ref[...] += jnp.dot(act, wd_ref[...], preferred_element_type=F32)

    @pl.when(j == pl.num_programs(1) - 1)
    def _():
        o_ref[...] = x_ref[...] + acc_ref[...]


def _ffn(x2, seq, g, w_up, conv_w, w_down, tm=512, tf=512):
    m = x2.shape[0]
    n_f = D_FF // tf
    hb = tm // HALO
    return pl.pallas_call(
        functools.partial(_ffn_body, tiles_per_seq=seq // tm),
        grid=(m // tm, n_f),
        in_specs=[
            pl.BlockSpec((tm, D_MODEL), lambda i, j: (i, 0)),
            pl.BlockSpec((HALO, D_MODEL), lambda i, j: (jnp.maximum(i * hb - 1, 0), 0)),
            pl.BlockSpec((1, D_MODEL), lambda i, j: (0, 0)),
            pl.BlockSpec((D_MODEL, tf), lambda i, j: (0, j)),
            pl.BlockSpec((D_MODEL, tf), lambda i, j: (0, n_f + j)),
            pl.BlockSpec((CONV_W, tf), lambda i, j: (0, j)),
            pl.BlockSpec((tf, D_MODEL), lambda i, j: (j, 0)),
        ],
        out_specs=pl.BlockSpec((tm, D_MODEL), lambda i, j: (i, 0)),
        out_shape=jax.ShapeDtypeStruct((m, D_MODEL), F32),
        scratch_shapes=[
            pltpu.VMEM((tm + HALO, D_MODEL), BF16),
            pltpu.VMEM((tm + HALO, tf), F32),
            pltpu.VMEM((tm, D_MODEL), F32),
        ],
        compiler_params=_params(("parallel", "arbitrary")),
        name="ffn",
    )(x2, x2, g, w_up, w_up, conv_w, w_down)


def _norm_body(x_ref, g_ref, o_ref):
    o_ref[...] = _rms(x_ref[...], g_ref[...])


def _final_norm(x2, g, tm=1024):
    m = x2.shape[0]
    return pl.pallas_call(
        _norm_body,
        grid=(m // tm,),
        in_specs=[pl.BlockSpec((tm, D_MODEL), lambda i: (i, 0)), pl.BlockSpec((1, D_MODEL), lambda i: (0, 0))],
        out_specs=pl.BlockSpec((tm, D_MODEL), lambda i: (i, 0)),
        out_shape=jax.ShapeDtypeStruct((m, D_MODEL), F32),
        compiler_params=_params(("parallel",)),
        name="final_norm",
    )(x2, g)


def _split_w_in(w_in):
    c = [0]
    for sz in [D_MODEL] * 4 + [N_KV * HD] * 6 + [3 * N_HEADS] + [D_MODEL] * 5:
        c.append(c[-1] + sz)
    main = jnp.concatenate([w_in[:, c[0]:c[4]], w_in[:, c[11]:c[16]], w_in[:, c[4]:c[10]]], axis=1)
    gate = jnp.pad(w_in[:, c[10]:c[11]], ((0, 0), (0, LANES - 3 * N_HEADS)))
    return main.astype(BF16), gate.astype(BF16)


def _mixer_layer(x2, b_, s_, tt, bc, ovt, norm_g, w_in, conv_a, cmp_pos, cmp_w1, cmp_w2, sgu_w, sgu_b,
                 sgu_norm_g, sgu_norm_b, w_br_a, w_br_b, w_br_c, w_o):
    n_qb = s_ // Q_BLOCK
    w_main, w_gate = _split_w_in(w_in)
    proj, g_nsa = _in_proj(x2, norm_g.reshape(1, D_MODEL), w_main, w_gate)

    kv = proj[:, KV_OFF:].reshape(b_, s_, 6, N_KV, HD)
    kc, vc, ks, vs, kw, vw = [kv[:, :, n] for n in range(6)]

    def rows16(a):
        return a.transpose(0, 2, 1, 3).reshape(b_, N_KV, s_ // CMP_STRIDE, CMP_STRIDE * HD)

    k_cmp, v_cmp = _compress(rows16(kc), rows16(vc), cmp_pos.reshape(2, 1, CMP_LEN * HD),
                             cmp_w1.astype(BF16), cmp_w2.astype(BF16))
    gt = (g_nsa[:, :3 * N_HEADS].reshape(b_, n_qb, Q_BLOCK, 3, N_KV, Q_PER_KV)
          .transpose(0, 4, 1, 3, 5, 2).reshape(b_, N_KV, n_qb, 3, Q_PER_KV * Q_BLOCK))
    out_b = _nsa(proj, k_cmp, v_cmp.transpose(0, 1, 3, 2),
                 ks.transpose(0, 2, 1, 3), vs.transpose(0, 2, 3, 1),
                 kw.transpose(0, 2, 1, 3), vw.transpose(0, 2, 3, 1), gt, tt, bc, ovt)

    sgu_b2 = jnp.broadcast_to(sgu_b.T[:, :, None], (CHUNK, SGU_GROUPS, D_MODEL // SGU_GROUPS)).reshape(CHUNK, D_MODEL)
    return _mix(proj, out_b, x2, s_, conv_a, sgu_w, sgu_b2,
                sgu_norm_g.reshape(1, D_MODEL), sgu_norm_b.reshape(1, D_MODEL),
                w_br_a.astype(BF16), w_br_b.astype(BF16), w_br_c.astype(BF16), w_o.astype(BF16))


def _overlap_t(s_):
    n_blk = s_ // SEL_BLOCK
    cmp_start = jnp.arange(LANES) * CMP_STRIDE
    cmp_end = cmp_start + CMP_LEN - 1
    blk_start = jnp.arange(n_blk) * SEL_BLOCK
    n_cmp = (s_ - CMP_LEN) // CMP_STRIDE + 1
    ov = ((cmp_start[None, :] < blk_start[:, None] + SEL_BLOCK) & (cmp_end[None, :] >= blk_start[:, None])
          & (jnp.arange(LANES)[None, :] < n_cmp))
    return ov.astype(F32)


def kernel(x, rel_bias, norm_mix, w_in, conv_a, cmp_pos, cmp_w1, cmp_w2, sgu_w, sgu_b, sgu_norm_g, sgu_norm_b,
           w_br_a, w_br_b, w_br_c, w_o, norm_ffn, ffn_w_up, ffn_conv, ffn_w_down, norm_final):
    b_, s_, _ = x.shape
    depth = w_in.shape[0]
    x2 = x.reshape(b_ * s_, D_MODEL)
    tt, bc = _bias_tiles(rel_bias, s_ // Q_BLOCK)
    ovt = _overlap_t(s_)
    for l in range(depth):
        x2 = _mixer_layer(x2, b_, s_, tt, bc, ovt, norm_mix[l], w_in[l], conv_a[l], cmp_pos[l], cmp_w1[l],
                          cmp_w2[l], sgu_w[l], sgu_b[l], sgu_norm_g[l], sgu_norm_b[l],
                          w_br_a[l], w_br_b[l], w_br_c[l], w_o[l])
        x2 = _ffn(x2, s_, norm_ffn[l].reshape(1, D_MODEL), ffn_w_up[l].astype(BF16), ffn_conv[l],
                  ffn_w_down[l].astype(BF16))
    return _final_norm(x2, norm_final.reshape(1, D_MODEL)).reshape(b_, s_, D_MODEL)
```

```python
import functools
import math

import jax
import jax.numpy as jnp
from jax import lax
from jax.experimental import pallas as pl
from jax.experimental.pallas import tpu as pltpu

D_MODEL = 1024
HD = 64
N_HEADS = D_MODEL // HD
N_KV = 4
Q_PER_KV = N_HEADS // N_KV
SGU_GROUPS = 8
CHUNK = 128
CONV_W = 3
CMP_LEN = 32
CMP_STRIDE = 16
CMP_HID = 2 * HD
SEL_BLOCK = 64
SEL_TOP_N = 8
WINDOW = 512
Q_BLOCK = 128
D_FF = 3 * D_MODEL
N_BUCKETS = 32
MAX_EXACT = N_BUCKETS // 2
REL_MAX_DIST = 128
SCALE = HD ** -0.5
EPS = 1e-6
NEG_INF = -1e30
FORCE = 1e9

LANES = 128
HALO = 16
N_SEG = 9
N_MAIN = N_SEG * D_MODEL + 6 * N_KV * HD
KV_OFF = N_SEG * D_MODEL
VMEM_LIMIT = 56 * 1024 * 1024

BF16 = jnp.bfloat16
F32 = jnp.float32
_NT = (((1,), (1,)), ((), ()))


def _params(sem):
    return pltpu.CompilerParams(dimension_semantics=sem, vmem_limit_bytes=VMEM_LIMIT)


def _rms(x, g):
    return x * lax.rsqrt(jnp.mean(x * x, axis=-1, keepdims=True) + EPS) * g


def _in_proj_body(x_ref, g_ref, w_ref, wg_ref, o_ref, og_ref, h_ref):
    @pl.when(pl.program_id(1) == 0)
    def _():
        hb = _rms(x_ref[...], g_ref[...]).astype(BF16)
        h_ref[...] = hb
        og_ref[...] = jnp.dot(hb, wg_ref[...], preferred_element_type=F32)

    o_ref[...] = jnp.dot(h_ref[...], w_ref[...], preferred_element_type=F32).astype(o_ref.dtype)


def _in_proj(x2, g, w_main, w_gate, tm=1024, tn=512):
    m = x2.shape[0]
    return pl.pallas_call(
        _in_proj_body,
        grid=(m // tm, N_MAIN // tn),
        in_specs=[
            pl.BlockSpec((tm, D_MODEL), lambda i, j: (i, 0)),
            pl.BlockSpec((1, D_MODEL), lambda i, j: (0, 0)),
            pl.BlockSpec((D_MODEL, tn), lambda i, j: (0, j)),
            pl.BlockSpec((D_MODEL, LANES), lambda i, j: (0, 0)),
        ],
        out_specs=[
            pl.BlockSpec((tm, tn), lambda i, j: (i, j)),
            pl.BlockSpec((tm, LANES), lambda i, j: (i, 0)),
        ],
        out_shape=[
            jax.ShapeDtypeStruct((m, N_MAIN), BF16),
            jax.ShapeDtypeStruct((m, LANES), F32),
        ],
        scratch_shapes=[pltpu.VMEM((tm, D_MODEL), BF16)],
        compiler_params=_params(("parallel", "arbitrary")),
        name="in_proj",
    )(x2, g, w_main, w_gate)


def _rel_bucket(dist):
    dist = jnp.maximum(dist, 0)
    log_ratio = jnp.log(jnp.maximum(dist, 1).astype(F32) / MAX_EXACT) / math.log(REL_MAX_DIST / MAX_EXACT)
    large = MAX_EXACT + (log_ratio * (N_BUCKETS - MAX_EXACT)).astype(jnp.int32)
    return jnp.where(dist < MAX_EXACT, dist, jnp.minimum(large, N_BUCKETS - 1))


def _bias_body(tab_ref, bt_ref, bc_ref, tt_ref, bco_ref):
    h = pl.program_id(0)

    def lookup(bk):
        acc = jnp.zeros(bk.shape, F32)
        for b in range(N_BUCKETS):
            acc = jnp.where(bk == b, tab_ref[b, h], acc)
        return acc

    far = lookup(bt_ref[2])
    for d in range(2):
        tt_ref[d] = lookup(bt_ref[d]) - far
    for i in range(bc_ref.shape[0]):
        bco_ref[i] = lookup(bc_ref[i])


def _bias_tiles(rel_bias, n_qb):
    kk = jnp.arange(LANES)[:, None]
    qq = jnp.arange(LANES)[None, :]
    bt = jnp.stack([_rel_bucket(LANES * d + qq - kk) for d in range(3)]).astype(jnp.int32)
    cmp_end = CMP_STRIDE * jnp.arange(LANES) + CMP_LEN - 1
    t = (Q_BLOCK * jnp.arange(n_qb))[:, None, None] + qq[None]
    bc = _rel_bucket(t - cmp_end[None, :, None]).astype(jnp.int32)
    return pl.pallas_call(
        _bias_body,
        grid=(N_HEADS,),
        in_specs=[
            pl.BlockSpec(memory_space=pltpu.SMEM),
            pl.BlockSpec((3, LANES, LANES), lambda h: (0, 0, 0)),
            pl.BlockSpec((n_qb, LANES, LANES), lambda h: (0, 0, 0)),
        ],
        out_specs=[
            pl.BlockSpec((None, 2, LANES, LANES), lambda h: (h // Q_PER_KV, 0, 0, h % Q_PER_KV)),
            pl.BlockSpec((None, n_qb, LANES, LANES), lambda h: (h // Q_PER_KV, 0, 0, h % Q_PER_KV)),
        ],
        out_shape=[
            jax.ShapeDtypeStruct((N_KV, 2, LANES, Q_PER_KV * LANES), F32),
            jax.ShapeDtypeStruct((N_KV, n_qb, LANES, Q_PER_KV * LANES), F32),
        ],
        compiler_params=_params(("arbitrary",)),
        name="bias_tiles",
    )(rel_bias.astype(F32), bt, bc)


def _compress_body(k3_ref, v3_ref, pos_ref, w1_ref, w2_ref, ko_ref, vo_ref):
    half = CMP_STRIDE * HD
    for which, (src, dst) in enumerate(((k3_ref, ko_ref), (v3_ref, vo_ref))):
        x = src[...].astype(F32)
        pos = pos_ref[which]
        xa = (x + pos[:, :half]).astype(BF16)
        xb = (x + pos[:, half:]).astype(BF16)
        a = jnp.dot(xa, w1_ref[which, :half, :], preferred_element_type=F32)
        b = jnp.dot(xb, w1_ref[which, half:, :], preferred_element_type=F32)
        hid = jax.nn.gelu(a + pltpu.roll(b, LANES - 1, 0))
        dst[...] = jnp.dot(hid.astype(BF16), w2_ref[which], preferred_element_type=F32).astype(dst.dtype)


def _compress(k3, v3, pos, w1, w2):
    b_, g_, nm, width = k3.shape
    spec3 = pl.BlockSpec((None, None, nm, width), lambda b, g: (b, g, 0, 0))
    ospec = pl.BlockSpec((None, None, nm, HD), lambda b, g: (b, g, 0, 0))
    return pl.pallas_call(
        _compress_body,
        grid=(b_, g_),
        in_specs=[
            spec3, spec3,
            pl.BlockSpec((2, 1, CMP_LEN * HD), lambda b, g: (0, 0, 0)),
            pl.BlockSpec((2, CMP_LEN * HD, CMP_HID), lambda b, g: (0, 0, 0)),
            pl.BlockSpec((2, CMP_HID, HD), lambda b, g: (0, 0, 0)),
        ],
        out_specs=[ospec, ospec],
        out_shape=[jax.ShapeDtypeStruct((b_, g_, nm, HD), BF16)] * 2,
        compiler_params=_params(("parallel", "parallel")),
        name="compress",
    )(k3, v3, pos, w1, w2)


def _softmax_cols(s, ok):
    s = jnp.where(ok, s, NEG_INF)
    m = jnp.max(s, axis=0, keepdims=True)
    p = jnp.where(ok, jnp.exp(s - m), 0.0)
    l = jnp.sum(p, axis=0, keepdims=True)
    return p / jnp.where(l > 0.0, l, 1.0)


FAR_TILES = 4
PAD_COL = 32


def _nsa_body(q_ref, kc_ref, vct_ref, ksa_ref, vst_ref, kwa_ref, vwt_ref, gt_ref, tt_ref, bc_ref, ovt_ref, o_ref):
    i = pl.program_id(2)
    t0 = i * Q_BLOCK
    nl = Q_PER_KV * LANES
    n_win = WINDOW // LANES + 1
    qt = q_ref[...]
    q = jnp.concatenate([qt[:, r * HD:(r + 1) * HD] for r in range(Q_PER_KV)], axis=0) * SCALE

    s = lax.dot_general(kc_ref[...], q, _NT, preferred_element_type=F32) + bc_ref[...]
    ik = lax.broadcasted_iota(jnp.int32, (LANES, nl), 0)
    iq = lax.broadcasted_iota(jnp.int32, (LANES, nl), 1) & (LANES - 1)
    n_cmp = (vst_ref.shape[1] - LANES - CMP_LEN) // CMP_STRIDE + 1
    ok_c = (t0 + iq - (ik * CMP_STRIDE + CMP_LEN - 1) >= 0) & (ik < n_cmp)
    p_c = _softmax_cols(s, ok_c)
    o_c = jnp.dot(vct_ref[...], p_c.astype(BF16), preferred_element_type=F32)

    p_sum = p_c[:, 0:LANES]
    for r in range(1, Q_PER_KV):
        p_sum = p_sum + p_c[:, r * LANES:(r + 1) * LANES]
    imp = jnp.dot(ovt_ref[...], p_sum, precision=lax.Precision.HIGHEST, preferred_element_type=F32)
    n_blk = imp.shape[0]
    j_io = lax.broadcasted_iota(jnp.int32, (n_blk, LANES), 0)
    t_b = t0 + lax.broadcasted_iota(jnp.int32, (n_blk, LANES), 1)
    cur = t_b >> 6
    forced = (j_io == 0) | (j_io == cur) | (j_io == cur - 1)
    valid = j_io * SEL_BLOCK <= t_b
    score = jnp.where(forced, FORCE, jnp.where(valid, imp, -FORCE))
    rank = jnp.zeros((n_blk, LANES), jnp.int32)
    for ii in range(n_blk):
        row = score[ii:ii + 1, :]
        beats = (row > score) | ((row == score) & (j_io > ii))
        rank = rank + beats.astype(jnp.int32)
    keep = (rank < min(SEL_TOP_N, n_blk)) & (score >= 0.0)

    def augment(kept):
        cols = jnp.concatenate([jnp.where(kept, 0.0, NEG_INF), jnp.full((8, LANES), NEG_INF, F32),
                                jnp.zeros((LANES - n_blk - 8, LANES), F32)], axis=0)
        cols_t = cols.T[:, 0:HD].astype(BF16)
        return jnp.concatenate([q, jnp.concatenate([cols_t] * Q_PER_KV, axis=0)], axis=1)

    q_near = augment(keep)
    q_far = augment(keep & (j_io < 2 * (i - 1)))

    def softmax_av(tiles, vt):
        s_all = jnp.concatenate(tiles, axis=0)
        m = jnp.max(s_all, axis=0, keepdims=True)
        p = jnp.exp(s_all - m)
        l = jnp.sum(p, axis=0, keepdims=True)
        return m, l, jnp.dot(vt, p.astype(BF16), preferred_element_type=F32)

    def logit_tiles(k_ref, qq, row0, n):
        s_all = lax.dot_general(k_ref[pl.ds(row0, n * LANES), :], qq, _NT, preferred_element_type=F32)
        return [s_all[t * LANES:(t + 1) * LANES] for t in range(n)]

    r0 = pl.multiple_of(i * LANES, LANES)
    causal = iq >= ik

    tw = logit_tiles(kwa_ref, q_near, r0, n_win)
    tw[0] = jnp.where(ik > iq, tw[0], NEG_INF)
    tw[n_win - 2] = tw[n_win - 2] + tt_ref[1]
    tw[n_win - 1] = jnp.where(causal, tw[n_win - 1] + tt_ref[0], NEG_INF)
    _, l_w, a_w = softmax_av(tw, vwt_ref[:, pl.ds(r0, n_win * LANES)])
    o_w = a_w / l_w

    tn = logit_tiles(ksa_ref, q_near, r0, 2)
    tn[0] = tn[0] + tt_ref[1]
    tn[1] = jnp.where(causal, tn[1] + tt_ref[0], NEG_INF)
    m_n, l_n, a_n = softmax_av(tn, vst_ref[:, pl.ds(r0, 2 * LANES)])

    def far(c, carry):
        m_old, l_old, a_old = carry
        rc = pl.multiple_of((c * FAR_TILES + 1) * LANES, LANES)
        s_all = lax.dot_general(ksa_ref[pl.ds(rc, FAR_TILES * LANES), :], q_far, _NT, preferred_element_type=F32)
        m_new = jnp.maximum(m_old, jnp.max(s_all, axis=0, keepdims=True))
        alpha = jnp.exp(m_old - m_new)
        p = jnp.exp(s_all - m_new)
        l_new = alpha * l_old + jnp.sum(p, axis=0, keepdims=True)
        a_new = alpha * a_old + jnp.dot(vst_ref[:, pl.ds(rc, FAR_TILES * LANES)], p.astype(BF16),
                                        preferred_element_type=F32)
        return m_new, l_new, a_new

    n_far = jnp.maximum(i - 1, 0)
    _, l_s, a_s = lax.fori_loop(0, (n_far + FAR_TILES - 1) // FAR_TILES, far, (m_n, l_n, a_n))
    o_s = a_s / l_s

    g = jax.nn.sigmoid(gt_ref[...])
    o_t = g[0:1] * o_c + g[1:2] * o_s + g[2:3] * o_w
    for pair in range(Q_PER_KV // 2):
        blk = jnp.concatenate([o_t[:, (2 * pair) * LANES:(2 * pair + 1) * LANES],
                               o_t[:, (2 * pair + 1) * LANES:(2 * pair + 2) * LANES]], axis=0)
        o_ref[:, pair * LANES:(pair + 1) * LANES] = blk.T.astype(o_ref.dtype)


def _nsa(proj, kc, vct, ksa, vst, kwa, vwt, gt, tt, bc, ovt, s_):
    b_, g_ = kwa.shape[:2]
    n_qb = s_ // Q_BLOCK
    m = proj.shape[0]
    nl = Q_PER_KV * LANES
    qcol = 3 * D_MODEL // (Q_PER_KV * HD)

    def bg(shape):
        return pl.BlockSpec((None, None) + shape, lambda b, g, i: (b, g, 0, 0))

    return pl.pallas_call(
        _nsa_body,
        grid=(b_, g_, n_qb),
        in_specs=[
            pl.BlockSpec((Q_BLOCK, Q_PER_KV * HD), lambda b, g, i: (b * n_qb + i, qcol + g)),
            bg((LANES, HD)), bg((HD, LANES)),
            bg((s_ + LANES, 2 * HD)), bg((HD, s_ + LANES)),
            bg((s_ + WINDOW, 2 * HD)), bg((HD, s_ + WINDOW)),
            pl.BlockSpec((None, None, None, 3, nl), lambda b, g, i: (b, g, i, 0, 0)),
            pl.BlockSpec((None, 2, LANES, nl), lambda b, g, i: (g, 0, 0, 0)),
            pl.BlockSpec((None, None, LANES, nl), lambda b, g, i: (g, i, 0, 0)),
            pl.BlockSpec((s_ // SEL_BLOCK, LANES), lambda b, g, i: (0, 0)),
        ],
        out_specs=pl.BlockSpec((Q_BLOCK, Q_PER_KV * HD), lambda b, g, i: (b * n_qb + i, g)),
        out_shape=jax.ShapeDtypeStruct((m, D_MODEL), BF16),
        compiler_params=_params(("parallel", "parallel", "arbitrary")),
        name="nsa",
    )(proj, kc, vct, ksa, vst, kwa, vwt, gt, tt, bc, ovt)


def _mix_body(gb_ref, gc_ref, va_ref, gch_ref, vah_ref, uc_ref, vc_ref, g0_ref, g1_ref, g2_ref, ob_ref, x_ref,
              ca_ref, sw_ref, sb_ref, lng_ref, lnb_ref, wa_ref, wb_ref, wc_ref, wo_ref, o_ref, xa_ref, mx_ref):
    tm = gb_ref.shape[0]
    first = pl.program_id(1) == 0

    xa = gc_ref[...].astype(F32) * va_ref[...].astype(F32)
    halo = gch_ref[...].astype(F32) * vah_ref[...].astype(F32)
    xa_ref[0:HALO, :] = jnp.where(first, 0.0, halo)
    xa_ref[HALO:, :] = xa
    ca = ca_ref[...]
    conv = (ca[0:1] * xa_ref[pl.ds(HALO - 2, tm), :] + ca[1:2] * xa_ref[pl.ds(HALO - 1, tm), :]
            + ca[2:3] * xa)
    out_a = gb_ref[...].astype(F32) * conv
    y_a = jnp.dot(out_a.astype(BF16), wa_ref[...], preferred_element_type=F32)

    v = jax.nn.gelu(vc_ref[...].astype(F32))
    mu = jnp.mean(v, axis=-1, keepdims=True)
    vz = v - mu
    var = jnp.mean(vz * vz, axis=-1, keepdims=True)
    vn = (vz * lax.rsqrt(var + EPS) * lng_ref[...] + lnb_ref[...]).astype(BF16)
    row = lax.broadcasted_iota(jnp.int32, (CHUNK, CHUNK), 0)
    col = lax.broadcasted_iota(jnp.int32, (CHUNK, CHUNK), 1)
    for gi in range(SGU_GROUPS):
        w = jnp.where(row >= col, sw_ref[gi], 0.0).astype(BF16)
        for c in range(tm // CHUNK):
            blk = vn[c * CHUNK:(c + 1) * CHUNK, gi * LANES:(gi + 1) * LANES]
            mx_ref[c * CHUNK:(c + 1) * CHUNK, gi * LANES:(gi + 1) * LANES] = (
                jnp.dot(w, blk, preferred_element_type=F32) + sb_ref[:, gi * LANES:(gi + 1) * LANES])
    out_c = jax.nn.gelu(uc_ref[...].astype(F32)) * mx_ref[...]
    y_c = jnp.dot(out_c.astype(BF16), wc_ref[...], preferred_element_type=F32)

    y_b = jnp.dot(ob_ref[...], wb_ref[...], preferred_element_type=F32)

    merged = (jax.nn.sigmoid(g0_ref[...].astype(F32)) * y_a
              + jax.nn.sigmoid(g1_ref[...].astype(F32)) * y_b
              + jax.nn.sigmoid(g2_ref[...].astype(F32)) * y_c)
    o_ref[...] = x_ref[...] + jnp.dot(merged.astype(BF16), wo_ref[...], preferred_element_type=F32)


def _mix(proj, out_b, x2, seq, conv_a, sgu_w, sgu_b2, ln_g, ln_b, w_a, w_b, w_c, w_o, tm=256):
    m = x2.shape[0]
    n_t = seq // tm
    hb = tm // HALO

    def seg(k):
        return pl.BlockSpec((tm, D_MODEL), lambda b, i: (b * n_t + i, k))

    def halo(k):
        return pl.BlockSpec((HALO, D_MODEL), lambda b, i: (jnp.maximum((b * n_t + i) * hb - 1, 0), k))

    def full(shape):
        return pl.BlockSpec(shape, lambda b, i: (0,) * len(shape))

    row = pl.BlockSpec((tm, D_MODEL), lambda b, i: (b * n_t + i, 0))
    wspec = full((D_MODEL, D_MODEL))
    return pl.pallas_call(
        _mix_body,
        grid=(m // seq, n_t),
        in_specs=[
            seg(0), seg(1), seg(2), halo(1), halo(2), seg(4), seg(5), seg(6), seg(7), seg(8), row, row,
            full((CONV_W, D_MODEL)), full((SGU_GROUPS, CHUNK, CHUNK)), full((CHUNK, D_MODEL)),
            full((1, D_MODEL)), full((1, D_MODEL)), wspec, wspec, wspec, wspec,
        ],
        out_specs=row,
        out_shape=jax.ShapeDtypeStruct((m, D_MODEL), F32),
        scratch_shapes=[pltpu.VMEM((tm + HALO, D_MODEL), F32), pltpu.VMEM((tm, D_MODEL), F32)],
        compiler_params=_params(("parallel", "arbitrary")),
        name="mix",
    )(proj, proj, proj, proj, proj, proj, proj, proj, proj, proj, out_b, x2,
      conv_a, sgu_w, sgu_b2, ln_g, ln_b, w_a, w_b, w_c, w_o)


def _ffn_body(x_ref, xh_ref, g_ref, wg_ref, wv_ref, cw_ref, wd_ref, o_ref, h_ref, gt_ref, acc_ref, *, tiles_per_seq):
    i = pl.program_id(0)
    j = pl.program_id(1)
    tm = x_ref.shape[0]

    @pl.when(j == 0)
    def _():
        h_ref[0:HALO, :] = _rms(xh_ref[...], g_ref[...]).astype(BF16)
        h_ref[HALO:, :] = _rms(x_ref[...], g_ref[...]).astype(BF16)
        acc_ref[...] = jnp.zeros_like(acc_ref)

    gate = jnp.dot(h_ref[...], wg_ref[...], preferred_element_type=F32)
    rows = lax.broadcasted_iota(jnp.int32, gate.shape, 0)
    seq_start = (i % tiles_per_seq) == 0
    gt_ref[...] = jnp.where(seq_start & (rows < HALO), 0.0, gate)
    cw = cw_ref[...]
    conv = (cw[0:1] * gt_ref[pl.ds(HALO - 2, tm), :] + cw[1:2] * gt_ref[pl.ds(HALO - 1, tm), :]
            + cw[2:3] * gt_ref[pl.ds(HALO, tm), :])
    val = jnp.dot(h_ref[HALO:, :], wv_ref[...], preferred_element_type=F32)
    act = (jax.nn.gelu(conv) * val).astype(BF16)
    acc_ref[...] += jnp.dot(act, wd_ref[...], preferred_element_type=F32)

    @pl.when(j == pl.num_programs(1) - 1)
    def _():
        o_ref[...] = x_ref[...] + acc_ref[...]


def _ffn(x2, seq, g, w_up, conv_w, w_down, tm=512, tf=512):
    m = x2.shape[0]
    n_f = D_FF // tf
    hb = tm // HALO
    return pl.pallas_call(
        functools.partial(_ffn_body, tiles_per_seq=seq // tm),
        grid=(m // tm, n_f),
        in_specs=[
            pl.BlockSpec((tm, D_MODEL), lambda i, j: (i, 0)),
            pl.BlockSpec((HALO, D_MODEL), lambda i, j: (jnp.maximum(i * hb - 1, 0), 0)),
            pl.BlockSpec((1, D_MODEL), lambda i, j: (0, 0)),
            pl.BlockSpec((D_MODEL, tf), lambda i, j: (0, j)),
            pl.BlockSpec((D_MODEL, tf), lambda i, j: (0, n_f + j)),
            pl.BlockSpec((CONV_W, tf), lambda i, j: (0, j)),
            pl.BlockSpec((tf, D_MODEL), lambda i, j: (j, 0)),
        ],
        out_specs=pl.BlockSpec((tm, D_MODEL), lambda i, j: (i, 0)),
        out_shape=jax.ShapeDtypeStruct((m, D_MODEL), F32),
        scratch_shapes=[
            pltpu.VMEM((tm + HALO, D_MODEL), BF16),
            pltpu.VMEM((tm + HALO, tf), F32),
            pltpu.VMEM((tm, D_MODEL), F32),
        ],
        compiler_params=_params(("parallel", "arbitrary")),
        name="ffn",
    )(x2, x2, g, w_up, w_up, conv_w, w_down)


def _norm_body(x_ref, g_ref, o_ref):
    o_ref[...] = _rms(x_ref[...], g_ref[...])


def _final_norm(x2, g, tm=1024):
    m = x2.shape[0]
    return pl.pallas_call(
        _norm_body,
        grid=(m // tm,),
        in_specs=[pl.BlockSpec((tm, D_MODEL), lambda i: (i, 0)), pl.BlockSpec((1, D_MODEL), lambda i: (0, 0))],
        out_specs=pl.BlockSpec((tm, D_MODEL), lambda i: (i, 0)),
        out_shape=jax.ShapeDtypeStruct((m, D_MODEL), F32),
        compiler_params=_params(("parallel",)),
        name="final_norm",
    )(x2, g)


def _split_w_in(w_in):
    c = [0]
    for sz in [D_MODEL] * 4 + [N_KV * HD] * 6 + [3 * N_HEADS] + [D_MODEL] * 5:
        c.append(c[-1] + sz)
    main = jnp.concatenate([w_in[:, c[0]:c[4]], w_in[:, c[11]:c[16]], w_in[:, c[4]:c[10]]], axis=1)
    gate = jnp.pad(w_in[:, c[10]:c[11]], ((0, 0), (0, LANES - 3 * N_HEADS)))
    return main.astype(BF16), gate.astype(BF16)


def _pad_keys(k, n_pad, with_block_id):
    b_, s_, g_, _ = k.shape
    pos = jnp.arange(-n_pad, s_)[:, None]
    col = jnp.arange(HD)[None, :]
    extra = (pos < 0) & (col == PAD_COL)
    if with_block_id:
        extra = extra | ((pos >= 0) & (pos // SEL_BLOCK == col))
    kp = jnp.pad(k.transpose(0, 2, 1, 3), ((0, 0), (0, 0), (n_pad, 0), (0, 0)))
    return jnp.concatenate([kp, jnp.broadcast_to(extra.astype(k.dtype), (b_, g_, n_pad + s_, HD))], axis=-1)


def _pad_values(v, n_pad):
    return jnp.pad(v.transpose(0, 2, 3, 1), ((0, 0), (0, 0), (0, 0), (n_pad, 0)))


def _mixer_layer(x2, b_, s_, tt, bc, ovt, norm_g, w_in, conv_a, cmp_pos, cmp_w1, cmp_w2, sgu_w, sgu_b,
                 sgu_norm_g, sgu_norm_b, w_br_a, w_br_b, w_br_c, w_o):
    n_qb = s_ // Q_BLOCK
    w_main, w_gate = _split_w_in(w_in)
    proj, g_nsa = _in_proj(x2, norm_g.reshape(1, D_MODEL), w_main, w_gate)

    kv = proj[:, KV_OFF:].reshape(b_, s_, 6, N_KV, HD)
    kc, vc, ks, vs, kw, vw = [kv[:, :, n] for n in range(6)]

    def rows16(a):
        return a.transpose(0, 2, 1, 3).reshape(b_, N_KV, s_ // CMP_STRIDE, CMP_STRIDE * HD)

    k_cmp, v_cmp = _compress(rows16(kc), rows16(vc), cmp_pos.reshape(2, 1, CMP_LEN * HD),
                             cmp_w1.astype(BF16), cmp_w2.astype(BF16))
    gt = (g_nsa[:, :3 * N_HEADS].reshape(b_, n_qb, Q_BLOCK, 3, N_KV, Q_PER_KV)
          .transpose(0, 4, 1, 3, 5, 2).reshape(b_, N_KV, n_qb, 3, Q_PER_KV * Q_BLOCK))
    out_b = _nsa(proj, k_cmp, v_cmp.transpose(0, 1, 3, 2),
                 _pad_keys(ks, LANES, True), _pad_values(vs, LANES),
                 _pad_keys(kw, WINDOW, False), _pad_values(vw, WINDOW), gt, tt, bc, ovt, s_)

    sgu_b2 = jnp.broadcast_to(sgu_b.T[:, :, None], (CHUNK, SGU_GROUPS, D_MODEL // SGU_GROUPS)).reshape(CHUNK, D_MODEL)
    return _mix(proj, out_b, x2, s_, conv_a, sgu_w, sgu_b2,
                sgu_norm_g.reshape(1, D_MODEL), sgu_norm_b.reshape(1, D_MODEL),
                w_br_a.astype(BF16), w_br_b.astype(BF16), w_br_c.astype(BF16), w_o.astype(BF16))


def _overlap_t(s_):
    n_blk = s_ // SEL_BLOCK
    cmp_start = jnp.arange(LANES) * CMP_STRIDE
    cmp_end = cmp_start + CMP_LEN - 1
    blk_start = jnp.arange(n_blk) * SEL_BLOCK
    n_cmp = (s_ - CMP_LEN) // CMP_STRIDE + 1
    ov = ((cmp_start[None, :] < blk_start[:, None] + SEL_BLOCK) & (cmp_end[None, :] >= blk_start[:, None])
          & (jnp.arange(LANES)[None, :] < n_cmp))
    return ov.astype(F32)


def kernel(x, rel_bias, norm_mix, w_in, conv_a, cmp_pos, cmp_w1, cmp_w2, sgu_w, sgu_b, sgu_norm_g, sgu_norm_b,
           w_br_a, w_br_b, w_br_c, w_o, norm_ffn, ffn_w_up, ffn_conv, ffn_w_down, norm_final):
    b_, s_, _ = x.shape
    depth = w_in.shape[0]
    x2 = x.reshape(b_ * s_, D_MODEL)
    tt, bc = _bias_tiles(rel_bias, s_ // Q_BLOCK)
    ovt = _overlap_t(s_)
    for l in range(depth):
        x2 = _mixer_layer(x2, b_, s_, tt, bc, ovt, norm_mix[l], w_in[l], conv_a[l], cmp_pos[l], cmp_w1[l],
                          cmp_w2[l], sgu_w[l], sgu_b[l], sgu_norm_g[l], sgu_norm_b[l],
                          w_br_a[l], w_br_b[l], w_br_c[l], w_o[l])
        x2 = _ffn(x2, s_, norm_ffn[l].reshape(1, D_MODEL), ffn_w_up[l].astype(BF16), ffn_conv[l],
                  ffn_w_down[l].astype(BF16))
    return _final_norm(x2, norm_final.reshape(1, D_MODEL)).reshape(b_, s_, D_MODEL)
```

```python
import functools
import math

import jax
import jax.numpy as jnp
from jax import lax
from jax.experimental import pallas as pl
from jax.experimental.pallas import tpu as pltpu

D_MODEL = 1024
HD = 64
N_HEADS = D_MODEL // HD
N_KV = 4
Q_PER_KV = N_HEADS // N_KV
SGU_GROUPS = 8
CHUNK = 128
CONV_W = 3
CMP_LEN = 32
CMP_STRIDE = 16
CMP_HID = 2 * HD
SEL_BLOCK = 64
SEL_TOP_N = 8
WINDOW = 512
Q_BLOCK = 128
D_FF = 3 * D_MODEL
N_BUCKETS = 32
MAX_EXACT = N_BUCKETS // 2
REL_MAX_DIST = 128
SCALE = HD ** -0.5
EPS = 1e-6
NEG_INF = -1e30
FORCE = 1e9
LOG2E = 1.4426950408889634

LANES = 128
HALO = 16
N_SEG = 9
N_MAIN = N_SEG * D_MODEL + 6 * N_KV * HD
KV_OFF = N_SEG * D_MODEL
VMEM_LIMIT = 56 * 1024 * 1024

BF16 = jnp.bfloat16
F32 = jnp.float32
_NT = (((1,), (1,)), ((), ()))


def _params(sem):
    return pltpu.CompilerParams(dimension_semantics=sem, vmem_limit_bytes=VMEM_LIMIT)


def _rms(x, g):
    return x * lax.rsqrt(jnp.mean(x * x, axis=-1, keepdims=True) + EPS) * g


def _in_proj_body(x_ref, g_ref, w_ref, wg_ref, o_ref, og_ref, h_ref):
    @pl.when(pl.program_id(1) == 0)
    def _():
        hb = _rms(x_ref[...], g_ref[...]).astype(BF16)
        h_ref[...] = hb
        og_ref[...] = jnp.dot(hb, wg_ref[...], preferred_element_type=F32)

    o_ref[...] = jnp.dot(h_ref[...], w_ref[...], preferred_element_type=F32).astype(o_ref.dtype)


def _in_proj(x2, g, w_main, w_gate, tm=1024, tn=512):
    m = x2.shape[0]
    return pl.pallas_call(
        _in_proj_body,
        grid=(m // tm, N_MAIN // tn),
        in_specs=[
            pl.BlockSpec((tm, D_MODEL), lambda i, j: (i, 0)),
            pl.BlockSpec((1, D_MODEL), lambda i, j: (0, 0)),
            pl.BlockSpec((D_MODEL, tn), lambda i, j: (0, j)),
            pl.BlockSpec((D_MODEL, LANES), lambda i, j: (0, 0)),
        ],
        out_specs=[
            pl.BlockSpec((tm, tn), lambda i, j: (i, j)),
            pl.BlockSpec((tm, LANES), lambda i, j: (i, 0)),
        ],
        out_shape=[
            jax.ShapeDtypeStruct((m, N_MAIN), BF16),
            jax.ShapeDtypeStruct((m, LANES), F32),
        ],
        scratch_shapes=[pltpu.VMEM((tm, D_MODEL), BF16)],
        compiler_params=_params(("parallel", "arbitrary")),
        name="in_proj",
    )(x2, g, w_main, w_gate)


def _rel_bucket(dist):
    dist = jnp.maximum(dist, 0)
    log_ratio = jnp.log(jnp.maximum(dist, 1).astype(F32) / MAX_EXACT) / math.log(REL_MAX_DIST / MAX_EXACT)
    large = MAX_EXACT + (log_ratio * (N_BUCKETS - MAX_EXACT)).astype(jnp.int32)
    return jnp.where(dist < MAX_EXACT, dist, jnp.minimum(large, N_BUCKETS - 1))


def _bias_body(tab_ref, bt_ref, bc_ref, tt_ref, bco_ref, *, n_cmp):
    h = pl.program_id(0)

    def lookup(bk):
        acc = jnp.zeros(bk.shape, F32)
        for b in range(N_BUCKETS):
            acc = jnp.where(bk == b, tab_ref[b, h], acc)
        return acc

    kk = lax.broadcasted_iota(jnp.int32, (LANES, LANES), 0)
    qq = lax.broadcasted_iota(jnp.int32, (LANES, LANES), 1)
    far = lookup(bt_ref[2])
    tt_ref[0] = (lookup(bt_ref[1]) - far) * LOG2E
    tt_ref[1] = jnp.where(qq >= kk, (lookup(bt_ref[0]) - far) * LOG2E, NEG_INF)
    tt_ref[2] = jnp.where(kk > qq, 0.0, NEG_INF)
    for i in range(bc_ref.shape[0]):
        ok = (i * Q_BLOCK + qq - (kk * CMP_STRIDE + CMP_LEN - 1) >= 0) & (kk < n_cmp)
        bco_ref[i] = jnp.where(ok, lookup(bc_ref[i]), NEG_INF)


def _bias_tiles(rel_bias, n_qb):
    kk = jnp.arange(LANES)[:, None]
    qq = jnp.arange(LANES)[None, :]
    bt = jnp.stack([_rel_bucket(LANES * d + qq - kk) for d in range(3)]).astype(jnp.int32)
    cmp_end = CMP_STRIDE * jnp.arange(LANES) + CMP_LEN - 1
    t = (Q_BLOCK * jnp.arange(n_qb))[:, None, None] + qq[None]
    bc = _rel_bucket(t - cmp_end[None, :, None]).astype(jnp.int32)
    n_cmp = (n_qb * Q_BLOCK - CMP_LEN) // CMP_STRIDE + 1
    return pl.pallas_call(
        functools.partial(_bias_body, n_cmp=n_cmp),
        grid=(N_HEADS,),
        in_specs=[
            pl.BlockSpec(memory_space=pltpu.SMEM),
            pl.BlockSpec((3, LANES, LANES), lambda h: (0, 0, 0)),
            pl.BlockSpec((n_qb, LANES, LANES), lambda h: (0, 0, 0)),
        ],
        out_specs=[
            pl.BlockSpec((None, 3, LANES, LANES), lambda h: (h // Q_PER_KV, 0, 0, h % Q_PER_KV)),
            pl.BlockSpec((None, n_qb, LANES, LANES), lambda h: (h // Q_PER_KV, 0, 0, h % Q_PER_KV)),
        ],
        out_shape=[
            jax.ShapeDtypeStruct((N_KV, 3, LANES, Q_PER_KV * LANES), F32),
            jax.ShapeDtypeStruct((N_KV, n_qb, LANES, Q_PER_KV * LANES), F32),
        ],
        compiler_params=_params(("arbitrary",)),
        name="bias_tiles",
    )(rel_bias.astype(F32), bt, bc)


def _compress_body(k3_ref, v3_ref, pos_ref, w1_ref, w2_ref, ko_ref, vo_ref):
    half = CMP_STRIDE * HD
    for which, (src, dst) in enumerate(((k3_ref, ko_ref), (v3_ref, vo_ref))):
        x = src[...].astype(F32)
        pos = pos_ref[which]
        xa = (x + pos[:, :half]).astype(BF16)
        xb = (x + pos[:, half:]).astype(BF16)
        a = jnp.dot(xa, w1_ref[which, :half, :], preferred_element_type=F32)
        b = jnp.dot(xb, w1_ref[which, half:, :], preferred_element_type=F32)
        hid = jax.nn.gelu(a + pltpu.roll(b, LANES - 1, 0))
        dst[...] = jnp.dot(hid.astype(BF16), w2_ref[which], preferred_element_type=F32).astype(dst.dtype)


def _compress(k3, v3, pos, w1, w2):
    b_, g_, nm, width = k3.shape
    spec3 = pl.BlockSpec((None, None, nm, width), lambda b, g: (b, g, 0, 0))
    ospec = pl.BlockSpec((None, None, nm, HD), lambda b, g: (b, g, 0, 0))
    return pl.pallas_call(
        _compress_body,
        grid=(b_, g_),
        in_specs=[
            spec3, spec3,
            pl.BlockSpec((2, 1, CMP_LEN * HD), lambda b, g: (0, 0, 0)),
            pl.BlockSpec((2, CMP_LEN * HD, CMP_HID), lambda b, g: (0, 0, 0)),
            pl.BlockSpec((2, CMP_HID, HD), lambda b, g: (0, 0, 0)),
        ],
        out_specs=[ospec, ospec],
        out_shape=[jax.ShapeDtypeStruct((b_, g_, nm, HD), BF16)] * 2,
        compiler_params=_params(("parallel", "parallel")),
        name="compress",
    )(k3, v3, pos, w1, w2)


FAR_TILES = 4
PAD_COL = 32
ONES_ROWS = 16


def _nsa_body(q_ref, kc_ref, vct_ref, ksa_ref, vst_ref, kwa_ref, vwt_ref, gt_ref, tt_ref, bc_ref, ovt_ref, o_ref,
              sbuf0_ref, sbuf1_ref, mbuf0_ref, mbuf1_ref, qf_ref, m_ref, av_ref):
    i = pl.program_id(2)
    t0 = i * Q_BLOCK
    nl = Q_PER_KV * LANES
    n_win = WINDOW // LANES + 1
    qt = q_ref[...]
    q = jnp.concatenate([qt[:, r * HD:(r + 1) * HD] for r in range(Q_PER_KV)], axis=0) * SCALE

    s = lax.dot_general(kc_ref[...], q, _NT, preferred_element_type=F32) + bc_ref[...]
    m_c = jnp.max(s, axis=0, keepdims=True)
    p = jnp.exp(s - m_c)
    l_c = jnp.sum(p, axis=0, keepdims=True)
    p_c = p * (jnp.where(m_c > 0.5 * NEG_INF, 1.0, 0.0) / l_c)
    o_c = jnp.dot(vct_ref[...], p_c.astype(BF16), preferred_element_type=F32)

    p_sum = p_c[:, 0:LANES]
    for r in range(1, Q_PER_KV):
        p_sum = p_sum + p_c[:, r * LANES:(r + 1) * LANES]
    imp = jnp.dot(ovt_ref[...], p_sum, precision=lax.Precision.HIGHEST, preferred_element_type=F32)
    n_blk = imp.shape[0]
    j_io = lax.broadcasted_iota(jnp.int32, (n_blk, LANES), 0)
    t_b = t0 + lax.broadcasted_iota(jnp.int32, (n_blk, LANES), 1)
    cur = t_b >> 6
    forced = (j_io == 0) | (j_io == cur) | (j_io == cur - 1)
    valid = j_io * SEL_BLOCK <= t_b
    score = jnp.where(forced, FORCE, jnp.where(valid, imp, -FORCE))
    rank = jnp.zeros((n_blk, LANES), jnp.int32)
    for ii in range(n_blk):
        row = score[ii:ii + 1, :]
        beats = (row > score) | ((row == score) & (j_io > ii))
        rank = rank + beats.astype(jnp.int32)
    keep = (rank < min(SEL_TOP_N, n_blk)) & (score >= 0.0)

    q2 = (q.astype(F32) * LOG2E).astype(BF16)

    def augment(kept):
        cols = jnp.concatenate([jnp.where(kept, 0.0, NEG_INF), jnp.full((8, LANES), NEG_INF, F32),
                                jnp.zeros((LANES - n_blk - 8, LANES), F32)], axis=0)
        cols_t = cols.T[:, 0:HD].astype(BF16)
        return jnp.concatenate([q2, jnp.concatenate([cols_t] * Q_PER_KV, axis=0)], axis=1)

    q_near = augment(keep)
    qf_ref[...] = augment(keep & (j_io < 2 * (i - 1)))
    col_w = lax.broadcasted_iota(jnp.int32, (Q_PER_KV * LANES, HD), 1)
    q_win = jnp.concatenate([q2, jnp.where(col_w == PAD_COL, NEG_INF, 0.0).astype(BF16)], axis=1)

    def softmax_av(tiles, vt):
        s_all = jnp.concatenate(tiles, axis=0)
        m = jnp.max(s_all, axis=0, keepdims=True)
        p = jnp.exp2(s_all - m)
        return m, jnp.dot(vt, p.astype(BF16), preferred_element_type=F32)

    def normalise(av):
        return av[0:HD] / av[HD:HD + 1]

    def logit_tiles(k_ref, qq, row0, n):
        s_all = lax.dot_general(k_ref[pl.ds(row0, n * LANES), :], qq, _NT, preferred_element_type=F32)
        return [s_all[t * LANES:(t + 1) * LANES] for t in range(n)]

    r0 = pl.multiple_of(i * LANES, LANES)

    tw = logit_tiles(kwa_ref, q_win, r0, n_win)
    tw[0] = tw[0] + tt_ref[2]
    tw[n_win - 2] = tw[n_win - 2] + tt_ref[0]
    tw[n_win - 1] = tw[n_win - 1] + tt_ref[1]
    _, av_w = softmax_av(tw, vwt_ref[:, pl.ds(r0, n_win * LANES)])
    o_w = normalise(av_w)

    tn = logit_tiles(ksa_ref, q_near, r0, 2)
    tn[0] = tn[0] + tt_ref[0]
    tn[1] = tn[1] + tt_ref[1]
    m_n, av_n = softmax_av(tn, vst_ref[:, pl.ds(r0, 2 * LANES)])
    m_ref[...] = m_n
    av_ref[...] = av_n

    max_chunks = (ksa_ref.shape[0] - LANES) // (FAR_TILES * LANES)
    n_chunks = (jnp.maximum(i - 1, 0) + FAR_TILES - 1) // FAR_TILES

    sbufs = (sbuf0_ref, sbuf1_ref)
    mbufs = (mbuf0_ref, mbuf1_ref)

    def far_logits(c):
        rows = (c * FAR_TILES + 1) * LANES
        s_all = lax.dot_general(ksa_ref[rows:rows + FAR_TILES * LANES, :], qf_ref[...], _NT,
                                preferred_element_type=F32)
        sbufs[c % 2][...] = s_all
        mbufs[c % 2][...] = jnp.max(s_all, axis=0, keepdims=True)

    def far_fold(c):
        rows = (c * FAR_TILES + 1) * LANES
        m_old = m_ref[...]
        m_new = jnp.maximum(m_old, mbufs[c % 2][...])
        p = jnp.exp2(sbufs[c % 2][...] - m_new)
        av_ref[...] = jnp.exp2(m_old - m_new) * av_ref[...] + jnp.dot(
            vst_ref[:, rows:rows + FAR_TILES * LANES], p.astype(BF16), preferred_element_type=F32)
        m_ref[...] = m_new

    @pl.when(n_chunks >= 1)
    def _():
        far_logits(0)

    for c in range(max_chunks):
        if c + 1 < max_chunks:
            @pl.when(n_chunks >= c + 2)
            def _(c=c):
                far_logits(c + 1)
                far_fold(c)

        @pl.when(n_chunks == c + 1)
        def _(c=c):
            far_fold(c)

    o_s = normalise(av_ref[...])

    g = jax.nn.sigmoid(gt_ref[...])
    o_t = g[0:1] * o_c + g[1:2] * o_s + g[2:3] * o_w
    for pair in range(Q_PER_KV // 2):
        blk = jnp.concatenate([o_t[:, (2 * pair) * LANES:(2 * pair + 1) * LANES],
                               o_t[:, (2 * pair + 1) * LANES:(2 * pair + 2) * LANES]], axis=0)
        o_ref[:, pair * LANES:(pair + 1) * LANES] = blk.T.astype(o_ref.dtype)


def _nsa(proj, kc, vct, ksa, vst, kwa, vwt, gt, tt, bc, ovt, s_):
    b_, g_ = kwa.shape[:2]
    n_qb = s_ // Q_BLOCK
    m = proj.shape[0]
    nl = Q_PER_KV * LANES
    qcol = 3 * D_MODEL // (Q_PER_KV * HD)

    def bg(shape):
        return pl.BlockSpec((None, None) + shape, lambda b, g, i: (b, g, 0, 0))

    return pl.pallas_call(
        _nsa_body,
        grid=(b_, g_, n_qb),
        in_specs=[
            pl.BlockSpec((Q_BLOCK, Q_PER_KV * HD), lambda b, g, i: (b * n_qb + i, qcol + g)),
            bg((LANES, HD)), bg((HD, LANES)),
            bg((s_ + LANES, 2 * HD)), bg((HD + ONES_ROWS, s_ + LANES)),
            bg((s_ + WINDOW, 2 * HD)), bg((HD + ONES_ROWS, s_ + WINDOW)),
            pl.BlockSpec((None, None, None, 3, nl), lambda b, g, i: (b, g, i, 0, 0)),
            pl.BlockSpec((None, 3, LANES, nl), lambda b, g, i: (g, 0, 0, 0)),
            pl.BlockSpec((None, None, LANES, nl), lambda b, g, i: (g, i, 0, 0)),
            pl.BlockSpec((s_ // SEL_BLOCK, LANES), lambda b, g, i: (0, 0)),
        ],
        out_specs=pl.BlockSpec((Q_BLOCK, Q_PER_KV * HD), lambda b, g, i: (b * n_qb + i, g)),
        out_shape=jax.ShapeDtypeStruct((m, D_MODEL), BF16),
        scratch_shapes=[pltpu.VMEM((FAR_TILES * LANES, nl), F32), pltpu.VMEM((FAR_TILES * LANES, nl), F32),
                        pltpu.VMEM((1, nl), F32), pltpu.VMEM((1, nl), F32),
                        pltpu.VMEM((nl, 2 * HD), BF16), pltpu.VMEM((1, nl), F32),
                        pltpu.VMEM((HD + ONES_ROWS, nl), F32)],
        compiler_params=_params(("parallel", "parallel", "arbitrary")),
        name="nsa",
    )(proj, kc, vct, ksa, vst, kwa, vwt, gt, tt, bc, ovt)


def _mix_body(gb_ref, gc_ref, va_ref, gch_ref, vah_ref, uc_ref, vc_ref, g0_ref, g1_ref, g2_ref, ob_ref, x_ref,
              ca_ref, sw_ref, sb_ref, lng_ref, lnb_ref, wa_ref, wb_ref, wc_ref, wo_ref, o_ref, xa_ref, mx_ref):
    tm = gb_ref.shape[0]
    first = pl.program_id(1) == 0

    xa = gc_ref[...].astype(F32) * va_ref[...].astype(F32)
    halo = gch_ref[...].astype(F32) * vah_ref[...].astype(F32)
    xa_ref[0:HALO, :] = jnp.where(first, 0.0, halo)
    xa_ref[HALO:, :] = xa
    ca = ca_ref[...]
    conv = (ca[0:1] * xa_ref[pl.ds(HALO - 2, tm), :] + ca[1:2] * xa_ref[pl.ds(HALO - 1, tm), :]
            + ca[2:3] * xa)
    out_a = gb_ref[...].astype(F32) * conv
    y_a = jnp.dot(out_a.astype(BF16), wa_ref[...], preferred_element_type=F32)

    v = jax.nn.gelu(vc_ref[...].astype(F32))
    mu = jnp.mean(v, axis=-1, keepdims=True)
    vz = v - mu
    var = jnp.mean(vz * vz, axis=-1, keepdims=True)
    vn = (vz * lax.rsqrt(var + EPS) * lng_ref[...] + lnb_ref[...]).astype(BF16)
    row = lax.broadcasted_iota(jnp.int32, (CHUNK, CHUNK), 0)
    col = lax.broadcasted_iota(jnp.int32, (CHUNK, CHUNK), 1)
    for gi in range(SGU_GROUPS):
        w = jnp.where(row >= col, sw_ref[gi], 0.0).astype(BF16)
        for c in range(tm // CHUNK):
            blk = vn[c * CHUNK:(c + 1) * CHUNK, gi * LANES:(gi + 1) * LANES]
            mx_ref[c * CHUNK:(c + 1) * CHUNK, gi * LANES:(gi + 1) * LANES] = (
                jnp.dot(w, blk, preferred_element_type=F32) + sb_ref[:, gi * LANES:(gi + 1) * LANES])
    out_c = jax.nn.gelu(uc_ref[...].astype(F32)) * mx_ref[...]
    y_c = jnp.dot(out_c.astype(BF16), wc_ref[...], preferred_element_type=F32)

    y_b = jnp.dot(ob_ref[...], wb_ref[...], preferred_element_type=F32)

    merged = (jax.nn.sigmoid(g0_ref[...].astype(F32)) * y_a
              + jax.nn.sigmoid(g1_ref[...].astype(F32)) * y_b
              + jax.nn.sigmoid(g2_ref[...].astype(F32)) * y_c)
    o_ref[...] = x_ref[...] + jnp.dot(merged.astype(BF16), wo_ref[...], preferred_element_type=F32)


def _mix(proj, out_b, x2, seq, conv_a, sgu_w, sgu_b2, ln_g, ln_b, w_a, w_b, w_c, w_o, tm=256):
    m = x2.shape[0]
    n_t = seq // tm
    hb = tm // HALO

    def seg(k):
        return pl.BlockSpec((tm, D_MODEL), lambda b, i: (b * n_t + i, k))

    def halo(k):
        return pl.BlockSpec((HALO, D_MODEL), lambda b, i: (jnp.maximum((b * n_t + i) * hb - 1, 0), k))

    def full(shape):
        return pl.BlockSpec(shape, lambda b, i: (0,) * len(shape))

    row = pl.BlockSpec((tm, D_MODEL), lambda b, i: (b * n_t + i, 0))
    wspec = full((D_MODEL, D_MODEL))
    return pl.pallas_call(
        _mix_body,
        grid=(m // seq, n_t),
        in_specs=[
            seg(0), seg(1), seg(2), halo(1), halo(2), seg(4), seg(5), seg(6), seg(7), seg(8), row, row,
            full((CONV_W, D_MODEL)), full((SGU_GROUPS, CHUNK, CHUNK)), full((CHUNK, D_MODEL)),
            full((1, D_MODEL)), full((1, D_MODEL)), wspec, wspec, wspec, wspec,
        ],
        out_specs=row,
        out_shape=jax.ShapeDtypeStruct((m, D_MODEL), F32),
        scratch_shapes=[pltpu.VMEM((tm + HALO, D_MODEL), F32), pltpu.VMEM((tm, D_MODEL), F32)],
        compiler_params=_params(("parallel", "arbitrary")),
        name="mix",
    )(proj, proj, proj, proj, proj, proj, proj, proj, proj, proj, out_b, x2,
      conv_a, sgu_w, sgu_b2, ln_g, ln_b, w_a, w_b, w_c, w_o)


def _ffn_body(x_ref, xh_ref, g_ref, wg_ref, wv_ref, cw_ref, wd_ref, o_ref, h_ref, gt_ref, acc_ref, *, tiles_per_seq):
    i = pl.program_id(0)
    j = pl.program_id(1)
    tm = x_ref.shape[0]

    @pl.when(j == 0)
    def _():
        h_ref[0:HALO, :] = _rms(xh_ref[...], g_ref[...]).astype(BF16)
        h_ref[HALO:, :] = _rms(x_ref[...], g_ref[...]).astype(BF16)
        acc_ref[...] = jnp.zeros_like(acc_ref)

    gate = jnp.dot(h_ref[...], wg_ref[...], preferred_element_type=F32)
    rows = lax.broadcasted_iota(jnp.int32, gate.shape, 0)
    seq_start = (i % tiles_per_seq) == 0
    gt_ref[...] = jnp.where(seq_start & (rows < HALO), 0.0, gate)
    cw = cw_ref[...]
    conv = (cw[0:1] * gt_ref[pl.ds(HALO - 2, tm), :] + cw[1:2] * gt_ref[pl.ds(HALO - 1, tm), :]
            + cw[2:3] * gt_ref[pl.ds(HALO, tm), :])
    val = jnp.dot(h_ref[HALO:, :], wv_ref[...], preferred_element_type=F32)
    act = (jax.nn.gelu(conv) * val).astype(BF16)
    acc_ref[...] += jnp.dot(act, wd_ref[...], preferred_element_type=F32)

    @pl.when(j == pl.num_programs(1) - 1)
    def _():
        o_ref[...] = x_ref[...] + acc_ref[...]


def _ffn(x2, seq, g, w_up, conv_w, w_down, tm=512, tf=512):
    m = x2.shape[0]
    n_f = D_FF // tf
    hb = tm // HALO
    return pl.pallas_call(
        functools.partial(_ffn_body, tiles_per_seq=seq // tm),
        grid=(m // tm, n_f),
        in_specs=[
            pl.BlockSpec((tm, D_MODEL), lambda i, j: (i, 0)),
            pl.BlockSpec((HALO, D_MODEL), lambda i, j: (jnp.maximum(i * hb - 1, 0), 0)),
            pl.BlockSpec((1, D_MODEL), lambda i, j: (0, 0)),
            pl.BlockSpec((D_MODEL, tf), lambda i, j: (0, j)),
            pl.BlockSpec((D_MODEL, tf), lambda i, j: (0, n_f + j)),
            pl.BlockSpec((CONV_W, tf), lambda i, j: (0, j)),
            pl.BlockSpec((tf, D_MODEL), lambda i, j: (j, 0)),
        ],
        out_specs=pl.BlockSpec((tm, D_MODEL), lambda i, j: (i, 0)),
        out_shape=jax.ShapeDtypeStruct((m, D_MODEL), F32),
        scratch_shapes=[
            pltpu.VMEM((tm + HALO, D_MODEL), BF16),
            pltpu.VMEM((tm + HALO, tf), F32),
            pltpu.VMEM((tm, D_MODEL), F32),
        ],
        compiler_params=_params(("parallel", "arbitrary")),
        name="ffn",
    )(x2, x2, g, w_up, w_up, conv_w, w_down)


def _norm_body(x_ref, g_ref, o_ref):
    o_ref[...] = _rms(x_ref[...], g_ref[...])


def _final_norm(x2, g, tm=1024):
    m = x2.shape[0]
    return pl.pallas_call(
        _norm_body,
        grid=(m // tm,),
        in_specs=[pl.BlockSpec((tm, D_MODEL), lambda i: (i, 0)), pl.BlockSpec((1, D_MODEL), lambda i: (0, 0))],
        out_specs=pl.BlockSpec((tm, D_MODEL), lambda i: (i, 0)),
        out_shape=jax.ShapeDtypeStruct((m, D_MODEL), F32),
        compiler_params=_params(("parallel",)),
        name="final_norm",
    )(x2, g)


def _split_w_in(w_in):
    c = [0]
    for sz in [D_MODEL] * 4 + [N_KV * HD] * 6 + [3 * N_HEADS] + [D_MODEL] * 5:
        c.append(c[-1] + sz)
    main = jnp.concatenate([w_in[:, c[0]:c[4]], w_in[:, c[11]:c[16]], w_in[:, c[4]:c[10]]], axis=1)
    gate = jnp.pad(w_in[:, c[10]:c[11]], ((0, 0), (0, LANES - 3 * N_HEADS)))
    return main.astype(BF16), gate.astype(BF16)


def _pad_keys(k, n_pad, with_block_id):
    b_, s_, g_, _ = k.shape
    pos = jnp.arange(-n_pad, s_)[:, None]
    col = jnp.arange(HD)[None, :]
    extra = (pos < 0) & (col == PAD_COL)
    if with_block_id:
        extra = extra | ((pos >= 0) & (pos // SEL_BLOCK == col))
    kp = jnp.pad(k.transpose(0, 2, 1, 3), ((0, 0), (0, 0), (n_pad, 0), (0, 0)))
    return jnp.concatenate([kp, jnp.broadcast_to(extra.astype(k.dtype), (b_, g_, n_pad + s_, HD))], axis=-1)


def _pad_values(v, n_pad):
    b_, s_, g_, _ = v.shape
    vt = jnp.pad(v.transpose(0, 2, 3, 1), ((0, 0), (0, 0), (0, 0), (n_pad, 0)))
    ones = (jnp.arange(ONES_ROWS)[:, None] == 0).astype(v.dtype)
    return jnp.concatenate([vt, jnp.broadcast_to(ones, (b_, g_, ONES_ROWS, n_pad + s_))], axis=2)


def _mixer_layer(x2, b_, s_, tt, bc, ovt, norm_g, w_in, conv_a, cmp_pos, cmp_w1, cmp_w2, sgu_w, sgu_b,
                 sgu_norm_g, sgu_norm_b, w_br_a, w_br_b, w_br_c, w_o):
    n_qb = s_ // Q_BLOCK
    w_main, w_gate = _split_w_in(w_in)
    proj, g_nsa = _in_proj(x2, norm_g.reshape(1, D_MODEL), w_main, w_gate)

    kv = proj[:, KV_OFF:].reshape(b_, s_, 6, N_KV, HD)
    kc, vc, ks, vs, kw, vw = [kv[:, :, n] for n in range(6)]

    def rows16(a):
        return a.transpose(0, 2, 1, 3).reshape(b_, N_KV, s_ // CMP_STRIDE, CMP_STRIDE * HD)

    k_cmp, v_cmp = _compress(rows16(kc), rows16(vc), cmp_pos.reshape(2, 1, CMP_LEN * HD),
                             cmp_w1.astype(BF16), cmp_w2.astype(BF16))
    gt = (g_nsa[:, :3 * N_HEADS].reshape(b_, n_qb, Q_BLOCK, 3, N_KV, Q_PER_KV)
          .transpose(0, 4, 1, 3, 5, 2).reshape(b_, N_KV, n_qb, 3, Q_PER_KV * Q_BLOCK))
    out_b = _nsa(proj, k_cmp, v_cmp.transpose(0, 1, 3, 2),
                 _pad_keys(ks, LANES, True), _pad_values(vs, LANES),
                 _pad_keys(kw, WINDOW, False), _pad_values(vw, WINDOW), gt, tt, bc, ovt, s_)

    sgu_b2 = jnp.broadcast_to(sgu_b.T[:, :, None], (CHUNK, SGU_GROUPS, D_MODEL // SGU_GROUPS)).reshape(CHUNK, D_MODEL)
    return _mix(proj, out_b, x2, s_, conv_a, sgu_w, sgu_b2,
                sgu_norm_g.reshape(1, D_MODEL), sgu_norm_b.reshape(1, D_MODEL),
                w_br_a.astype(BF16), w_br_b.astype(BF16), w_br_c.astype(BF16), w_o.astype(BF16))


def _overlap_t(s_):
    n_blk = s_ // SEL_BLOCK
    cmp_start = jnp.arange(LANES) * CMP_STRIDE
    cmp_end = cmp_start + CMP_LEN - 1
    blk_start = jnp.arange(n_blk) * SEL_BLOCK
    n_cmp = (s_ - CMP_LEN) // CMP_STRIDE + 1
    ov = ((cmp_start[None, :] < blk_start[:, None] + SEL_BLOCK) & (cmp_end[None, :] >= blk_start[:, None])
          & (jnp.arange(LANES)[None, :] < n_cmp))
    return ov.astype(F32)


def kernel(x, rel_bias, norm_mix, w_in, conv_a, cmp_pos, cmp_w1, cmp_w2, sgu_w, sgu_b, sgu_norm_g, sgu_norm_b,
           w_br_a, w_br_b, w_br_c, w_o, norm_ffn, ffn_w_up, ffn_conv, ffn_w_down, norm_final):
    b_, s_, _ = x.shape
    depth = w_in.shape[0]
    x2 = x.reshape(b_ * s_, D_MODEL)
    tt, bc = _bias_tiles(rel_bias, s_ // Q_BLOCK)
    ovt = _overlap_t(s_)
    for l in range(depth):
        x2 = _mixer_layer(x2, b_, s_, tt, bc, ovt, norm_mix[l], w_in[l], conv_a[l], cmp_pos[l], cmp_w1[l],
                          cmp_w2[l], sgu_w[l], sgu_b[l], sgu_norm_g[l], sgu_norm_b[l],
                          w_br_a[l], w_br_b[l], w_br_c[l], w_o[l])
        x2 = _ffn(x2, s_, norm_ffn[l].reshape(1, D_MODEL), ffn_w_up[l].astype(BF16), ffn_conv[l],
                  ffn_w_down[l].astype(BF16))
    return _final_norm(x2, norm_final.reshape(1, D_MODEL)).reshape(b_, s_, D_MODEL)
```

```python
import functools
import math

import jax
import jax.numpy as jnp
from jax import lax
from jax.experimental import pallas as pl
from jax.experimental.pallas import tpu as pltpu

D_MODEL = 1024
HD = 64
N_HEADS = D_MODEL // HD
N_KV = 4
Q_PER_KV = N_HEADS // N_KV
SGU_GROUPS = 8
CHUNK = 128
CONV_W = 3
CMP_LEN = 32
CMP_STRIDE = 16
CMP_HID = 2 * HD
SEL_BLOCK = 64
SEL_TOP_N = 8
WINDOW = 512
Q_BLOCK = 128
D_FF = 3 * D_MODEL
N_BUCKETS = 32
MAX_EXACT = N_BUCKETS // 2
REL_MAX_DIST = 128
SCALE = HD ** -0.5
EPS = 1e-6
NEG_INF = -1e30
FORCE = 1e9
LOG2E = 1.4426950408889634

LANES = 128
ROW_CHUNK = 256
HALO = 16
N_SEG = 9
SEG_GB, SEG_GC, SEG_VA, SEG_Q, SEG_U, SEG_V, SEG_G0 = 0, 1, 2, 3, 4, 5, 6
N_MAIN = N_SEG * D_MODEL + 6 * N_KV * HD
KV_OFF = N_SEG * D_MODEL
VMEM_LIMIT = 56 * 1024 * 1024

BF16 = jnp.bfloat16
F32 = jnp.float32
_NT = (((1,), (1,)), ((), ()))


def _params(sem):
    return pltpu.CompilerParams(dimension_semantics=sem, vmem_limit_bytes=VMEM_LIMIT)


def _rms(x, g):
    return x * lax.rsqrt(jnp.mean(x * x, axis=-1, keepdims=True) + EPS) * g


def _in_proj_body(x_ref, g_ref, w_ref, wg_ref, o_ref, og_ref, h_ref, *, gelu_tiles, sigmoid_tiles):
    j = pl.program_id(1)

    @pl.when(j == 0)
    def _():
        hb = _rms(x_ref[...], g_ref[...]).astype(BF16)
        h_ref[...] = hb
        og_ref[...] = jax.nn.sigmoid(jnp.dot(hb, wg_ref[...], preferred_element_type=F32))

    def tile(act):
        for r in range(0, h_ref.shape[0], ROW_CHUNK):
            y = jnp.dot(h_ref[r:r + ROW_CHUNK, :], w_ref[...], preferred_element_type=F32)
            o_ref[r:r + ROW_CHUNK, :] = act(y).astype(o_ref.dtype)

    is_gelu = (j >= gelu_tiles[0]) & (j < gelu_tiles[1])
    is_sigmoid = (j >= sigmoid_tiles[0]) & (j < sigmoid_tiles[1])
    pl.when(is_gelu)(lambda: tile(jax.nn.gelu))
    pl.when(is_sigmoid)(lambda: tile(jax.nn.sigmoid))
    pl.when(jnp.logical_not(is_gelu | is_sigmoid))(lambda: tile(lambda y: y))


def _in_proj(x2, g, w_main, w_gate, tm=1024, tn=512):
    m = x2.shape[0]
    per_seg = D_MODEL // tn
    body = functools.partial(_in_proj_body, gelu_tiles=(SEG_U * per_seg, (SEG_V + 1) * per_seg),
                             sigmoid_tiles=(SEG_G0 * per_seg, (SEG_G0 + 3) * per_seg))
    return pl.pallas_call(
        body,
        grid=(m // tm, N_MAIN // tn),
        in_specs=[
            pl.BlockSpec((tm, D_MODEL), lambda i, j: (i, 0)),
            pl.BlockSpec((1, D_MODEL), lambda i, j: (0, 0)),
            pl.BlockSpec((D_MODEL, tn), lambda i, j: (0, j)),
            pl.BlockSpec((D_MODEL, LANES), lambda i, j: (0, 0)),
        ],
        out_specs=[
            pl.BlockSpec((tm, tn), lambda i, j: (i, j)),
            pl.BlockSpec((tm, LANES), lambda i, j: (i, 0)),
        ],
        out_shape=[
            jax.ShapeDtypeStruct((m, N_MAIN), BF16),
            jax.ShapeDtypeStruct((m, LANES), F32),
        ],
        scratch_shapes=[pltpu.VMEM((tm, D_MODEL), BF16)],
        compiler_params=_params(("parallel", "arbitrary")),
        name="in_proj",
    )(x2, g, w_main, w_gate)


def _rel_bucket(dist):
    dist = jnp.maximum(dist, 0)
    log_ratio = jnp.log(jnp.maximum(dist, 1).astype(F32) / MAX_EXACT) / math.log(REL_MAX_DIST / MAX_EXACT)
    large = MAX_EXACT + (log_ratio * (N_BUCKETS - MAX_EXACT)).astype(jnp.int32)
    return jnp.where(dist < MAX_EXACT, dist, jnp.minimum(large, N_BUCKETS - 1))


def _bias_body(tab_ref, bt_ref, bc_ref, tt_ref, bco_ref, *, n_cmp):
    h = pl.program_id(0)

    def lookup(bk):
        acc = jnp.zeros(bk.shape, F32)
        for b in range(N_BUCKETS):
            acc = jnp.where(bk == b, tab_ref[b, h], acc)
        return acc

    kk = lax.broadcasted_iota(jnp.int32, (LANES, LANES), 0)
    qq = lax.broadcasted_iota(jnp.int32, (LANES, LANES), 1)
    far = lookup(bt_ref[2])
    tt_ref[0] = (lookup(bt_ref[1]) - far) * LOG2E
    tt_ref[1] = jnp.where(qq >= kk, (lookup(bt_ref[0]) - far) * LOG2E, NEG_INF)
    tt_ref[2] = jnp.where(kk > qq, 0.0, NEG_INF)
    for i in range(bc_ref.shape[0]):
        ok = (i * Q_BLOCK + qq - (kk * CMP_STRIDE + CMP_LEN - 1) >= 0) & (kk < n_cmp)
        bco_ref[i] = jnp.where(ok, lookup(bc_ref[i]), NEG_INF)


def _bias_tiles(rel_bias, n_qb):
    kk = jnp.arange(LANES)[:, None]
    qq = jnp.arange(LANES)[None, :]
    bt = jnp.stack([_rel_bucket(LANES * d + qq - kk) for d in range(3)]).astype(jnp.int32)
    cmp_end = CMP_STRIDE * jnp.arange(LANES) + CMP_LEN - 1
    t = (Q_BLOCK * jnp.arange(n_qb))[:, None, None] + qq[None]
    bc = _rel_bucket(t - cmp_end[None, :, None]).astype(jnp.int32)
    n_cmp = (n_qb * Q_BLOCK - CMP_LEN) // CMP_STRIDE + 1
    return pl.pallas_call(
        functools.partial(_bias_body, n_cmp=n_cmp),
        grid=(N_HEADS,),
        in_specs=[
            pl.BlockSpec(memory_space=pltpu.SMEM),
            pl.BlockSpec((3, LANES, LANES), lambda h: (0, 0, 0)),
            pl.BlockSpec((n_qb, LANES, LANES), lambda h: (0, 0, 0)),
        ],
        out_specs=[
            pl.BlockSpec((None, 3, LANES, LANES), lambda h: (h // Q_PER_KV, 0, 0, h % Q_PER_KV)),
            pl.BlockSpec((None, n_qb, LANES, LANES), lambda h: (h // Q_PER_KV, 0, 0, h % Q_PER_KV)),
        ],
        out_shape=[
            jax.ShapeDtypeStruct((N_KV, 3, LANES, Q_PER_KV * LANES), F32),
            jax.ShapeDtypeStruct((N_KV, n_qb, LANES, Q_PER_KV * LANES), F32),
        ],
        compiler_params=_params(("arbitrary",)),
        name="bias_tiles",
    )(rel_bias.astype(F32), bt, bc)


def _compress_body(k3_ref, v3_ref, pos_ref, w1_ref, w2_ref, ko_ref, vo_ref):
    half = CMP_STRIDE * HD
    for which, (src, dst) in enumerate(((k3_ref, ko_ref), (v3_ref, vo_ref))):
        x = src[...].astype(F32)
        pos = pos_ref[which]
        xa = (x + pos[:, :half]).astype(BF16)
        xb = (x + pos[:, half:]).astype(BF16)
        a = jnp.dot(xa, w1_ref[which, :half, :], preferred_element_type=F32)
        b = jnp.dot(xb, w1_ref[which, half:, :], preferred_element_type=F32)
        hid = jax.nn.gelu(a + pltpu.roll(b, LANES - 1, 0))
        dst[...] = jnp.dot(hid.astype(BF16), w2_ref[which], preferred_element_type=F32).astype(dst.dtype)


def _compress(k3, v3, pos, w1, w2):
    b_, g_, nm, width = k3.shape
    spec3 = pl.BlockSpec((None, None, nm, width), lambda b, g: (b, g, 0, 0))
    ospec = pl.BlockSpec((None, None, nm, HD), lambda b, g: (b, g, 0, 0))
    return pl.pallas_call(
        _compress_body,
        grid=(b_, g_),
        in_specs=[
            spec3, spec3,
            pl.BlockSpec((2, 1, CMP_LEN * HD), lambda b, g: (0, 0, 0)),
            pl.BlockSpec((2, CMP_LEN * HD, CMP_HID), lambda b, g: (0, 0, 0)),
            pl.BlockSpec((2, CMP_HID, HD), lambda b, g: (0, 0, 0)),
        ],
        out_specs=[ospec, ospec],
        out_shape=[jax.ShapeDtypeStruct((b_, g_, nm, HD), BF16)] * 2,
        compiler_params=_params(("parallel", "parallel")),
        name="compress",
    )(k3, v3, pos, w1, w2)


FAR_TILES = 4
PAD_COL = 32
ONES_ROWS = 16


def _nsa_body(qa_ref, qb_ref, kc_ref, vct_ref, ksa_ref, vst_ref, kwa_ref, vwt_ref, gta_ref, gtb_ref, tt_ref,
              bca_ref, bcb_ref, ovt_ref, oa_ref, ob_ref, sbuf_ref, qf_ref):
    p_id = pl.program_id(2)
    n_qb = 2 * pl.num_programs(2)
    nl = Q_PER_KV * LANES
    n_win = WINDOW // LANES + 1
    n_blk = ovt_ref.shape[0]
    max_chunks = (ksa_ref.shape[0] - LANES) // (FAR_TILES * LANES)
    j_io = lax.broadcasted_iota(jnp.int32, (n_blk, LANES), 0)
    q_io = lax.broadcasted_iota(jnp.int32, (n_blk, LANES), 1)
    col_w = lax.broadcasted_iota(jnp.int32, (nl, HD), 1)
    pad_cols = jnp.where(col_w == PAD_COL, NEG_INF, 0.0).astype(BF16)

    def softmax_av(tiles, vt):
        s_all = jnp.concatenate(tiles, axis=0)
        m = jnp.max(s_all, axis=0, keepdims=True)
        p = jnp.exp2(s_all - m)
        return m, jnp.dot(vt, p.astype(BF16), preferred_element_type=F32)

    def normalise(av):
        return av[0:HD] / av[HD:HD + 1]

    def logit_tiles(k_ref, qq, row0, n):
        s_all = lax.dot_general(k_ref[pl.ds(row0, n * LANES), :], qq, _NT, preferred_element_type=F32)
        return [s_all[t * LANES:(t + 1) * LANES] for t in range(n)]

    def stage_cmp(i, q_ref, bc_ref):
        qt = q_ref[...]
        q = jnp.concatenate([qt[:, r * HD:(r + 1) * HD] for r in range(Q_PER_KV)], axis=0) * SCALE
        s = lax.dot_general(kc_ref[...], q, _NT, preferred_element_type=F32) + bc_ref[...]
        m_c = jnp.max(s, axis=0, keepdims=True)
        p = jnp.exp(s - m_c)
        l_c = jnp.sum(p, axis=0, keepdims=True)
        p_c = p * (jnp.where(m_c > 0.5 * NEG_INF, 1.0, 0.0) / l_c)
        o_c = jnp.dot(vct_ref[...], p_c.astype(BF16), preferred_element_type=F32)
        return dict(i=i, q2=(q.astype(F32) * LOG2E).astype(BF16), p_c=p_c, o_c=o_c)

    def stage_window_logits(st):
        r0 = pl.multiple_of(st["i"] * LANES, LANES)
        tw = logit_tiles(kwa_ref, jnp.concatenate([st["q2"], pad_cols], axis=1), r0, n_win)
        tw[0] = tw[0] + tt_ref[2]
        tw[n_win - 2] = tw[n_win - 2] + tt_ref[0]
        tw[n_win - 1] = tw[n_win - 1] + tt_ref[1]
        st["tw"] = tw

    def stage_window_softmax(st):
        r0 = pl.multiple_of(st["i"] * LANES, LANES)
        _, av_w = softmax_av(st.pop("tw"), vwt_ref[:, pl.ds(r0, n_win * LANES)])
        st["o_w"] = normalise(av_w)

    def stage_select(st, slot):
        p_c = st.pop("p_c")
        p_sum = p_c[:, 0:LANES]
        for r in range(1, Q_PER_KV):
            p_sum = p_sum + p_c[:, r * LANES:(r + 1) * LANES]
        imp = jnp.dot(ovt_ref[...], p_sum, precision=lax.Precision.HIGHEST, preferred_element_type=F32)
        t_b = st["i"] * Q_BLOCK + q_io
        cur = t_b >> 6
        forced = (j_io == 0) | (j_io == cur) | (j_io == cur - 1)
        valid = j_io * SEL_BLOCK <= t_b
        score = jnp.where(forced, FORCE, jnp.where(valid, imp, -FORCE))
        rank = jnp.zeros((n_blk, LANES), jnp.int32)
        for ii in range(n_blk):
            row = score[ii:ii + 1, :]
            beats = (row > score) | ((row == score) & (j_io > ii))
            rank = rank + beats.astype(jnp.int32)
        keep = (rank < min(SEL_TOP_N, n_blk)) & (score >= 0.0)

        def augment(kept):
            cols = jnp.concatenate([jnp.where(kept, 0.0, NEG_INF), jnp.full((8, LANES), NEG_INF, F32),
                                    jnp.zeros((LANES - n_blk - 8, LANES), F32)], axis=0)
            cols_t = cols.T[:, 0:HD].astype(BF16)
            return jnp.concatenate([st["q2"], jnp.concatenate([cols_t] * Q_PER_KV, axis=0)], axis=1)

        st["q_near"] = augment(keep)
        qf_ref[slot] = augment(keep & (j_io < 2 * (st["i"] - 1)))

    def stage_near_logits(st):
        r0 = pl.multiple_of(st["i"] * LANES, LANES)
        tn = logit_tiles(ksa_ref, st.pop("q_near"), r0, 2)
        tn[0] = tn[0] + tt_ref[0]
        tn[1] = tn[1] + tt_ref[1]
        st["tn"] = tn

    def stage_near_softmax(st):
        r0 = pl.multiple_of(st["i"] * LANES, LANES)
        st["m_n"], st["av_n"] = softmax_av(st.pop("tn"), vst_ref[:, pl.ds(r0, 2 * LANES)])

    i_a = p_id
    i_b = n_qb - 1 - p_id
    st_a = stage_cmp(i_a, qa_ref, bca_ref)
    st_b = stage_cmp(i_b, qb_ref, bcb_ref)
    stage_window_logits(st_a)
    stage_window_logits(st_b)
    stage_select(st_a, 0)
    stage_window_softmax(st_a)
    stage_select(st_b, 1)
    stage_window_softmax(st_b)
    stage_near_logits(st_a)
    stage_near_logits(st_b)
    stage_near_softmax(st_a)
    stage_near_softmax(st_b)
    oc_a, ow_a, mn_a, avn_a = st_a["o_c"], st_a["o_w"], st_a["m_n"], st_a["av_n"]
    oc_b, ow_b, mn_b, avn_b = st_b["o_c"], st_b["o_w"], st_b["m_n"], st_b["av_n"]


    n_a = (jnp.maximum(i_a - 1, 0) + FAR_TILES - 1) // FAR_TILES
    is_a, rows, mx = [], [], []
    for k in range(max_chunks):
        own = k < n_a
        c = jnp.where(own, k, k - n_a)
        r = pl.multiple_of((c * FAR_TILES + 1) * LANES, LANES)
        s_k = lax.dot_general(ksa_ref[pl.ds(r, FAR_TILES * LANES), :], qf_ref[jnp.where(own, 0, 1)], _NT,
                              preferred_element_type=F32)
        sbuf_ref[k] = s_k
        is_a.append(own)
        rows.append(r)
        mx.append(jnp.max(s_k, axis=0, keepdims=True))
    m_a, m_b = mn_a, mn_b
    for k in range(max_chunks):
        m_a = jnp.where(is_a[k], jnp.maximum(m_a, mx[k]), m_a)
        m_b = jnp.where(is_a[k], m_b, jnp.maximum(m_b, mx[k]))
    av_a = jnp.exp2(mn_a - m_a) * avn_a
    av_b = jnp.exp2(mn_b - m_b) * avn_b
    for k in range(max_chunks):
        p = jnp.exp2(sbuf_ref[k] - jnp.where(is_a[k], m_a, m_b))
        pv = jnp.dot(vst_ref[:, pl.ds(rows[k], FAR_TILES * LANES)], p.astype(BF16), preferred_element_type=F32)
        av_a = av_a + jnp.where(is_a[k], pv, 0.0)
        av_b = av_b + jnp.where(is_a[k], 0.0, pv)

    def finish(o_ref, gt_ref, o_c, o_s, o_w):
        g = gt_ref[...]
        o_t = g[0:1] * o_c + g[1:2] * o_s + g[2:3] * o_w
        for pair in range(Q_PER_KV // 2):
            blk = jnp.concatenate([o_t[:, (2 * pair) * LANES:(2 * pair + 1) * LANES],
                                   o_t[:, (2 * pair + 1) * LANES:(2 * pair + 2) * LANES]], axis=0)
            o_ref[:, pair * LANES:(pair + 1) * LANES] = blk.T.astype(o_ref.dtype)

    finish(oa_ref, gta_ref, oc_a, normalise(av_a), ow_a)
    finish(ob_ref, gtb_ref, oc_b, normalise(av_b), ow_b)


def _nsa(proj, kc, vct, ksa, vst, kwa, vwt, gt, tt, bc, ovt, s_):
    b_, g_ = kwa.shape[:2]
    n_qb = s_ // Q_BLOCK
    n_p = n_qb // 2
    m = proj.shape[0]
    nl = Q_PER_KV * LANES
    qcol = SEG_Q * D_MODEL // (Q_PER_KV * HD)
    max_chunks = s_ // (FAR_TILES * LANES)

    def far_chunks(i):
        return (max(i - 1, 0) + FAR_TILES - 1) // FAR_TILES

    assert n_qb % 2 == 0 and all(far_chunks(p) + far_chunks(n_qb - 1 - p) == max_chunks for p in range(n_p))

    def bg(shape):
        return pl.BlockSpec((None, None) + shape, lambda b, g, p: (b, g, 0, 0))

    def blk_a(p):
        return p

    def blk_b(p):
        return n_qb - 1 - p

    def per_block(which):
        return [
            pl.BlockSpec((Q_BLOCK, Q_PER_KV * HD), lambda b, g, p: (b * n_qb + which(p), qcol + g)),
            pl.BlockSpec((None, None, None, 3, nl), lambda b, g, p: (b, g, which(p), 0, 0)),
            pl.BlockSpec((None, None, LANES, nl), lambda b, g, p: (g, which(p), 0, 0)),
        ]

    qa, gta, bca = per_block(blk_a)
    qb, gtb, bcb = per_block(blk_b)
    half = jax.ShapeDtypeStruct((m // 2, D_MODEL), BF16)
    out_a, out_b = pl.pallas_call(
        _nsa_body,
        grid=(b_, g_, n_p),
        in_specs=[
            qa, qb,
            bg((LANES, HD)), bg((HD, LANES)),
            bg((s_ + LANES, 2 * HD)), bg((HD + ONES_ROWS, s_ + LANES)),
            bg((s_ + WINDOW, 2 * HD)), bg((HD + ONES_ROWS, s_ + WINDOW)),
            gta, gtb,
            pl.BlockSpec((None, 3, LANES, nl), lambda b, g, p: (g, 0, 0, 0)),
            bca, bcb,
            pl.BlockSpec((s_ // SEL_BLOCK, LANES), lambda b, g, p: (0, 0)),
        ],
        out_specs=[
            pl.BlockSpec((Q_BLOCK, Q_PER_KV * HD), lambda b, g, p: (b * n_p + p, g)),
            pl.BlockSpec((Q_BLOCK, Q_PER_KV * HD), lambda b, g, p: (b * n_p + n_p - 1 - p, g)),
        ],
        out_shape=[half, half],
        scratch_shapes=[pltpu.VMEM((max_chunks, FAR_TILES * LANES, nl), F32), pltpu.VMEM((2, nl, 2 * HD), BF16)],
        compiler_params=_params(("parallel", "parallel", "arbitrary")),
        name="nsa",
    )(proj, proj, kc, vct, ksa, vst, kwa, vwt, gt, gt, tt, bc, bc, ovt)
    return jnp.concatenate([out_a.reshape(b_, s_ // 2, D_MODEL), out_b.reshape(b_, s_ // 2, D_MODEL)],
                           axis=1).reshape(m, D_MODEL)


def _mix_body(gb_ref, gc_ref, va_ref, gch_ref, vah_ref, uc_ref, vc_ref, g0_ref, g1_ref, g2_ref, ob_ref, x_ref,
              ca_ref, sw_ref, sb_ref, lng_ref, lnb_ref, wa_ref, wb_ref, wc_ref, wo_ref, o_ref, xa_ref, mx_ref):
    tm = gb_ref.shape[0]
    first = pl.program_id(1) == 0

    halo = gch_ref[...].astype(F32) * vah_ref[...].astype(F32)
    xa_ref[0:HALO, :] = jnp.where(first, 0.0, halo)
    xa_ref[HALO:, :] = gc_ref[...].astype(F32) * va_ref[...].astype(F32)
    ca = ca_ref[...]
    row = lax.broadcasted_iota(jnp.int32, (CHUNK, CHUNK), 0)
    col = lax.broadcasted_iota(jnp.int32, (CHUNK, CHUNK), 1)
    sw = [jnp.where(row >= col, sw_ref[gi], 0.0).astype(BF16) for gi in range(SGU_GROUPS)]

    for r in range(0, tm, ROW_CHUNK):
        rows = slice(r, r + ROW_CHUNK)
        conv = (ca[0:1] * xa_ref[pl.ds(r + HALO - 2, ROW_CHUNK), :]
                + ca[1:2] * xa_ref[pl.ds(r + HALO - 1, ROW_CHUNK), :]
                + ca[2:3] * xa_ref[pl.ds(r + HALO, ROW_CHUNK), :])
        out_a = gb_ref[rows, :].astype(F32) * conv
        y_a = jnp.dot(out_a.astype(BF16), wa_ref[...], preferred_element_type=F32)

        v = vc_ref[rows, :].astype(F32)
        mu = jnp.mean(v, axis=-1, keepdims=True)
        vz = v - mu
        var = jnp.mean(vz * vz, axis=-1, keepdims=True)
        vn = (vz * lax.rsqrt(var + EPS) * lng_ref[...] + lnb_ref[...]).astype(BF16)
        for gi in range(SGU_GROUPS):
            cols = slice(gi * LANES, (gi + 1) * LANES)
            for c in range(0, ROW_CHUNK, CHUNK):
                mx_ref[r + c:r + c + CHUNK, cols] = (
                    jnp.dot(sw[gi], vn[c:c + CHUNK, cols], preferred_element_type=F32) + sb_ref[:, cols])
        out_c = uc_ref[rows, :].astype(F32) * mx_ref[rows, :]
        y_c = jnp.dot(out_c.astype(BF16), wc_ref[...], preferred_element_type=F32)

        y_b = jnp.dot(ob_ref[rows, :], wb_ref[...], preferred_element_type=F32)

        merged = (g0_ref[rows, :].astype(F32) * y_a + g1_ref[rows, :].astype(F32) * y_b
                  + g2_ref[rows, :].astype(F32) * y_c)
        o_ref[rows, :] = x_ref[rows, :] + jnp.dot(merged.astype(BF16), wo_ref[...], preferred_element_type=F32)


def _mix(proj, out_b, x2, seq, conv_a, sgu_w, sgu_b2, ln_g, ln_b, w_a, w_b, w_c, w_o, tm=512):
    m = x2.shape[0]
    n_t = seq // tm
    hb = tm // HALO

    def seg(k):
        return pl.BlockSpec((tm, D_MODEL), lambda b, i: (b * n_t + i, k))

    def halo(k):
        return pl.BlockSpec((HALO, D_MODEL), lambda b, i: (jnp.maximum((b * n_t + i) * hb - 1, 0), k))

    def full(shape):
        return pl.BlockSpec(shape, lambda b, i: (0,) * len(shape))

    row = pl.BlockSpec((tm, D_MODEL), lambda b, i: (b * n_t + i, 0))
    wspec = pl.BlockSpec((D_MODEL, D_MODEL), lambda b, i: (0, 0), pipeline_mode=pl.Buffered(1))
    return pl.pallas_call(
        _mix_body,
        grid=(m // seq, n_t),
        in_specs=[
            seg(SEG_GB), seg(SEG_GC), seg(SEG_VA), halo(SEG_GC), halo(SEG_VA), seg(SEG_U), seg(SEG_V),
            seg(SEG_G0), seg(SEG_G0 + 1), seg(SEG_G0 + 2), row, row,
            full((CONV_W, D_MODEL)), full((SGU_GROUPS, CHUNK, CHUNK)), full((CHUNK, D_MODEL)),
            full((1, D_MODEL)), full((1, D_MODEL)), wspec, wspec, wspec, wspec,
        ],
        out_specs=row,
        out_shape=jax.ShapeDtypeStruct((m, D_MODEL), F32),
        scratch_shapes=[pltpu.VMEM((tm + HALO, D_MODEL), F32), pltpu.VMEM((tm, D_MODEL), F32)],
        compiler_params=_params(("parallel", "arbitrary")),
        name="mix",
    )(proj, proj, proj, proj, proj, proj, proj, proj, proj, proj, out_b, x2,
      conv_a, sgu_w, sgu_b2, ln_g, ln_b, w_a, w_b, w_c, w_o)


def _ffn_body(x_ref, xh_ref, g_ref, wg_ref, wv_ref, cw_ref, wd_ref, o_ref, h_ref, gt_ref, acc_ref, *, tiles_per_seq):
    i = pl.program_id(0)
    j = pl.program_id(1)
    tm = x_ref.shape[0]

    @pl.when(j == 0)
    def _():
        h_ref[0:HALO, :] = _rms(xh_ref[...], g_ref[...]).astype(BF16)
        h_ref[HALO:, :] = _rms(x_ref[...], g_ref[...]).astype(BF16)
        acc_ref[...] = jnp.zeros_like(acc_ref)

    gate = jnp.dot(h_ref[...], wg_ref[...], preferred_element_type=F32)
    rows = lax.broadcasted_iota(jnp.int32, gate.shape, 0)
    seq_start = (i % tiles_per_seq) == 0
    gt_ref[...] = jnp.where(seq_start & (rows < HALO), 0.0, gate)
    cw = cw_ref[...]
    conv = (cw[0:1] * gt_ref[pl.ds(HALO - 2, tm), :] + cw[1:2] * gt_ref[pl.ds(HALO - 1, tm), :]
            + cw[2:3] * gt_ref[pl.ds(HALO, tm), :])
    val = jnp.dot(h_ref[HALO:, :], wv_ref[...], preferred_element_type=F32)
    act = (jax.nn.gelu(conv) * val).astype(BF16)
    acc_ref[...] += jnp.dot(act, wd_ref[...], preferred_element_type=F32)

    @pl.when(j == pl.num_programs(1) - 1)
    def _():
        o_ref[...] = x_ref[...] + acc_ref[...]


def _ffn(x2, seq, g, w_up, conv_w, w_down, tm=1024, tf=512):
    m = x2.shape[0]
    n_f = D_FF // tf
    hb = tm // HALO
    return pl.pallas_call(
        functools.partial(_ffn_body, tiles_per_seq=seq // tm),
        grid=(m // tm, n_f),
        in_specs=[
            pl.BlockSpec((tm, D_MODEL), lambda i, j: (i, 0)),
            pl.BlockSpec((HALO, D_MODEL), lambda i, j: (jnp.maximum(i * hb - 1, 0), 0)),
            pl.BlockSpec((1, D_MODEL), lambda i, j: (0, 0)),
            pl.BlockSpec((D_MODEL, tf), lambda i, j: (0, j)),
            pl.BlockSpec((D_MODEL, tf), lambda i, j: (0, n_f + j)),
            pl.BlockSpec((CONV_W, tf), lambda i, j: (0, j)),
            pl.BlockSpec((tf, D_MODEL), lambda i, j: (j, 0)),
        ],
        out_specs=pl.BlockSpec((tm, D_MODEL), lambda i, j: (i, 0)),
        out_shape=jax.ShapeDtypeStruct((m, D_MODEL), F32),
        scratch_shapes=[
            pltpu.VMEM((tm + HALO, D_MODEL), BF16),
            pltpu.VMEM((tm + HALO, tf), F32),
            pltpu.VMEM((tm, D_MODEL), F32),
        ],
        compiler_params=_params(("parallel", "arbitrary")),
        name="ffn",
    )(x2, x2, g, w_up, w_up, conv_w, w_down)


def _norm_body(x_ref, g_ref, o_ref):
    o_ref[...] = _rms(x_ref[...], g_ref[...])


def _final_norm(x2, g, tm=1024):
    m = x2.shape[0]
    return pl.pallas_call(
        _norm_body,
        grid=(m // tm,),
        in_specs=[pl.BlockSpec((tm, D_MODEL), lambda i: (i, 0)), pl.BlockSpec((1, D_MODEL), lambda i: (0, 0))],
        out_specs=pl.BlockSpec((tm, D_MODEL), lambda i: (i, 0)),
        out_shape=jax.ShapeDtypeStruct((m, D_MODEL), F32),
        compiler_params=_params(("parallel",)),
        name="final_norm",
    )(x2, g)


def _split_w_in(w_in):
    c = [0]
    for sz in [D_MODEL] * 4 + [N_KV * HD] * 6 + [3 * N_HEADS] + [D_MODEL] * 5:
        c.append(c[-1] + sz)
    main = jnp.concatenate([w_in[:, c[0]:c[4]], w_in[:, c[11]:c[16]], w_in[:, c[4]:c[10]]], axis=1)
    gate = jnp.pad(w_in[:, c[10]:c[11]], ((0, 0), (0, LANES - 3 * N_HEADS)))
    return main.astype(BF16), gate.astype(BF16)


def _pad_keys(k, n_pad, with_block_id):
    b_, s_, g_, _ = k.shape
    pos = jnp.arange(-n_pad, s_)[:, None]
    col = jnp.arange(HD)[None, :]
    extra = (pos < 0) & (col == PAD_COL)
    if with_block_id:
        extra = extra | ((pos >= 0) & (pos // SEL_BLOCK == col))
    kp = jnp.pad(k.transpose(0, 2, 1, 3), ((0, 0), (0, 0), (n_pad, 0), (0, 0)))
    return jnp.concatenate([kp, jnp.broadcast_to(extra.astype(k.dtype), (b_, g_, n_pad + s_, HD))], axis=-1)


def _pad_values(v, n_pad):
    b_, s_, g_, _ = v.shape
    vt = jnp.pad(v.transpose(0, 2, 3, 1), ((0, 0), (0, 0), (0, 0), (n_pad, 0)))
    ones = (jnp.arange(ONES_ROWS)[:, None] == 0).astype(v.dtype)
    return jnp.concatenate([vt, jnp.broadcast_to(ones, (b_, g_, ONES_ROWS, n_pad + s_))], axis=2)


def _mixer_layer(x2, b_, s_, tt, bc, ovt, norm_g, w_in, conv_a, cmp_pos, cmp_w1, cmp_w2, sgu_w, sgu_b,
                 sgu_norm_g, sgu_norm_b, w_br_a, w_br_b, w_br_c, w_o):
    n_qb = s_ // Q_BLOCK
    w_main, w_gate = _split_w_in(w_in)
    proj, g_nsa = _in_proj(x2, norm_g.reshape(1, D_MODEL), w_main, w_gate)

    kv = proj[:, KV_OFF:].reshape(b_, s_, 6, N_KV, HD)
    kc, vc, ks, vs, kw, vw = [kv[:, :, n] for n in range(6)]

    def rows16(a):
        return a.transpose(0, 2, 1, 3).reshape(b_, N_KV, s_ // CMP_STRIDE, CMP_STRIDE * HD)

    k_cmp, v_cmp = _compress(rows16(kc), rows16(vc), cmp_pos.reshape(2, 1, CMP_LEN * HD),
                             cmp_w1.astype(BF16), cmp_w2.astype(BF16))
    gt = (g_nsa[:, :3 * N_HEADS].reshape(b_, n_qb, Q_BLOCK, 3, N_KV, Q_PER_KV)
          .transpose(0, 4, 1, 3, 5, 2).reshape(b_, N_KV, n_qb, 3, Q_PER_KV * Q_BLOCK))
    out_b = _nsa(proj, k_cmp, v_cmp.transpose(0, 1, 3, 2),
                 _pad_keys(ks, LANES, True), _pad_values(vs, LANES),
                 _pad_keys(kw, WINDOW, False), _pad_values(vw, WINDOW), gt, tt, bc, ovt, s_)

    sgu_b2 = jnp.broadcast_to(sgu_b.T[:, :, None], (CHUNK, SGU_GROUPS, D_MODEL // SGU_GROUPS)).reshape(CHUNK, D_MODEL)
    return _mix(proj, out_b, x2, s_, conv_a, sgu_w, sgu_b2,
                sgu_norm_g.reshape(1, D_MODEL), sgu_norm_b.reshape(1, D_MODEL),
                w_br_a.astype(BF16), w_br_b.astype(BF16), w_br_c.astype(BF16), w_o.astype(BF16))


def _overlap_t(s_):
    n_blk = s_ // SEL_BLOCK
    cmp_start = jnp.arange(LANES) * CMP_STRIDE
    cmp_end = cmp_start + CMP_LEN - 1
    blk_start = jnp.arange(n_blk) * SEL_BLOCK
    n_cmp = (s_ - CMP_LEN) // CMP_STRIDE + 1
    ov = ((cmp_start[None, :] < blk_start[:, None] + SEL_BLOCK) & (cmp_end[None, :] >= blk_start[:, None])
          & (jnp.arange(LANES)[None, :] < n_cmp))
    return ov.astype(F32)


def kernel(x, rel_bias, norm_mix, w_in, conv_a, cmp_pos, cmp_w1, cmp_w2, sgu_w, sgu_b, sgu_norm_g, sgu_norm_b,
           w_br_a, w_br_b, w_br_c, w_o, norm_ffn, ffn_w_up, ffn_conv, ffn_w_down, norm_final):
    b_, s_, _ = x.shape
    depth = w_in.shape[0]
    x2 = x.reshape(b_ * s_, D_MODEL)
    tt, bc = _bias_tiles(rel_bias, s_ // Q_BLOCK)
    ovt = _overlap_t(s_)
    for l in range(depth):
        x2 = _mixer_layer(x2, b_, s_, tt, bc, ovt, norm_mix[l], w_in[l], conv_a[l], cmp_pos[l], cmp_w1[l],
                          cmp_w2[l], sgu_w[l], sgu_b[l], sgu_norm_g[l], sgu_norm_b[l],
                          w_br_a[l], w_br_b[l], w_br_c[l], w_o[l])
        x2 = _ffn(x2, s_, norm_ffn[l].reshape(1, D_MODEL), ffn_w_up[l].astype(BF16), ffn_conv[l],
                  ffn_w_down[l].astype(BF16))
    return _final_norm(x2, norm_final.reshape(1, D_MODEL)).reshape(b_, s_, D_MODEL)
```

```python
import functools
import math

import jax
import jax.numpy as jnp
from jax import lax
from jax.experimental import pallas as pl
from jax.experimental.pallas import tpu as pltpu

D_MODEL = 1024
HD = 64
N_HEADS = D_MODEL // HD
N_KV = 4
Q_PER_KV = N_HEADS // N_KV
SGU_GROUPS = 8
CHUNK = 128
CONV_W = 3
CMP_LEN = 32
CMP_STRIDE = 16
CMP_HID = 2 * HD
SEL_BLOCK = 64
SEL_TOP_N = 8
WINDOW = 512
Q_BLOCK = 128
D_FF = 3 * D_MODEL
N_BUCKETS = 32
MAX_EXACT = N_BUCKETS // 2
REL_MAX_DIST = 128
SCALE = HD ** -0.5
EPS = 1e-6
NEG_INF = -1e30
FORCE = 1e9
LOG2E = 1.4426950408889634

LANES = 128
ROW_CHUNK = 256
HALO = 16
N_SEG = 9
SEG_GB, SEG_GC, SEG_VA, SEG_Q, SEG_U, SEG_V, SEG_G0 = 0, 1, 2, 3, 4, 5, 6
N_MAIN = N_SEG * D_MODEL
N_KVCOL = 6 * N_KV * HD
KV_PAD = WINDOW
VMEM_LIMIT = 56 * 1024 * 1024

BF16 = jnp.bfloat16
F32 = jnp.float32
_NT = (((1,), (1,)), ((), ()))


def _params(sem):
    return pltpu.CompilerParams(dimension_semantics=sem, vmem_limit_bytes=VMEM_LIMIT)


def _rms(x, g):
    return x * lax.rsqrt(jnp.mean(x * x, axis=-1, keepdims=True) + EPS) * g


def _in_proj_body(x_ref, g_ref, w_ref, o_ref, h_ref, *, gelu_tiles, sigmoid_tiles):
    j = pl.program_id(1)

    @pl.when(j == 0)
    def _():
        h_ref[...] = _rms(x_ref[...], g_ref[...]).astype(BF16)

    def tile(act):
        for r in range(0, h_ref.shape[0], ROW_CHUNK):
            y = jnp.dot(h_ref[r:r + ROW_CHUNK, :], w_ref[...], preferred_element_type=F32)
            o_ref[r:r + ROW_CHUNK, :] = act(y).astype(o_ref.dtype)

    is_gelu = (j >= gelu_tiles[0]) & (j < gelu_tiles[1])
    is_sigmoid = (j >= sigmoid_tiles[0]) & (j < sigmoid_tiles[1])
    pl.when(is_gelu)(lambda: tile(jax.nn.gelu))
    pl.when(is_sigmoid)(lambda: tile(jax.nn.sigmoid))
    pl.when(jnp.logical_not(is_gelu | is_sigmoid))(lambda: tile(lambda y: y))


def _in_proj(x2, g, w_main, tm=1024, tn=512):
    m = x2.shape[0]
    per_seg = D_MODEL // tn
    body = functools.partial(_in_proj_body, gelu_tiles=(SEG_U * per_seg, (SEG_V + 1) * per_seg),
                             sigmoid_tiles=(SEG_G0 * per_seg, (SEG_G0 + 3) * per_seg))
    return pl.pallas_call(
        body,
        grid=(m // tm, N_MAIN // tn),
        in_specs=[
            pl.BlockSpec((tm, D_MODEL), lambda i, j: (i, 0)),
            pl.BlockSpec((1, D_MODEL), lambda i, j: (0, 0)),
            pl.BlockSpec((D_MODEL, tn), lambda i, j: (0, j)),
        ],
        out_specs=pl.BlockSpec((tm, tn), lambda i, j: (i, j)),
        out_shape=jax.ShapeDtypeStruct((m, N_MAIN), BF16),
        scratch_shapes=[pltpu.VMEM((tm, D_MODEL), BF16)],
        compiler_params=_params(("parallel", "arbitrary")),
        name="in_proj",
    )(x2, g, w_main)


def _kv_proj_body(x_ref, g_ref, w_ref, wg_ref, kc_ref, vc_ref, ksa_ref, vst_ref, kwa_ref, vwt_ref, gt_ref):
    s_id = pl.program_id(1)
    tm = x_ref.shape[0]
    lane = lax.broadcasted_iota(jnp.int32, (tm, LANES), 1)
    ones_rows = (lax.broadcasted_iota(jnp.int32, (ONES_ROWS, tm), 0) == 0).astype(BF16)

    @pl.when(s_id == 0)
    def _():
        pad_keys = jnp.where(lane == HD + PAD_COL, 1.0, 0.0).astype(BF16)
        for g in range(N_KV):
            ksa_ref[g] = pad_keys
            kwa_ref[g] = pad_keys
            vst_ref[g] = jnp.zeros(vst_ref.shape[1:], BF16)
            vwt_ref[g] = jnp.zeros(vwt_ref.shape[1:], BF16)

    @pl.when(s_id > 0)
    def _():
        h = _rms(x_ref[...], g_ref[...]).astype(BF16)
        y = jnp.dot(h, w_ref[...], preferred_element_type=F32)
        tok = (s_id - 1) * tm + lax.broadcasted_iota(jnp.int32, (tm, LANES), 0)
        blk_id = jnp.where(lane - HD == tok // SEL_BLOCK, 1.0, 0.0)
        for g in range(N_KV):
            base = g * 6 * HD
            sel = y[:, base:base + 2 * HD]
            win = y[:, base + 2 * HD:base + 4 * HD]
            cmp = y[:, base + 4 * HD:base + 6 * HD]
            ksa_ref[g] = jnp.where(lane < HD, sel, blk_id).astype(BF16)
            kwa_ref[g] = jnp.where(lane < HD, win, 0.0).astype(BF16)
            vst_ref[g, 0:HD, :] = sel.T[HD:].astype(BF16)
            vst_ref[g, HD:, :] = ones_rows
            vwt_ref[g, 0:HD, :] = win.T[HD:].astype(BF16)
            vwt_ref[g, HD:, :] = ones_rows
            kc_ref[g] = cmp[:, 0:HD].astype(BF16)
            vc_ref[g] = cmp[:, HD:].astype(BF16)
        gt_ref[...] = jax.nn.sigmoid(jnp.dot(h, wg_ref[...], preferred_element_type=F32)).T


def _kv_proj(x2, b_, s_, g, w_kv, w_gate, tm=512):
    n_t = s_ // tm
    assert KV_PAD == tm

    def tok(b, s):
        return jnp.maximum(s - 1, 0)

    return pl.pallas_call(
        _kv_proj_body,
        grid=(b_, n_t + 1),
        in_specs=[
            pl.BlockSpec((tm, D_MODEL), lambda b, s: (b * n_t + tok(b, s), 0)),
            pl.BlockSpec((1, D_MODEL), lambda b, s: (0, 0)),
            pl.BlockSpec((D_MODEL, N_KVCOL), lambda b, s: (0, 0), pipeline_mode=pl.Buffered(1)),
            pl.BlockSpec((D_MODEL, LANES), lambda b, s: (0, 0), pipeline_mode=pl.Buffered(1)),
        ],
        out_specs=[
            pl.BlockSpec((None, N_KV, tm, HD), lambda b, s: (b, 0, tok(b, s), 0)),
            pl.BlockSpec((None, N_KV, tm, HD), lambda b, s: (b, 0, tok(b, s), 0)),
            pl.BlockSpec((None, N_KV, tm, 2 * HD), lambda b, s: (b, 0, s, 0)),
            pl.BlockSpec((None, N_KV, HD + ONES_ROWS, tm), lambda b, s: (b, 0, 0, s)),
            pl.BlockSpec((None, N_KV, tm, 2 * HD), lambda b, s: (b, 0, s, 0)),
            pl.BlockSpec((None, N_KV, HD + ONES_ROWS, tm), lambda b, s: (b, 0, 0, s)),
            pl.BlockSpec((None, LANES, tm), lambda b, s: (b, 0, tok(b, s))),
        ],
        out_shape=[
            jax.ShapeDtypeStruct((b_, N_KV, s_, HD), BF16),
            jax.ShapeDtypeStruct((b_, N_KV, s_, HD), BF16),
            jax.ShapeDtypeStruct((b_, N_KV, KV_PAD + s_, 2 * HD), BF16),
            jax.ShapeDtypeStruct((b_, N_KV, HD + ONES_ROWS, KV_PAD + s_), BF16),
            jax.ShapeDtypeStruct((b_, N_KV, KV_PAD + s_, 2 * HD), BF16),
            jax.ShapeDtypeStruct((b_, N_KV, HD + ONES_ROWS, KV_PAD + s_), BF16),
            jax.ShapeDtypeStruct((b_, LANES, s_), F32),
        ],
        compiler_params=_params(("parallel", "arbitrary")),
        name="kv_proj",
    )(x2, g, w_kv, w_gate)


def _rel_bucket(dist):
    dist = jnp.maximum(dist, 0)
    log_ratio = jnp.log(jnp.maximum(dist, 1).astype(F32) / MAX_EXACT) / math.log(REL_MAX_DIST / MAX_EXACT)
    large = MAX_EXACT + (log_ratio * (N_BUCKETS - MAX_EXACT)).astype(jnp.int32)
    return jnp.where(dist < MAX_EXACT, dist, jnp.minimum(large, N_BUCKETS - 1))


def _bias_body(tab_ref, bt_ref, bc_ref, tt_ref, bco_ref, *, n_cmp):
    h = pl.program_id(0)

    def lookup(bk):
        acc = jnp.zeros(bk.shape, F32)
        for b in range(N_BUCKETS):
            acc = jnp.where(bk == b, tab_ref[b, h], acc)
        return acc

    kk = lax.broadcasted_iota(jnp.int32, (LANES, LANES), 0)
    qq = lax.broadcasted_iota(jnp.int32, (LANES, LANES), 1)
    far = lookup(bt_ref[2])
    tt_ref[0] = (lookup(bt_ref[1]) - far) * LOG2E
    tt_ref[1] = jnp.where(qq >= kk, (lookup(bt_ref[0]) - far) * LOG2E, NEG_INF)
    tt_ref[2] = jnp.where(kk > qq, 0.0, NEG_INF)
    for i in range(bc_ref.shape[0]):
        ok = (i * Q_BLOCK + qq - (kk * CMP_STRIDE + CMP_LEN - 1) >= 0) & (kk < n_cmp)
        bco_ref[i] = jnp.where(ok, lookup(bc_ref[i]), NEG_INF)


def _bias_tiles(rel_bias, n_qb):
    kk = jnp.arange(LANES)[:, None]
    qq = jnp.arange(LANES)[None, :]
    bt = jnp.stack([_rel_bucket(LANES * d + qq - kk) for d in range(3)]).astype(jnp.int32)
    cmp_end = CMP_STRIDE * jnp.arange(LANES) + CMP_LEN - 1
    t = (Q_BLOCK * jnp.arange(n_qb))[:, None, None] + qq[None]
    bc = _rel_bucket(t - cmp_end[None, :, None]).astype(jnp.int32)
    n_cmp = (n_qb * Q_BLOCK - CMP_LEN) // CMP_STRIDE + 1
    return pl.pallas_call(
        functools.partial(_bias_body, n_cmp=n_cmp),
        grid=(N_HEADS,),
        in_specs=[
            pl.BlockSpec(memory_space=pltpu.SMEM),
            pl.BlockSpec((3, LANES, LANES), lambda h: (0, 0, 0)),
            pl.BlockSpec((n_qb, LANES, LANES), lambda h: (0, 0, 0)),
        ],
        out_specs=[
            pl.BlockSpec((None, 3, LANES, LANES), lambda h: (h // Q_PER_KV, 0, 0, h % Q_PER_KV)),
            pl.BlockSpec((None, n_qb, LANES, LANES), lambda h: (h // Q_PER_KV, 0, 0, h % Q_PER_KV)),
        ],
        out_shape=[
            jax.ShapeDtypeStruct((N_KV, 3, LANES, Q_PER_KV * LANES), F32),
            jax.ShapeDtypeStruct((N_KV, n_qb, LANES, Q_PER_KV * LANES), F32),
        ],
        compiler_params=_params(("arbitrary",)),
        name="bias_tiles",
    )(rel_bias.astype(F32), bt, bc)


def _compress_body(k3_ref, v3_ref, pos_ref, w1_ref, w2_ref, ko_ref, vo_ref):
    half = CMP_STRIDE * HD
    for which, (src, dst) in enumerate(((k3_ref, ko_ref), (v3_ref, vo_ref))):
        x = src[...].astype(F32)
        pos = pos_ref[which]
        xa = (x + pos[:, :half]).astype(BF16)
        xb = (x + pos[:, half:]).astype(BF16)
        a = jnp.dot(xa, w1_ref[which, :half, :], preferred_element_type=F32)
        b = jnp.dot(xb, w1_ref[which, half:, :], preferred_element_type=F32)
        hid = jax.nn.gelu(a + pltpu.roll(b, LANES - 1, 0))
        dst[...] = jnp.dot(hid.astype(BF16), w2_ref[which], preferred_element_type=F32).astype(dst.dtype)


def _compress(k3, v3, pos, w1, w2):
    b_, g_, nm, width = k3.shape
    spec3 = pl.BlockSpec((None, None, nm, width), lambda b, g: (b, g, 0, 0))
    ospec = pl.BlockSpec((None, None, nm, HD), lambda b, g: (b, g, 0, 0))
    return pl.pallas_call(
        _compress_body,
        grid=(b_, g_),
        in_specs=[
            spec3, spec3,
            pl.BlockSpec((2, 1, CMP_LEN * HD), lambda b, g: (0, 0, 0)),
            pl.BlockSpec((2, CMP_LEN * HD, CMP_HID), lambda b, g: (0, 0, 0)),
            pl.BlockSpec((2, CMP_HID, HD), lambda b, g: (0, 0, 0)),
        ],
        out_specs=[ospec, ospec],
        out_shape=[jax.ShapeDtypeStruct((b_, g_, nm, HD), BF16)] * 2,
        compiler_params=_params(("parallel", "parallel")),
        name="compress",
    )(k3, v3, pos, w1, w2)


FAR_TILES = 4
PAD_COL = 32
ONES_ROWS = 16


def _nsa_body(qa_ref, qb_ref, kc_ref, vct_ref, ksa_ref, vst_ref, kwa_ref, vwt_ref, gta_ref, gtb_ref, tt_ref,
              bca_ref, bcb_ref, ovt_ref, oa_ref, ob_ref, sbuf_ref, qf_ref):
    p_id = pl.program_id(2)
    n_qb = 2 * pl.num_programs(2)
    nl = Q_PER_KV * LANES
    n_win = WINDOW // LANES + 1
    n_blk = ovt_ref.shape[0]
    max_chunks = (ksa_ref.shape[0] - KV_PAD) // (FAR_TILES * LANES)
    pad_tiles = KV_PAD // LANES
    j_io = lax.broadcasted_iota(jnp.int32, (n_blk, LANES), 0)
    q_io = lax.broadcasted_iota(jnp.int32, (n_blk, LANES), 1)
    col_w = lax.broadcasted_iota(jnp.int32, (nl, HD), 1)
    pad_cols = jnp.where(col_w == PAD_COL, NEG_INF, 0.0).astype(BF16)

    def softmax_av(tiles, vt):
        s_all = jnp.concatenate(tiles, axis=0)
        m = jnp.max(s_all, axis=0, keepdims=True)
        p = jnp.exp2(s_all - m)
        return m, jnp.dot(vt, p.astype(BF16), preferred_element_type=F32)

    def normalise(av):
        return av[0:HD] / av[HD:HD + 1]

    def logit_tiles(k_ref, qq, row0, n):
        s_all = lax.dot_general(k_ref[pl.ds(row0, n * LANES), :], qq, _NT, preferred_element_type=F32)
        return [s_all[t * LANES:(t + 1) * LANES] for t in range(n)]

    def stage_cmp(i, q_ref, bc_ref):
        qt = q_ref[...]
        q = jnp.concatenate([qt[:, r * HD:(r + 1) * HD] for r in range(Q_PER_KV)], axis=0) * SCALE
        s = lax.dot_general(kc_ref[...], q, _NT, preferred_element_type=F32) + bc_ref[...]
        m_c = jnp.max(s, axis=0, keepdims=True)
        p = jnp.exp(s - m_c)
        l_c = jnp.sum(p, axis=0, keepdims=True)
        p_c = p * (jnp.where(m_c > 0.5 * NEG_INF, 1.0, 0.0) / l_c)
        o_c = jnp.dot(vct_ref[...], p_c.astype(BF16), preferred_element_type=F32)
        return dict(i=i, q2=(q.astype(F32) * LOG2E).astype(BF16), p_c=p_c, o_c=o_c)

    def stage_window_logits(st):
        r0 = pl.multiple_of((st["i"] + pad_tiles + 1 - n_win) * LANES, LANES)
        tw = logit_tiles(kwa_ref, jnp.concatenate([st["q2"], pad_cols], axis=1), r0, n_win)
        tw[0] = tw[0] + tt_ref[2]
        tw[n_win - 2] = tw[n_win - 2] + tt_ref[0]
        tw[n_win - 1] = tw[n_win - 1] + tt_ref[1]
        st["tw"] = tw

    def stage_window_softmax(st):
        r0 = pl.multiple_of((st["i"] + pad_tiles + 1 - n_win) * LANES, LANES)
        _, av_w = softmax_av(st.pop("tw"), vwt_ref[:, pl.ds(r0, n_win * LANES)])
        st["o_w"] = normalise(av_w)

    def stage_select(st, slot):
        p_c = st.pop("p_c")
        p_sum = p_c[:, 0:LANES]
        for r in range(1, Q_PER_KV):
            p_sum = p_sum + p_c[:, r * LANES:(r + 1) * LANES]
        imp = jnp.dot(ovt_ref[...], p_sum, precision=lax.Precision.HIGHEST, preferred_element_type=F32)
        t_b = st["i"] * Q_BLOCK + q_io
        cur = t_b >> 6
        forced = (j_io == 0) | (j_io == cur) | (j_io == cur - 1)
        valid = j_io * SEL_BLOCK <= t_b
        score = jnp.where(forced, FORCE, jnp.where(valid, imp, -FORCE))
        rank = jnp.zeros((n_blk, LANES), jnp.int32)
        for ii in range(n_blk):
            row = score[ii:ii + 1, :]
            beats = (row > score) | ((row == score) & (j_io > ii))
            rank = rank + beats.astype(jnp.int32)
        keep = (rank < min(SEL_TOP_N, n_blk)) & (score >= 0.0)

        def augment(kept):
            cols = jnp.concatenate([jnp.where(kept, 0.0, NEG_INF), jnp.full((8, LANES), NEG_INF, F32),
                                    jnp.zeros((LANES - n_blk - 8, LANES), F32)], axis=0)
            cols_t = cols.T[:, 0:HD].astype(BF16)
            return jnp.concatenate([st["q2"], jnp.concatenate([cols_t] * Q_PER_KV, axis=0)], axis=1)

        st["q_near"] = augment(keep)
        qf_ref[slot] = augment(keep & (j_io < 2 * (st["i"] - 1)))

    def stage_near_logits(st):
        r0 = pl.multiple_of((st["i"] + pad_tiles - 1) * LANES, LANES)
        tn = logit_tiles(ksa_ref, st.pop("q_near"), r0, 2)
        tn[0] = tn[0] + tt_ref[0]
        tn[1] = tn[1] + tt_ref[1]
        st["tn"] = tn

    def stage_near_softmax(st):
        r0 = pl.multiple_of((st["i"] + pad_tiles - 1) * LANES, LANES)
        st["m_n"], st["av_n"] = softmax_av(st.pop("tn"), vst_ref[:, pl.ds(r0, 2 * LANES)])

    i_a = p_id
    i_b = n_qb - 1 - p_id
    st_a = stage_cmp(i_a, qa_ref, bca_ref)
    st_b = stage_cmp(i_b, qb_ref, bcb_ref)
    stage_window_logits(st_a)
    stage_window_logits(st_b)
    stage_select(st_a, 0)
    stage_window_softmax(st_a)
    stage_select(st_b, 1)
    stage_window_softmax(st_b)
    stage_near_logits(st_a)
    stage_near_logits(st_b)
    stage_near_softmax(st_a)
    stage_near_softmax(st_b)
    oc_a, ow_a, mn_a, avn_a = st_a["o_c"], st_a["o_w"], st_a["m_n"], st_a["av_n"]
    oc_b, ow_b, mn_b, avn_b = st_b["o_c"], st_b["o_w"], st_b["m_n"], st_b["av_n"]


    n_a = (jnp.maximum(i_a - 1, 0) + FAR_TILES - 1) // FAR_TILES
    is_a, rows, mx = [], [], []
    for k in range(max_chunks):
        own = k < n_a
        c = jnp.where(own, k, k - n_a)
        r = pl.multiple_of((c * FAR_TILES + pad_tiles) * LANES, LANES)
        s_k = lax.dot_general(ksa_ref[pl.ds(r, FAR_TILES * LANES), :], qf_ref[jnp.where(own, 0, 1)], _NT,
                              preferred_element_type=F32)
        sbuf_ref[k] = s_k
        is_a.append(own)
        rows.append(r)
        mx.append(jnp.max(s_k, axis=0, keepdims=True))
    m_a, m_b = mn_a, mn_b
    for k in range(max_chunks):
        m_a = jnp.where(is_a[k], jnp.maximum(m_a, mx[k]), m_a)
        m_b = jnp.where(is_a[k], m_b, jnp.maximum(m_b, mx[k]))
    av_a = jnp.exp2(mn_a - m_a) * avn_a
    av_b = jnp.exp2(mn_b - m_b) * avn_b
    for k in range(max_chunks):
        p = jnp.exp2(sbuf_ref[k] - jnp.where(is_a[k], m_a, m_b))
        pv = jnp.dot(vst_ref[:, pl.ds(rows[k], FAR_TILES * LANES)], p.astype(BF16), preferred_element_type=F32)
        av_a = av_a + jnp.where(is_a[k], pv, 0.0)
        av_b = av_b + jnp.where(is_a[k], 0.0, pv)

    def finish(o_ref, gt_ref, o_c, o_s, o_w):
        def gate(branch):
            rows = gt_ref[pl.ds(branch * N_HEADS + pl.program_id(1) * Q_PER_KV, Q_PER_KV), :]
            return jnp.concatenate([rows[r:r + 1] for r in range(Q_PER_KV)], axis=1)

        o_t = gate(0) * o_c + gate(1) * o_s + gate(2) * o_w
        for pair in range(Q_PER_KV // 2):
            blk = jnp.concatenate([o_t[:, (2 * pair) * LANES:(2 * pair + 1) * LANES],
                                   o_t[:, (2 * pair + 1) * LANES:(2 * pair + 2) * LANES]], axis=0)
            o_ref[:, pair * LANES:(pair + 1) * LANES] = blk.T.astype(o_ref.dtype)

    finish(oa_ref, gta_ref, oc_a, normalise(av_a), ow_a)
    finish(ob_ref, gtb_ref, oc_b, normalise(av_b), ow_b)


def _nsa(proj, kc, vct, ksa, vst, kwa, vwt, gt, tt, bc, ovt, s_):
    b_, g_ = kwa.shape[:2]
    n_qb = s_ // Q_BLOCK
    n_p = n_qb // 2
    m = proj.shape[0]
    nl = Q_PER_KV * LANES
    qcol = SEG_Q * D_MODEL // (Q_PER_KV * HD)
    max_chunks = s_ // (FAR_TILES * LANES)

    def far_chunks(i):
        return (max(i - 1, 0) + FAR_TILES - 1) // FAR_TILES

    assert n_qb % 2 == 0 and all(far_chunks(p) + far_chunks(n_qb - 1 - p) == max_chunks for p in range(n_p))

    def bg(shape):
        return pl.BlockSpec((None, None) + shape, lambda b, g, p: (b, g, 0, 0))

    def blk_a(p):
        return p

    def blk_b(p):
        return n_qb - 1 - p

    def per_block(which):
        return [
            pl.BlockSpec((Q_BLOCK, Q_PER_KV * HD), lambda b, g, p: (b * n_qb + which(p), qcol + g)),
            pl.BlockSpec((None, LANES, Q_BLOCK), lambda b, g, p: (b, 0, which(p))),
            pl.BlockSpec((None, None, LANES, nl), lambda b, g, p: (g, which(p), 0, 0)),
        ]

    qa, gta, bca = per_block(blk_a)
    qb, gtb, bcb = per_block(blk_b)
    half = jax.ShapeDtypeStruct((m // 2, D_MODEL), BF16)
    out_a, out_b = pl.pallas_call(
        _nsa_body,
        grid=(b_, g_, n_p),
        in_specs=[
            qa, qb,
            bg((LANES, HD)), bg((HD, LANES)),
            bg((s_ + KV_PAD, 2 * HD)), bg((HD + ONES_ROWS, s_ + KV_PAD)),
            bg((s_ + KV_PAD, 2 * HD)), bg((HD + ONES_ROWS, s_ + KV_PAD)),
            gta, gtb,
            pl.BlockSpec((None, 3, LANES, nl), lambda b, g, p: (g, 0, 0, 0)),
            bca, bcb,
            pl.BlockSpec((s_ // SEL_BLOCK, LANES), lambda b, g, p: (0, 0)),
        ],
        out_specs=[
            pl.BlockSpec((Q_BLOCK, Q_PER_KV * HD), lambda b, g, p: (b * n_p + p, g)),
            pl.BlockSpec((Q_BLOCK, Q_PER_KV * HD), lambda b, g, p: (b * n_p + n_p - 1 - p, g)),
        ],
        out_shape=[half, half],
        scratch_shapes=[pltpu.VMEM((max_chunks, FAR_TILES * LANES, nl), F32), pltpu.VMEM((2, nl, 2 * HD), BF16)],
        compiler_params=_params(("parallel", "parallel", "arbitrary")),
        name="nsa",
    )(proj, proj, kc, vct, ksa, vst, kwa, vwt, gt, gt, tt, bc, bc, ovt)
    return jnp.concatenate([out_a.reshape(b_, s_ // 2, D_MODEL), out_b.reshape(b_, s_ // 2, D_MODEL)],
                           axis=1).reshape(m, D_MODEL)


def _mix_body(gb_ref, gc_ref, va_ref, gch_ref, vah_ref, uc_ref, vc_ref, g0_ref, g1_ref, g2_ref, ob_ref, x_ref,
              ca_ref, sw_ref, sb_ref, lng_ref, lnb_ref, wa_ref, wb_ref, wc_ref, wo_ref, o_ref, xa_ref, mx_ref):
    tm = gb_ref.shape[0]
    first = pl.program_id(1) == 0

    halo = gch_ref[...].astype(F32) * vah_ref[...].astype(F32)
    xa_ref[0:HALO, :] = jnp.where(first, 0.0, halo)
    xa_ref[HALO:, :] = gc_ref[...].astype(F32) * va_ref[...].astype(F32)
    ca = ca_ref[...]
    row = lax.broadcasted_iota(jnp.int32, (CHUNK, CHUNK), 0)
    col = lax.broadcasted_iota(jnp.int32, (CHUNK, CHUNK), 1)
    sw = [jnp.where(row >= col, sw_ref[gi], 0.0).astype(BF16) for gi in range(SGU_GROUPS)]

    for r in range(0, tm, ROW_CHUNK):
        rows = slice(r, r + ROW_CHUNK)
        conv = (ca[0:1] * xa_ref[pl.ds(r + HALO - 2, ROW_CHUNK), :]
                + ca[1:2] * xa_ref[pl.ds(r + HALO - 1, ROW_CHUNK), :]
                + ca[2:3] * xa_ref[pl.ds(r + HALO, ROW_CHUNK), :])
        out_a = gb_ref[rows, :].astype(F32) * conv
        y_a = jnp.dot(out_a.astype(BF16), wa_ref[...], preferred_element_type=F32)

        v = vc_ref[rows, :].astype(F32)
        mu = jnp.mean(v, axis=-1, keepdims=True)
        vz = v - mu
        var = jnp.mean(vz * vz, axis=-1, keepdims=True)
        vn = (vz * lax.rsqrt(var + EPS) * lng_ref[...] + lnb_ref[...]).astype(BF16)
        for gi in range(SGU_GROUPS):
            cols = slice(gi * LANES, (gi + 1) * LANES)
            for c in range(0, ROW_CHUNK, CHUNK):
                mx_ref[r + c:r + c + CHUNK, cols] = (
                    jnp.dot(sw[gi], vn[c:c + CHUNK, cols], preferred_element_type=F32) + sb_ref[:, cols])
        out_c = uc_ref[rows, :].astype(F32) * mx_ref[rows, :]
        y_c = jnp.dot(out_c.astype(BF16), wc_ref[...], preferred_element_type=F32)

        y_b = jnp.dot(ob_ref[rows, :], wb_ref[...], preferred_element_type=F32)

        merged = (g0_ref[rows, :].astype(F32) * y_a + g1_ref[rows, :].astype(F32) * y_b
                  + g2_ref[rows, :].astype(F32) * y_c)
        o_ref[rows, :] = x_ref[rows, :] + jnp.dot(merged.astype(BF16), wo_ref[...], preferred_element_type=F32)


def _mix(proj, out_b, x2, seq, conv_a, sgu_w, sgu_b2, ln_g, ln_b, w_a, w_b, w_c, w_o, tm=512):
    m = x2.shape[0]
    n_t = seq // tm
    hb = tm // HALO

    def seg(k):
        return pl.BlockSpec((tm, D_MODEL), lambda b, i: (b * n_t + i, k))

    def halo(k):
        return pl.BlockSpec((HALO, D_MODEL), lambda b, i: (jnp.maximum((b * n_t + i) * hb - 1, 0), k))

    def full(shape):
        return pl.BlockSpec(shape, lambda b, i: (0,) * len(shape))

    row = pl.BlockSpec((tm, D_MODEL), lambda b, i: (b * n_t + i, 0))
    wspec = pl.BlockSpec((D_MODEL, D_MODEL), lambda b, i: (0, 0), pipeline_mode=pl.Buffered(1))
    return pl.pallas_call(
        _mix_body,
        grid=(m // seq, n_t),
        in_specs=[
            seg(SEG_GB), seg(SEG_GC), seg(SEG_VA), halo(SEG_GC), halo(SEG_VA), seg(SEG_U), seg(SEG_V),
            seg(SEG_G0), seg(SEG_G0 + 1), seg(SEG_G0 + 2), row, row,
            full((CONV_W, D_MODEL)), full((SGU_GROUPS, CHUNK, CHUNK)), full((CHUNK, D_MODEL)),
            full((1, D_MODEL)), full((1, D_MODEL)), wspec, wspec, wspec, wspec,
        ],
        out_specs=row,
        out_shape=jax.ShapeDtypeStruct((m, D_MODEL), F32),
        scratch_shapes=[pltpu.VMEM((tm + HALO, D_MODEL), F32), pltpu.VMEM((tm, D_MODEL), F32)],
        compiler_params=_params(("parallel", "arbitrary")),
        name="mix",
    )(proj, proj, proj, proj, proj, proj, proj, proj, proj, proj, out_b, x2,
      conv_a, sgu_w, sgu_b2, ln_g, ln_b, w_a, w_b, w_c, w_o)


def _ffn_body(x_ref, xh_ref, g_ref, wg_ref, wv_ref, cw_ref, wd_ref, o_ref, h_ref, gt_ref, acc_ref, *, tiles_per_seq):
    i = pl.program_id(0)
    j = pl.program_id(1)
    tm = x_ref.shape[0]

    @pl.when(j == 0)
    def _():
        h_ref[0:HALO, :] = _rms(xh_ref[...], g_ref[...]).astype(BF16)
        h_ref[HALO:, :] = _rms(x_ref[...], g_ref[...]).astype(BF16)
        acc_ref[...] = jnp.zeros_like(acc_ref)

    gate = jnp.dot(h_ref[...], wg_ref[...], preferred_element_type=F32)
    rows = lax.broadcasted_iota(jnp.int32, gate.shape, 0)
    seq_start = (i % tiles_per_seq) == 0
    gt_ref[...] = jnp.where(seq_start & (rows < HALO), 0.0, gate)
    cw = cw_ref[...]
    conv = (cw[0:1] * gt_ref[pl.ds(HALO - 2, tm), :] + cw[1:2] * gt_ref[pl.ds(HALO - 1, tm), :]
            + cw[2:3] * gt_ref[pl.ds(HALO, tm), :])
    val = jnp.dot(h_ref[HALO:, :], wv_ref[...], preferred_element_type=F32)
    act = (jax.nn.gelu(conv) * val).astype(BF16)
    acc_ref[...] += jnp.dot(act, wd_ref[...], preferred_element_type=F32)

    @pl.when(j == pl.num_programs(1) - 1)
    def _():
        o_ref[...] = x_ref[...] + acc_ref[...]


def _ffn(x2, seq, g, w_up, conv_w, w_down, tm=1024, tf=512):
    m = x2.shape[0]
    n_f = D_FF // tf
    hb = tm // HALO
    return pl.pallas_call(
        functools.partial(_ffn_body, tiles_per_seq=seq // tm),
        grid=(m // tm, n_f),
        in_specs=[
            pl.BlockSpec((tm, D_MODEL), lambda i, j: (i, 0)),
            pl.BlockSpec((HALO, D_MODEL), lambda i, j: (jnp.maximum(i * hb - 1, 0), 0)),
            pl.BlockSpec((1, D_MODEL), lambda i, j: (0, 0)),
            pl.BlockSpec((D_MODEL, tf), lambda i, j: (0, j)),
            pl.BlockSpec((D_MODEL, tf), lambda i, j: (0, n_f + j)),
            pl.BlockSpec((CONV_W, tf), lambda i, j: (0, j)),
            pl.BlockSpec((tf, D_MODEL), lambda i, j: (j, 0)),
        ],
        out_specs=pl.BlockSpec((tm, D_MODEL), lambda i, j: (i, 0)),
        out_shape=jax.ShapeDtypeStruct((m, D_MODEL), F32),
        scratch_shapes=[
            pltpu.VMEM((tm + HALO, D_MODEL), BF16),
            pltpu.VMEM((tm + HALO, tf), F32),
            pltpu.VMEM((tm, D_MODEL), F32),
        ],
        compiler_params=_params(("parallel", "arbitrary")),
        name="ffn",
    )(x2, x2, g, w_up, w_up, conv_w, w_down)


def _norm_body(x_ref, g_ref, o_ref):
    o_ref[...] = _rms(x_ref[...], g_ref[...])


def _final_norm(x2, g, tm=1024):
    m = x2.shape[0]
    return pl.pallas_call(
        _norm_body,
        grid=(m // tm,),
        in_specs=[pl.BlockSpec((tm, D_MODEL), lambda i: (i, 0)), pl.BlockSpec((1, D_MODEL), lambda i: (0, 0))],
        out_specs=pl.BlockSpec((tm, D_MODEL), lambda i: (i, 0)),
        out_shape=jax.ShapeDtypeStruct((m, D_MODEL), F32),
        compiler_params=_params(("parallel",)),
        name="final_norm",
    )(x2, g)


def _split_w_in(w_in):
    c = [0]
    for sz in [D_MODEL] * 4 + [N_KV * HD] * 6 + [3 * N_HEADS] + [D_MODEL] * 5:
        c.append(c[-1] + sz)
    main = jnp.concatenate([w_in[:, c[0]:c[4]], w_in[:, c[11]:c[16]]], axis=1)
    kv = w_in[:, c[4]:c[10]].reshape(D_MODEL, 6, N_KV, HD)[:, jnp.array([2, 3, 4, 5, 0, 1])]
    kv = kv.transpose(0, 2, 1, 3).reshape(D_MODEL, N_KVCOL)
    gate = jnp.pad(w_in[:, c[10]:c[11]], ((0, 0), (0, LANES - 3 * N_HEADS)))
    return main.astype(BF16), kv.astype(BF16), gate.astype(BF16)


def _mixer_layer(x2, b_, s_, tt, bc, ovt, norm_g, w_in, conv_a, cmp_pos, cmp_w1, cmp_w2, sgu_w, sgu_b,
                 sgu_norm_g, sgu_norm_b, w_br_a, w_br_b, w_br_c, w_o):
    w_main, w_kv, w_gate = _split_w_in(w_in)
    norm_g = norm_g.reshape(1, D_MODEL)
    proj = _in_proj(x2, norm_g, w_main)
    kc, vc, ksa, vst, kwa, vwt, gt = _kv_proj(x2, b_, s_, norm_g, w_kv, w_gate)

    def rows16(a):
        return a.reshape(b_, N_KV, s_ // CMP_STRIDE, CMP_STRIDE * HD)

    k_cmp, v_cmp = _compress(rows16(kc), rows16(vc), cmp_pos.reshape(2, 1, CMP_LEN * HD),
                             cmp_w1.astype(BF16), cmp_w2.astype(BF16))
    out_b = _nsa(proj, k_cmp, v_cmp.transpose(0, 1, 3, 2), ksa, vst, kwa, vwt, gt, tt, bc, ovt, s_)

    sgu_b2 = jnp.broadcast_to(sgu_b.T[:, :, None], (CHUNK, SGU_GROUPS, D_MODEL // SGU_GROUPS)).reshape(CHUNK, D_MODEL)
    return _mix(proj, out_b, x2, s_, conv_a, sgu_w, sgu_b2,
                sgu_norm_g.reshape(1, D_MODEL), sgu_norm_b.reshape(1, D_MODEL),
                w_br_a.astype(BF16), w_br_b.astype(BF16), w_br_c.astype(BF16), w_o.astype(BF16))


def _overlap_t(s_):
    n_blk = s_ // SEL_BLOCK
    cmp_start = jnp.arange(LANES) * CMP_STRIDE
    cmp_end = cmp_start + CMP_LEN - 1
    blk_start = jnp.arange(n_blk) * SEL_BLOCK
    n_cmp = (s_ - CMP_LEN) // CMP_STRIDE + 1
    ov = ((cmp_start[None, :] < blk_start[:, None] + SEL_BLOCK) & (cmp_end[None, :] >= blk_start[:, None])
          & (jnp.arange(LANES)[None, :] < n_cmp))
    return ov.astype(F32)


def kernel(x, rel_bias, norm_mix, w_in, conv_a, cmp_pos, cmp_w1, cmp_w2, sgu_w, sgu_b, sgu_norm_g, sgu_norm_b,
           w_br_a, w_br_b, w_br_c, w_o, norm_ffn, ffn_w_up, ffn_conv, ffn_w_down, norm_final):
    b_, s_, _ = x.shape
    depth = w_in.shape[0]
    x2 = x.reshape(b_ * s_, D_MODEL)
    tt, bc = _bias_tiles(rel_bias, s_ // Q_BLOCK)
    ovt = _overlap_t(s_)
    for l in range(depth):
        x2 = _mixer_layer(x2, b_, s_, tt, bc, ovt, norm_mix[l], w_in[l], conv_a[l], cmp_pos[l], cmp_w1[l],
                          cmp_w2[l], sgu_w[l], sgu_b[l], sgu_norm_g[l], sgu_norm_b[l],
                          w_br_a[l], w_br_b[l], w_br_c[l], w_o[l])
        x2 = _ffn(x2, s_, norm_ffn[l].reshape(1, D_MODEL), ffn_w_up[l].astype(BF16), ffn_conv[l],
                  ffn_w_down[l].astype(BF16))
    return _final_norm(x2, norm_final.reshape(1, D_MODEL)).reshape(b_, s_, D_MODEL)
```

```python
import functools
import math

import jax
import jax.numpy as jnp
from jax import lax
from jax.experimental import pallas as pl
from jax.experimental.pallas import tpu as pltpu

D_MODEL = 1024
HD = 64
N_HEADS = D_MODEL // HD
N_KV = 4
Q_PER_KV = N_HEADS // N_KV
SGU_GROUPS = 8
CHUNK = 128
CONV_W = 3
CMP_LEN = 32
CMP_STRIDE = 16
CMP_HID = 2 * HD
SEL_BLOCK = 64
SEL_TOP_N = 8
WINDOW = 512
Q_BLOCK = 128
D_FF = 3 * D_MODEL
N_BUCKETS = 32
MAX_EXACT = N_BUCKETS // 2
REL_MAX_DIST = 128
SCALE = HD ** -0.5
EPS = 1e-6
NEG_INF = -1e30
FORCE = 1e9
LOG2E = 1.4426950408889634

LANES = 128
ROW_CHUNK = 256
HALO = 16
N_SEG = 9
SEG_GB, SEG_GC, SEG_VA, SEG_Q, SEG_U, SEG_V, SEG_G0 = 0, 1, 2, 3, 4, 5, 6
N_MAIN = N_SEG * D_MODEL
N_KVCOL = 6 * N_KV * HD
KV_PAD = WINDOW
VMEM_LIMIT = 56 * 1024 * 1024

BF16 = jnp.bfloat16
F32 = jnp.float32
_NT = (((1,), (1,)), ((), ()))


def _params(sem):
    return pltpu.CompilerParams(dimension_semantics=sem, vmem_limit_bytes=VMEM_LIMIT)


def _rms(x, g):
    return x * lax.rsqrt(jnp.mean(x * x, axis=-1, keepdims=True) + EPS) * g


def _in_proj_body(x_ref, g_ref, w_ref, o_ref, h_ref, *, gelu_tiles, sigmoid_tiles):
    j = pl.program_id(1)

    @pl.when(j == 0)
    def _():
        h_ref[...] = _rms(x_ref[...], g_ref[...]).astype(BF16)

    def tile(act):
        for r in range(0, h_ref.shape[0], ROW_CHUNK):
            y = jnp.dot(h_ref[r:r + ROW_CHUNK, :], w_ref[...], preferred_element_type=F32)
            o_ref[r:r + ROW_CHUNK, :] = act(y).astype(o_ref.dtype)

    is_gelu = (j >= gelu_tiles[0]) & (j < gelu_tiles[1])
    is_sigmoid = (j >= sigmoid_tiles[0]) & (j < sigmoid_tiles[1])
    pl.when(is_gelu)(lambda: tile(jax.nn.gelu))
    pl.when(is_sigmoid)(lambda: tile(jax.nn.sigmoid))
    pl.when(jnp.logical_not(is_gelu | is_sigmoid))(lambda: tile(lambda y: y))


def _in_proj(x2, g, w_main, tm=1024):
    m = x2.shape[0]
    tn = D_MODEL
    body = functools.partial(_in_proj_body, gelu_tiles=(SEG_U, SEG_V + 1), sigmoid_tiles=(SEG_G0, SEG_G0 + 3))
    return pl.pallas_call(
        body,
        grid=(m // tm, N_SEG),
        in_specs=[
            pl.BlockSpec((tm, D_MODEL), lambda i, j: (i, 0)),
            pl.BlockSpec((1, D_MODEL), lambda i, j: (0, 0)),
            pl.BlockSpec((None, D_MODEL, tn), lambda i, j: (j, 0, 0)),
        ],
        out_specs=pl.BlockSpec((tm, tn), lambda i, j: (i, j)),
        out_shape=jax.ShapeDtypeStruct((m, N_MAIN), BF16),
        scratch_shapes=[pltpu.VMEM((tm, D_MODEL), BF16)],
        compiler_params=_params(("parallel", "arbitrary")),
        name="in_proj",
    )(x2, g, w_main)


def _kv_proj_body(x_ref, g_ref, w_ref, wg_ref, kc_ref, vc_ref, ksa_ref, vst_ref, kwa_ref, vwt_ref, gt_ref, cmp_ref):
    s_id = pl.program_id(1)
    tm = x_ref.shape[0]
    lane = lax.broadcasted_iota(jnp.int32, (tm, LANES), 1)
    ones_rows = (lax.broadcasted_iota(jnp.int32, (ONES_ROWS, tm), 0) == 0).astype(BF16)

    @pl.when(s_id == 0)
    def _():
        pad_keys = jnp.where(lane == HD + PAD_COL, 1.0, 0.0).astype(BF16)
        for g in range(N_KV):
            ksa_ref[g] = pad_keys
            kwa_ref[g] = pad_keys
            vst_ref[g] = jnp.zeros(vst_ref.shape[1:], BF16)
            vwt_ref[g] = jnp.zeros(vwt_ref.shape[1:], BF16)

    @pl.when(s_id > 0)
    def _():
        h = _rms(x_ref[...], g_ref[...]).astype(BF16)
        y = jnp.dot(h, w_ref[...], preferred_element_type=F32)
        tok = (s_id - 1) * tm + lax.broadcasted_iota(jnp.int32, (tm, LANES), 0)
        blk_id = jnp.where(lane - HD == tok // SEL_BLOCK, 1.0, 0.0)
        for g in range(N_KV):
            base = g * 6 * HD
            sel = y[:, base:base + 2 * HD]
            win = y[:, base + 2 * HD:base + 4 * HD]
            cmp = y[:, base + 4 * HD:base + 6 * HD]
            ksa_ref[g] = jnp.where(lane < HD, sel, blk_id).astype(BF16)
            kwa_ref[g] = jnp.where(lane < HD, win, 0.0).astype(BF16)
            vst_ref[g, 0:HD, :] = sel.T[HD:].astype(BF16)
            vst_ref[g, HD:, :] = ones_rows
            vwt_ref[g, 0:HD, :] = win.T[HD:].astype(BF16)
            vwt_ref[g, HD:, :] = ones_rows
            cmp_ref[...] = cmp
            for c in range(CMP_STRIDE):
                every = cmp_ref[pl.ds(c, tm // CMP_STRIDE, stride=CMP_STRIDE), :].astype(BF16)
                kc_ref[g, :, c * HD:(c + 1) * HD] = every[:, 0:HD]
                vc_ref[g, :, c * HD:(c + 1) * HD] = every[:, HD:]
        gt_ref[...] = jax.nn.sigmoid(jnp.dot(h, wg_ref[...], preferred_element_type=F32)).T


def _kv_proj(x2, b_, s_, g, w_kv, w_gate, tm=512):
    n_t = s_ // tm
    assert KV_PAD == tm

    def tok(b, s):
        return jnp.maximum(s - 1, 0)

    return pl.pallas_call(
        _kv_proj_body,
        grid=(b_, n_t + 1),
        in_specs=[
            pl.BlockSpec((tm, D_MODEL), lambda b, s: (b * n_t + tok(b, s), 0)),
            pl.BlockSpec((1, D_MODEL), lambda b, s: (0, 0)),
            pl.BlockSpec((D_MODEL, N_KVCOL), lambda b, s: (0, 0), pipeline_mode=pl.Buffered(1)),
            pl.BlockSpec((D_MODEL, LANES), lambda b, s: (0, 0), pipeline_mode=pl.Buffered(1)),
        ],
        out_specs=[
            pl.BlockSpec((None, N_KV, tm // CMP_STRIDE, CMP_STRIDE * HD), lambda b, s: (b, 0, tok(b, s), 0)),
            pl.BlockSpec((None, N_KV, tm // CMP_STRIDE, CMP_STRIDE * HD), lambda b, s: (b, 0, tok(b, s), 0)),
            pl.BlockSpec((None, N_KV, tm, 2 * HD), lambda b, s: (b, 0, s, 0)),
            pl.BlockSpec((None, N_KV, HD + ONES_ROWS, tm), lambda b, s: (b, 0, 0, s)),
            pl.BlockSpec((None, N_KV, tm, 2 * HD), lambda b, s: (b, 0, s, 0)),
            pl.BlockSpec((None, N_KV, HD + ONES_ROWS, tm), lambda b, s: (b, 0, 0, s)),
            pl.BlockSpec((None, LANES, tm), lambda b, s: (b, 0, tok(b, s))),
        ],
        out_shape=[
            jax.ShapeDtypeStruct((b_, N_KV, s_ // CMP_STRIDE, CMP_STRIDE * HD), BF16),
            jax.ShapeDtypeStruct((b_, N_KV, s_ // CMP_STRIDE, CMP_STRIDE * HD), BF16),
            jax.ShapeDtypeStruct((b_, N_KV, KV_PAD + s_, 2 * HD), BF16),
            jax.ShapeDtypeStruct((b_, N_KV, HD + ONES_ROWS, KV_PAD + s_), BF16),
            jax.ShapeDtypeStruct((b_, N_KV, KV_PAD + s_, 2 * HD), BF16),
            jax.ShapeDtypeStruct((b_, N_KV, HD + ONES_ROWS, KV_PAD + s_), BF16),
            jax.ShapeDtypeStruct((b_, LANES, s_), F32),
        ],
        scratch_shapes=[pltpu.VMEM((tm, 2 * HD), F32)],
        compiler_params=_params(("parallel", "arbitrary")),
        name="kv_proj",
    )(x2, g, w_kv, w_gate)


def _rel_bucket(dist):
    dist = jnp.maximum(dist, 0)
    log_ratio = jnp.log(jnp.maximum(dist, 1).astype(F32) / MAX_EXACT) / math.log(REL_MAX_DIST / MAX_EXACT)
    large = MAX_EXACT + (log_ratio * (N_BUCKETS - MAX_EXACT)).astype(jnp.int32)
    return jnp.where(dist < MAX_EXACT, dist, jnp.minimum(large, N_BUCKETS - 1))


def _bias_body(tab_ref, bt_ref, bc_ref, tt_ref, bco_ref, *, n_cmp):
    h = pl.program_id(0)

    def lookup(bk):
        acc = jnp.zeros(bk.shape, F32)
        for b in range(N_BUCKETS):
            acc = jnp.where(bk == b, tab_ref[b, h], acc)
        return acc

    kk = lax.broadcasted_iota(jnp.int32, (LANES, LANES), 0)
    qq = lax.broadcasted_iota(jnp.int32, (LANES, LANES), 1)
    far = lookup(bt_ref[2])
    tt_ref[0] = (lookup(bt_ref[1]) - far) * LOG2E
    tt_ref[1] = jnp.where(qq >= kk, (lookup(bt_ref[0]) - far) * LOG2E, NEG_INF)
    tt_ref[2] = jnp.where(kk > qq, 0.0, NEG_INF)
    for i in range(bc_ref.shape[0]):
        ok = (i * Q_BLOCK + qq - (kk * CMP_STRIDE + CMP_LEN - 1) >= 0) & (kk < n_cmp)
        bco_ref[i] = jnp.where(ok, lookup(bc_ref[i]), NEG_INF)


def _bias_tiles(rel_bias, n_qb):
    kk = jnp.arange(LANES)[:, None]
    qq = jnp.arange(LANES)[None, :]
    bt = jnp.stack([_rel_bucket(LANES * d + qq - kk) for d in range(3)]).astype(jnp.int32)
    cmp_end = CMP_STRIDE * jnp.arange(LANES) + CMP_LEN - 1
    t = (Q_BLOCK * jnp.arange(n_qb))[:, None, None] + qq[None]
    bc = _rel_bucket(t - cmp_end[None, :, None]).astype(jnp.int32)
    n_cmp = (n_qb * Q_BLOCK - CMP_LEN) // CMP_STRIDE + 1
    return pl.pallas_call(
        functools.partial(_bias_body, n_cmp=n_cmp),
        grid=(N_HEADS,),
        in_specs=[
            pl.BlockSpec(memory_space=pltpu.SMEM),
            pl.BlockSpec((3, LANES, LANES), lambda h: (0, 0, 0)),
            pl.BlockSpec((n_qb, LANES, LANES), lambda h: (0, 0, 0)),
        ],
        out_specs=[
            pl.BlockSpec((None, 3, LANES, LANES), lambda h: (h // Q_PER_KV, 0, 0, h % Q_PER_KV)),
            pl.BlockSpec((None, n_qb, LANES, LANES), lambda h: (h // Q_PER_KV, 0, 0, h % Q_PER_KV)),
        ],
        out_shape=[
            jax.ShapeDtypeStruct((N_KV, 3, LANES, Q_PER_KV * LANES), F32),
            jax.ShapeDtypeStruct((N_KV, n_qb, LANES, Q_PER_KV * LANES), F32),
        ],
        compiler_params=_params(("arbitrary",)),
        name="bias_tiles",
    )(rel_bias.astype(F32), bt, bc)


def _compress_body(k3_ref, v3_ref, pos_ref, w1_ref, w2_ref, ko_ref, vo_ref):
    half = CMP_STRIDE * HD
    for which, (src, dst) in enumerate(((k3_ref, ko_ref), (v3_ref, vo_ref))):
        x = src[...].astype(F32)
        pos = pos_ref[which]
        xa = (x + pos[:, :half]).astype(BF16)
        xb = (x + pos[:, half:]).astype(BF16)
        a = jnp.dot(xa, w1_ref[which, :half, :], preferred_element_type=F32)
        b = jnp.dot(xb, w1_ref[which, half:, :], preferred_element_type=F32)
        hid = jax.nn.gelu(a + pltpu.roll(b, LANES - 1, 0))
        dst[...] = jnp.dot(hid.astype(BF16), w2_ref[which], preferred_element_type=F32).astype(dst.dtype)


def _compress(k3, v3, pos, w1, w2):
    b_, g_, nm, width = k3.shape
    spec3 = pl.BlockSpec((None, None, nm, width), lambda b, g: (b, g, 0, 0))
    ospec = pl.BlockSpec((None, None, nm, HD), lambda b, g: (b, g, 0, 0))
    return pl.pallas_call(
        _compress_body,
        grid=(b_, g_),
        in_specs=[
            spec3, spec3,
            pl.BlockSpec((2, 1, CMP_LEN * HD), lambda b, g: (0, 0, 0)),
            pl.BlockSpec((2, CMP_LEN * HD, CMP_HID), lambda b, g: (0, 0, 0)),
            pl.BlockSpec((2, CMP_HID, HD), lambda b, g: (0, 0, 0)),
        ],
        out_specs=[ospec, ospec],
        out_shape=[jax.ShapeDtypeStruct((b_, g_, nm, HD), BF16)] * 2,
        compiler_params=_params(("parallel", "parallel")),
        name="compress",
    )(k3, v3, pos, w1, w2)


FAR_TILES = 4
PAD_COL = 32
ONES_ROWS = 16


def _nsa_body(qa_ref, qb_ref, kc_ref, vct_ref, ksa_ref, vst_ref, kwa_ref, vwt_ref, gta_ref, gtb_ref, tt_ref,
              bca_ref, bcb_ref, ovt_ref, o_ref, sbuf_ref, qf_ref):
    p_id = pl.program_id(2)
    n_qb = 2 * pl.num_programs(2)
    nl = Q_PER_KV * LANES
    n_win = WINDOW // LANES + 1
    n_blk = ovt_ref.shape[0]
    max_chunks = (ksa_ref.shape[0] - KV_PAD) // (FAR_TILES * LANES)
    pad_tiles = KV_PAD // LANES
    j_io = lax.broadcasted_iota(jnp.int32, (n_blk, LANES), 0)
    q_io = lax.broadcasted_iota(jnp.int32, (n_blk, LANES), 1)
    col_w = lax.broadcasted_iota(jnp.int32, (nl, HD), 1)
    pad_cols = jnp.where(col_w == PAD_COL, NEG_INF, 0.0).astype(BF16)

    def softmax_av(tiles, vt):
        s_all = jnp.concatenate(tiles, axis=0)
        m = jnp.max(s_all, axis=0, keepdims=True)
        p = jnp.exp2(s_all - m)
        return m, jnp.dot(vt, p.astype(BF16), preferred_element_type=F32)

    def normalise(av):
        return av[0:HD] / av[HD:HD + 1]

    def logit_tiles(k_ref, qq, row0, n):
        s_all = lax.dot_general(k_ref[pl.ds(row0, n * LANES), :], qq, _NT, preferred_element_type=F32)
        return [s_all[t * LANES:(t + 1) * LANES] for t in range(n)]

    def stage_cmp(i, q_ref, bc_ref):
        qt = q_ref[...]
        q = jnp.concatenate([qt[:, r * HD:(r + 1) * HD] for r in range(Q_PER_KV)], axis=0) * SCALE
        s = lax.dot_general(kc_ref[...], q, _NT, preferred_element_type=F32) + bc_ref[...]
        m_c = jnp.max(s, axis=0, keepdims=True)
        p = jnp.exp(s - m_c)
        l_c = jnp.sum(p, axis=0, keepdims=True)
        p_c = p * (jnp.where(m_c > 0.5 * NEG_INF, 1.0, 0.0) / l_c)
        o_c = jnp.dot(vct_ref[...], p_c.astype(BF16), preferred_element_type=F32)
        return dict(i=i, q2=(q.astype(F32) * LOG2E).astype(BF16), p_c=p_c, o_c=o_c)

    def stage_window_logits(st):
        r0 = pl.multiple_of((st["i"] + pad_tiles + 1 - n_win) * LANES, LANES)
        tw = logit_tiles(kwa_ref, jnp.concatenate([st["q2"], pad_cols], axis=1), r0, n_win)
        tw[0] = tw[0] + tt_ref[2]
        tw[n_win - 2] = tw[n_win - 2] + tt_ref[0]
        tw[n_win - 1] = tw[n_win - 1] + tt_ref[1]
        st["tw"] = tw

    def stage_window_softmax(st):
        r0 = pl.multiple_of((st["i"] + pad_tiles + 1 - n_win) * LANES, LANES)
        _, av_w = softmax_av(st.pop("tw"), vwt_ref[:, pl.ds(r0, n_win * LANES)])
        st["o_w"] = normalise(av_w)

    def stage_select(st, slot):
        p_c = st.pop("p_c")
        p_sum = p_c[:, 0:LANES]
        for r in range(1, Q_PER_KV):
            p_sum = p_sum + p_c[:, r * LANES:(r + 1) * LANES]
        imp = jnp.dot(ovt_ref[...], p_sum, precision=lax.Precision.HIGHEST, preferred_element_type=F32)
        t_b = st["i"] * Q_BLOCK + q_io
        cur = t_b >> 6
        forced = (j_io == 0) | (j_io == cur) | (j_io == cur - 1)
        valid = j_io * SEL_BLOCK <= t_b
        score = jnp.where(forced, FORCE, jnp.where(valid, imp, -FORCE))
        rank = jnp.zeros((n_blk, LANES), jnp.int32)
        for ii in range(n_blk):
            row = score[ii:ii + 1, :]
            beats = (row > score) | ((row == score) & (j_io > ii))
            rank = rank + beats.astype(jnp.int32)
        keep = (rank < min(SEL_TOP_N, n_blk)) & (score >= 0.0)

        def augment(kept):
            cols = jnp.concatenate([jnp.where(kept, 0.0, NEG_INF), jnp.full((8, LANES), NEG_INF, F32),
                                    jnp.zeros((LANES - n_blk - 8, LANES), F32)], axis=0)
            cols_t = cols.T[:, 0:HD].astype(BF16)
            return jnp.concatenate([st["q2"], jnp.concatenate([cols_t] * Q_PER_KV, axis=0)], axis=1)

        st["q_near"] = augment(keep)
        qf_ref[slot] = augment(keep & (j_io < 2 * (st["i"] - 1)))

    def stage_near_logits(st):
        r0 = pl.multiple_of((st["i"] + pad_tiles - 1) * LANES, LANES)
        tn = logit_tiles(ksa_ref, st.pop("q_near"), r0, 2)
        tn[0] = tn[0] + tt_ref[0]
        tn[1] = tn[1] + tt_ref[1]
        st["tn"] = tn

    def stage_near_softmax(st):
        r0 = pl.multiple_of((st["i"] + pad_tiles - 1) * LANES, LANES)
        st["m_n"], st["av_n"] = softmax_av(st.pop("tn"), vst_ref[:, pl.ds(r0, 2 * LANES)])

    i_a = p_id
    i_b = n_qb - 1 - p_id
    st_a = stage_cmp(i_a, qa_ref, bca_ref)
    st_b = stage_cmp(i_b, qb_ref, bcb_ref)
    stage_window_logits(st_a)
    stage_window_logits(st_b)
    stage_select(st_a, 0)
    stage_window_softmax(st_a)
    stage_select(st_b, 1)
    stage_window_softmax(st_b)
    stage_near_logits(st_a)
    stage_near_logits(st_b)
    stage_near_softmax(st_a)
    stage_near_softmax(st_b)
    oc_a, ow_a, mn_a, avn_a = st_a["o_c"], st_a["o_w"], st_a["m_n"], st_a["av_n"]
    oc_b, ow_b, mn_b, avn_b = st_b["o_c"], st_b["o_w"], st_b["m_n"], st_b["av_n"]


    n_a = (jnp.maximum(i_a - 1, 0) + FAR_TILES - 1) // FAR_TILES
    is_a, rows, mx = [], [], []
    for k in range(max_chunks):
        own = k < n_a
        c = jnp.where(own, k, k - n_a)
        r = pl.multiple_of((c * FAR_TILES + pad_tiles) * LANES, LANES)
        s_k = lax.dot_general(ksa_ref[pl.ds(r, FAR_TILES * LANES), :], qf_ref[jnp.where(own, 0, 1)], _NT,
                              preferred_element_type=F32)
        sbuf_ref[k] = s_k
        is_a.append(own)
        rows.append(r)
        mx.append(jnp.max(s_k, axis=0, keepdims=True))
    m_a, m_b = mn_a, mn_b
    for k in range(max_chunks):
        m_a = jnp.where(is_a[k], jnp.maximum(m_a, mx[k]), m_a)
        m_b = jnp.where(is_a[k], m_b, jnp.maximum(m_b, mx[k]))
    av_a = jnp.exp2(mn_a - m_a) * avn_a
    av_b = jnp.exp2(mn_b - m_b) * avn_b
    for k in range(max_chunks):
        p = jnp.exp2(sbuf_ref[k] - jnp.where(is_a[k], m_a, m_b))
        pv = jnp.dot(vst_ref[:, pl.ds(rows[k], FAR_TILES * LANES)], p.astype(BF16), preferred_element_type=F32)
        av_a = av_a + jnp.where(is_a[k], pv, 0.0)
        av_b = av_b + jnp.where(is_a[k], 0.0, pv)

    def finish(half, gt_ref, o_c, o_s, o_w):
        def gate(branch):
            rows = gt_ref[pl.ds(branch * N_HEADS + pl.program_id(1) * Q_PER_KV, Q_PER_KV), :]
            return jnp.concatenate([rows[r:r + 1] for r in range(Q_PER_KV)], axis=1)

        o_t = gate(0) * o_c + gate(1) * o_s + gate(2) * o_w
        for pair in range(Q_PER_KV // 2):
            blk = jnp.concatenate([o_t[:, (2 * pair) * LANES:(2 * pair + 1) * LANES],
                                   o_t[:, (2 * pair + 1) * LANES:(2 * pair + 2) * LANES]], axis=0)
            o_ref[half, :, pair * LANES:(pair + 1) * LANES] = blk.T.astype(o_ref.dtype)

    finish(0, gta_ref, oc_a, normalise(av_a), ow_a)
    finish(1, gtb_ref, oc_b, normalise(av_b), ow_b)


def _nsa(proj, kc, vct, ksa, vst, kwa, vwt, gt, tt, bc, ovt, s_):
    b_, g_ = kwa.shape[:2]
    n_qb = s_ // Q_BLOCK
    n_p = n_qb // 2
    m = proj.shape[0]
    nl = Q_PER_KV * LANES
    qcol = SEG_Q * D_MODEL // (Q_PER_KV * HD)
    max_chunks = s_ // (FAR_TILES * LANES)

    def far_chunks(i):
        return (max(i - 1, 0) + FAR_TILES - 1) // FAR_TILES

    assert n_qb % 2 == 0 and all(far_chunks(p) + far_chunks(n_qb - 1 - p) == max_chunks for p in range(n_p))

    def bg(shape):
        return pl.BlockSpec((None, None) + shape, lambda b, g, p: (b, g, 0, 0))

    def blk_a(p):
        return p

    def blk_b(p):
        return n_qb - 1 - p

    def per_block(which):
        return [
            pl.BlockSpec((Q_BLOCK, Q_PER_KV * HD), lambda b, g, p: (b * n_qb + which(p), qcol + g)),
            pl.BlockSpec((None, LANES, Q_BLOCK), lambda b, g, p: (b, 0, which(p))),
            pl.BlockSpec((None, None, LANES, nl), lambda b, g, p: (g, which(p), 0, 0)),
        ]

    qa, gta, bca = per_block(blk_a)
    qb, gtb, bcb = per_block(blk_b)
    out = pl.pallas_call(
        _nsa_body,
        grid=(b_, g_, n_p),
        in_specs=[
            qa, qb,
            bg((LANES, HD)), bg((HD, LANES)),
            bg((s_ + KV_PAD, 2 * HD)), bg((HD + ONES_ROWS, s_ + KV_PAD)),
            bg((s_ + KV_PAD, 2 * HD)), bg((HD + ONES_ROWS, s_ + KV_PAD)),
            gta, gtb,
            pl.BlockSpec((None, 3, LANES, nl), lambda b, g, p: (g, 0, 0, 0)),
            bca, bcb,
            pl.BlockSpec((s_ // SEL_BLOCK, LANES), lambda b, g, p: (0, 0)),
        ],
        out_specs=pl.BlockSpec((2, None, Q_BLOCK, Q_PER_KV * HD), lambda b, g, p: (0, b * n_p + p, 0, g)),
        out_shape=jax.ShapeDtypeStruct((2, b_ * n_p, Q_BLOCK, D_MODEL), BF16),
        scratch_shapes=[pltpu.VMEM((max_chunks, FAR_TILES * LANES, nl), F32), pltpu.VMEM((2, nl, 2 * HD), BF16)],
        compiler_params=_params(("parallel", "parallel", "arbitrary")),
        name="nsa",
    )(proj, proj, kc, vct, ksa, vst, kwa, vwt, gt, gt, tt, bc, bc, ovt)
    return out.reshape(m, D_MODEL)


def _attn_block_row(b, qb, n_b, n_qb):
    n_p = n_qb // 2
    upper = qb >= n_p
    return jnp.where(upper, n_b * n_p, 0) + b * n_p + jnp.where(upper, n_qb - 1 - qb, qb)


def _mix_body(gb_ref, gc_ref, va_ref, gch_ref, vah_ref, uc_ref, vc_ref, g0_ref, g1_ref, g2_ref, x_ref,
              ca_ref, sw_ref, sb_ref, lng_ref, lnb_ref, wa_ref, wb_ref, wc_ref, wo_ref, *rest):
    tm = gb_ref.shape[0]
    ob_refs = rest[:tm // Q_BLOCK]
    o_ref, xa_ref, mx_ref = rest[tm // Q_BLOCK:]
    first = pl.program_id(1) == 0

    halo = gch_ref[...].astype(F32) * vah_ref[...].astype(F32)
    xa_ref[0:HALO, :] = jnp.where(first, 0.0, halo)
    xa_ref[HALO:, :] = gc_ref[...].astype(F32) * va_ref[...].astype(F32)
    ca = ca_ref[...]
    row = lax.broadcasted_iota(jnp.int32, (CHUNK, CHUNK), 0)
    col = lax.broadcasted_iota(jnp.int32, (CHUNK, CHUNK), 1)
    sw = [jnp.where(row >= col, sw_ref[gi], 0.0).astype(BF16) for gi in range(SGU_GROUPS)]

    for r in range(0, tm, ROW_CHUNK):
        rows = slice(r, r + ROW_CHUNK)
        conv = (ca[0:1] * xa_ref[pl.ds(r + HALO - 2, ROW_CHUNK), :]
                + ca[1:2] * xa_ref[pl.ds(r + HALO - 1, ROW_CHUNK), :]
                + ca[2:3] * xa_ref[pl.ds(r + HALO, ROW_CHUNK), :])
        out_a = gb_ref[rows, :].astype(F32) * conv
        y_a = jnp.dot(out_a.astype(BF16), wa_ref[...], preferred_element_type=F32)

        v = vc_ref[rows, :].astype(F32)
        mu = jnp.mean(v, axis=-1, keepdims=True)
        vz = v - mu
        var = jnp.mean(vz * vz, axis=-1, keepdims=True)
        vn = (vz * lax.rsqrt(var + EPS) * lng_ref[...] + lnb_ref[...]).astype(BF16)
        for gi in range(SGU_GROUPS):
            cols = slice(gi * LANES, (gi + 1) * LANES)
            for c in range(0, ROW_CHUNK, CHUNK):
                mx_ref[r + c:r + c + CHUNK, cols] = (
                    jnp.dot(sw[gi], vn[c:c + CHUNK, cols], preferred_element_type=F32) + sb_ref[:, cols])
        out_c = uc_ref[rows, :].astype(F32) * mx_ref[rows, :]
        y_c = jnp.dot(out_c.astype(BF16), wc_ref[...], preferred_element_type=F32)

        out_b = jnp.concatenate([ob_refs[k][...] for k in range(r // Q_BLOCK, (r + ROW_CHUNK) // Q_BLOCK)], axis=0)
        y_b = jnp.dot(out_b, wb_ref[...], preferred_element_type=F32)

        merged = (g0_ref[rows, :].astype(F32) * y_a + g1_ref[rows, :].astype(F32) * y_b
                  + g2_ref[rows, :].astype(F32) * y_c)
        o_ref[rows, :] = x_ref[rows, :] + jnp.dot(merged.astype(BF16), wo_ref[...], preferred_element_type=F32)


def _mix(proj, out_b, x2, seq, conv_a, sgu_w, sgu_b2, ln_g, ln_b, w_a, w_b, w_c, w_o, tm=512):
    m = x2.shape[0]
    n_t = seq // tm
    hb = tm // HALO

    def seg(k):
        return pl.BlockSpec((tm, D_MODEL), lambda b, i: (b * n_t + i, k))

    def halo(k):
        return pl.BlockSpec((HALO, D_MODEL), lambda b, i: (jnp.maximum((b * n_t + i) * hb - 1, 0), k))

    def full(shape):
        return pl.BlockSpec(shape, lambda b, i: (0,) * len(shape))

    def attn(k):
        return pl.BlockSpec((Q_BLOCK, D_MODEL), lambda b, i: (
            _attn_block_row(b, i * (tm // Q_BLOCK) + k, m // seq, seq // Q_BLOCK), 0))

    row = pl.BlockSpec((tm, D_MODEL), lambda b, i: (b * n_t + i, 0))
    wspec = pl.BlockSpec((D_MODEL, D_MODEL), lambda b, i: (0, 0), pipeline_mode=pl.Buffered(1))
    return pl.pallas_call(
        _mix_body,
        grid=(m // seq, n_t),
        in_specs=[
            seg(SEG_GB), seg(SEG_GC), seg(SEG_VA), halo(SEG_GC), halo(SEG_VA), seg(SEG_U), seg(SEG_V),
            seg(SEG_G0), seg(SEG_G0 + 1), seg(SEG_G0 + 2), row,
            full((CONV_W, D_MODEL)), full((SGU_GROUPS, CHUNK, CHUNK)), full((CHUNK, D_MODEL)),
            full((1, D_MODEL)), full((1, D_MODEL)), wspec, wspec, wspec, wspec,
        ] + [attn(k) for k in range(tm // Q_BLOCK)],
        out_specs=row,
        out_shape=jax.ShapeDtypeStruct((m, D_MODEL), F32),
        scratch_shapes=[pltpu.VMEM((tm + HALO, D_MODEL), F32), pltpu.VMEM((tm, D_MODEL), F32)],
        compiler_params=_params(("parallel", "arbitrary")),
        name="mix",
    )(proj, proj, proj, proj, proj, proj, proj, proj, proj, proj, x2,
      conv_a, sgu_w, sgu_b2, ln_g, ln_b, w_a, w_b, w_c, w_o, *([out_b] * (tm // Q_BLOCK)))


def _ffn_body(x_ref, xh_ref, g_ref, wg_ref, wv_ref, cw_ref, wd_ref, o_ref, h_ref, gt_ref, acc_ref, *, tiles_per_seq):
    i = pl.program_id(0)
    j = pl.program_id(1)
    tm = x_ref.shape[0]

    @pl.when(j == 0)
    def _():
        h_ref[0:HALO, :] = _rms(xh_ref[...], g_ref[...]).astype(BF16)
        h_ref[HALO:, :] = _rms(x_ref[...], g_ref[...]).astype(BF16)
        acc_ref[...] = jnp.zeros_like(acc_ref)

    gate = jnp.dot(h_ref[...], wg_ref[...], preferred_element_type=F32)
    rows = lax.broadcasted_iota(jnp.int32, gate.shape, 0)
    seq_start = (i % tiles_per_seq) == 0
    gt_ref[...] = jnp.where(seq_start & (rows < HALO), 0.0, gate)
    cw = cw_ref[...]
    conv = (cw[0:1] * gt_ref[pl.ds(HALO - 2, tm), :] + cw[1:2] * gt_ref[pl.ds(HALO - 1, tm), :]
            + cw[2:3] * gt_ref[pl.ds(HALO, tm), :])
    val = jnp.dot(h_ref[HALO:, :], wv_ref[...], preferred_element_type=F32)
    act = (jax.nn.gelu(conv) * val).astype(BF16)
    acc_ref[...] += jnp.dot(act, wd_ref[...], preferred_element_type=F32)

    @pl.when(j == pl.num_programs(1) - 1)
    def _():
        o_ref[...] = x_ref[...] + acc_ref[...]


def _ffn(x2, seq, g, w_up, conv_w, w_down, tm=1024, tf=512):
    m = x2.shape[0]
    n_f = D_FF // tf
    w_up = w_up.reshape(D_MODEL, 2 * n_f, tf).transpose(1, 0, 2)
    hb = tm // HALO
    return pl.pallas_call(
        functools.partial(_ffn_body, tiles_per_seq=seq // tm),
        grid=(m // tm, n_f),
        in_specs=[
            pl.BlockSpec((tm, D_MODEL), lambda i, j: (i, 0)),
            pl.BlockSpec((HALO, D_MODEL), lambda i, j: (jnp.maximum(i * hb - 1, 0), 0)),
            pl.BlockSpec((1, D_MODEL), lambda i, j: (0, 0)),
            pl.BlockSpec((None, D_MODEL, tf), lambda i, j: (j, 0, 0)),
            pl.BlockSpec((None, D_MODEL, tf), lambda i, j: (n_f + j, 0, 0)),
            pl.BlockSpec((CONV_W, tf), lambda i, j: (0, j)),
            pl.BlockSpec((tf, D_MODEL), lambda i, j: (j, 0)),
        ],
        out_specs=pl.BlockSpec((tm, D_MODEL), lambda i, j: (i, 0)),
        out_shape=jax.ShapeDtypeStruct((m, D_MODEL), F32),
        scratch_shapes=[
            pltpu.VMEM((tm + HALO, D_MODEL), BF16),
            pltpu.VMEM((tm + HALO, tf), F32),
            pltpu.VMEM((tm, D_MODEL), F32),
        ],
        compiler_params=_params(("parallel", "arbitrary")),
        name="ffn",
    )(x2, x2, g, w_up, w_up, conv_w, w_down)


def _norm_body(x_ref, g_ref, o_ref):
    o_ref[...] = _rms(x_ref[...], g_ref[...])


def _final_norm(x2, g, tm=1024):
    m = x2.shape[0]
    return pl.pallas_call(
        _norm_body,
        grid=(m // tm,),
        in_specs=[pl.BlockSpec((tm, D_MODEL), lambda i: (i, 0)), pl.BlockSpec((1, D_MODEL), lambda i: (0, 0))],
        out_specs=pl.BlockSpec((tm, D_MODEL), lambda i: (i, 0)),
        out_shape=jax.ShapeDtypeStruct((m, D_MODEL), F32),
        compiler_params=_params(("parallel",)),
        name="final_norm",
    )(x2, g)


def _split_w_in(w_in):
    c = [0]
    for sz in [D_MODEL] * 4 + [N_KV * HD] * 6 + [3 * N_HEADS] + [D_MODEL] * 5:
        c.append(c[-1] + sz)
    main = jnp.concatenate([w_in[:, c[0]:c[4]], w_in[:, c[11]:c[16]]], axis=1)
    main = main.reshape(D_MODEL, N_SEG, D_MODEL).transpose(1, 0, 2)
    kv = w_in[:, c[4]:c[10]].reshape(D_MODEL, 6, N_KV, HD)[:, jnp.array([2, 3, 4, 5, 0, 1])]
    kv = kv.transpose(0, 2, 1, 3).reshape(D_MODEL, N_KVCOL)
    gate = jnp.pad(w_in[:, c[10]:c[11]], ((0, 0), (0, LANES - 3 * N_HEADS)))
    return main.astype(BF16), kv.astype(BF16), gate.astype(BF16)


def _mixer_layer(x2, b_, s_, tt, bc, ovt, norm_g, w_in, conv_a, cmp_pos, cmp_w1, cmp_w2, sgu_w, sgu_b,
                 sgu_norm_g, sgu_norm_b, w_br_a, w_br_b, w_br_c, w_o):
    w_main, w_kv, w_gate = _split_w_in(w_in)
    norm_g = norm_g.reshape(1, D_MODEL)
    proj = _in_proj(x2, norm_g, w_main)
    kc, vc, ksa, vst, kwa, vwt, gt = _kv_proj(x2, b_, s_, norm_g, w_kv, w_gate)

    k_cmp, v_cmp = _compress(kc, vc, cmp_pos.reshape(2, 1, CMP_LEN * HD),
                             cmp_w1.astype(BF16), cmp_w2.astype(BF16))
    out_b = _nsa(proj, k_cmp, v_cmp.transpose(0, 1, 3, 2), ksa, vst, kwa, vwt, gt, tt, bc, ovt, s_)

    sgu_b2 = jnp.broadcast_to(sgu_b.T[:, :, None], (CHUNK, SGU_GROUPS, D_MODEL // SGU_GROUPS)).reshape(CHUNK, D_MODEL)
    return _mix(proj, out_b, x2, s_, conv_a, sgu_w, sgu_b2,
                sgu_norm_g.reshape(1, D_MODEL), sgu_norm_b.reshape(1, D_MODEL),
                w_br_a.astype(BF16), w_br_b.astype(BF16), w_br_c.astype(BF16), w_o.astype(BF16))


def _overlap_t(s_):
    n_blk = s_ // SEL_BLOCK
    cmp_start = jnp.arange(LANES) * CMP_STRIDE
    cmp_end = cmp_start + CMP_LEN - 1
    blk_start = jnp.arange(n_blk) * SEL_BLOCK
    n_cmp = (s_ - CMP_LEN) // CMP_STRIDE + 1
    ov = ((cmp_start[None, :] < blk_start[:, None] + SEL_BLOCK) & (cmp_end[None, :] >= blk_start[:, None])
          & (jnp.arange(LANES)[None, :] < n_cmp))
    return ov.astype(F32)


def kernel(x, rel_bias, norm_mix, w_in, conv_a, cmp_pos, cmp_w1, cmp_w2, sgu_w, sgu_b, sgu_norm_g, sgu_norm_b,
           w_br_a, w_br_b, w_br_c, w_o, norm_ffn, ffn_w_up, ffn_conv, ffn_w_down, norm_final):
    b_, s_, _ = x.shape
    depth = w_in.shape[0]
    x2 = x.reshape(b_ * s_, D_MODEL)
    tt, bc = _bias_tiles(rel_bias, s_ // Q_BLOCK)
    ovt = _overlap_t(s_)
    for l in range(depth):
        x2 = _mixer_layer(x2, b_, s_, tt, bc, ovt, norm_mix[l], w_in[l], conv_a[l], cmp_pos[l], cmp_w1[l],
                          cmp_w2[l], sgu_w[l], sgu_b[l], sgu_norm_g[l], sgu_norm_b[l],
                          w_br_a[l], w_br_b[l], w_br_c[l], w_o[l])
        x2 = _ffn(x2, s_, norm_ffn[l].reshape(1, D_MODEL), ffn_w_up[l].astype(BF16), ffn_conv[l],
                  ffn_w_down[l].astype(BF16))
    return _final_norm(x2, norm_final.reshape(1, D_MODEL)).reshape(b_, s_, D_MODEL)
```

```python
import functools
import math

import jax
import jax.numpy as jnp
from jax import lax
from jax.experimental import pallas as pl
from jax.experimental.pallas import tpu as pltpu

D_MODEL = 1024
HD = 64
N_HEADS = D_MODEL // HD
N_KV = 4
Q_PER_KV = N_HEADS // N_KV
SGU_GROUPS = 8
CHUNK = 128
CONV_W = 3
CMP_LEN = 32
CMP_STRIDE = 16
CMP_HID = 2 * HD
SEL_BLOCK = 64
SEL_TOP_N = 8
WINDOW = 512
Q_BLOCK = 128
D_FF = 3 * D_MODEL
N_BUCKETS = 32
MAX_EXACT = N_BUCKETS // 2
REL_MAX_DIST = 128
SCALE = HD ** -0.5
EPS = 1e-6
NEG_INF = -1e30
FORCE = 1e9
LOG2E = 1.4426950408889634

LANES = 128
ROW_CHUNK = 256
HALO = 16
N_SEG = 9
SEG_GB, SEG_GC, SEG_VA, SEG_Q, SEG_U, SEG_V, SEG_G0 = 0, 1, 2, 3, 4, 5, 6
N_MAIN = N_SEG * D_MODEL
N_KVCOL = 6 * N_KV * HD
KV_PAD = WINDOW
VMEM_LIMIT = 56 * 1024 * 1024

BF16 = jnp.bfloat16
F32 = jnp.float32
_NT = (((1,), (1,)), ((), ()))


def _params(sem):
    return pltpu.CompilerParams(dimension_semantics=sem, vmem_limit_bytes=VMEM_LIMIT)


def _rms(x, g):
    return x * lax.rsqrt(jnp.mean(x * x, axis=-1, keepdims=True) + EPS) * g


def _in_proj_body(x_ref, g_ref, w_ref, o_ref, h_ref, *, gelu_tiles, sigmoid_tiles):
    j = pl.program_id(1)

    @pl.when(j == 0)
    def _():
        h_ref[...] = _rms(x_ref[...], g_ref[...]).astype(BF16)

    def tile(act):
        for r in range(0, h_ref.shape[0], ROW_CHUNK):
            y = jnp.dot(h_ref[r:r + ROW_CHUNK, :], w_ref[...], preferred_element_type=F32)
            o_ref[r:r + ROW_CHUNK, :] = act(y).astype(o_ref.dtype)

    is_gelu = (j >= gelu_tiles[0]) & (j < gelu_tiles[1])
    is_sigmoid = (j >= sigmoid_tiles[0]) & (j < sigmoid_tiles[1])
    pl.when(is_gelu)(lambda: tile(jax.nn.gelu))
    pl.when(is_sigmoid)(lambda: tile(jax.nn.sigmoid))
    pl.when(jnp.logical_not(is_gelu | is_sigmoid))(lambda: tile(lambda y: y))


def _in_proj(x2, g, w_main, tm=1024):
    m = x2.shape[0]
    tn = D_MODEL
    body = functools.partial(_in_proj_body, gelu_tiles=(SEG_U, SEG_V + 1), sigmoid_tiles=(SEG_G0, SEG_G0 + 3))
    return pl.pallas_call(
        body,
        grid=(m // tm, N_SEG),
        in_specs=[
            pl.BlockSpec((tm, D_MODEL), lambda i, j: (i, 0)),
            pl.BlockSpec((1, D_MODEL), lambda i, j: (0, 0)),
            pl.BlockSpec((None, D_MODEL, tn), lambda i, j: (j, 0, 0)),
        ],
        out_specs=pl.BlockSpec((tm, tn), lambda i, j: (i, j)),
        out_shape=jax.ShapeDtypeStruct((m, N_MAIN), BF16),
        scratch_shapes=[pltpu.VMEM((tm, D_MODEL), BF16)],
        compiler_params=_params(("parallel", "arbitrary")),
        name="in_proj",
    )(x2, g, w_main)


def _kv_proj_body(x_ref, g_ref, w_ref, wg_ref, kc_ref, vc_ref, ksa_ref, vst_ref, kwa_ref, vwt_ref, gt_ref, cmp_ref):
    s_id = pl.program_id(1)
    tm = x_ref.shape[0]
    lane = lax.broadcasted_iota(jnp.int32, (tm, LANES), 1)
    ones_rows = (lax.broadcasted_iota(jnp.int32, (ONES_ROWS, tm), 0) == 0).astype(BF16)

    @pl.when(s_id == 0)
    def _():
        pad_keys = jnp.where(lane == HD + PAD_COL, 1.0, 0.0).astype(BF16)
        for g in range(N_KV):
            ksa_ref[g] = pad_keys
            kwa_ref[g] = pad_keys
            vst_ref[g] = jnp.zeros(vst_ref.shape[1:], BF16)
            vwt_ref[g] = jnp.zeros(vwt_ref.shape[1:], BF16)

    @pl.when(s_id > 0)
    def _():
        h = _rms(x_ref[...], g_ref[...]).astype(BF16)
        y = jnp.dot(h, w_ref[...], preferred_element_type=F32)
        tok = (s_id - 1) * tm + lax.broadcasted_iota(jnp.int32, (tm, LANES), 0)
        blk_id = jnp.where(lane - HD == tok // SEL_BLOCK, 1.0, 0.0)
        for g in range(N_KV):
            base = g * 6 * HD
            sel = y[:, base:base + 2 * HD]
            win = y[:, base + 2 * HD:base + 4 * HD]
            cmp = y[:, base + 4 * HD:base + 6 * HD]
            ksa_ref[g] = jnp.where(lane < HD, sel, blk_id).astype(BF16)
            kwa_ref[g] = jnp.where(lane < HD, win, 0.0).astype(BF16)
            vst_ref[g, 0:HD, :] = sel.T[HD:].astype(BF16)
            vst_ref[g, HD:, :] = ones_rows
            vwt_ref[g, 0:HD, :] = win.T[HD:].astype(BF16)
            vwt_ref[g, HD:, :] = ones_rows
            cmp_ref[...] = cmp
            for c in range(CMP_STRIDE):
                every = cmp_ref[pl.ds(c, tm // CMP_STRIDE, stride=CMP_STRIDE), :].astype(BF16)
                kc_ref[g, :, c * HD:(c + 1) * HD] = every[:, 0:HD]
                vc_ref[g, :, c * HD:(c + 1) * HD] = every[:, HD:]
        gt_ref[...] = jax.nn.sigmoid(jnp.dot(h, wg_ref[...], preferred_element_type=F32)).T


def _kv_proj(x2, b_, s_, g, w_kv, w_gate, tm=512):
    n_t = s_ // tm
    assert KV_PAD == tm

    def tok(b, s):
        return jnp.maximum(s - 1, 0)

    return pl.pallas_call(
        _kv_proj_body,
        grid=(b_, n_t + 1),
        in_specs=[
            pl.BlockSpec((tm, D_MODEL), lambda b, s: (b * n_t + tok(b, s), 0)),
            pl.BlockSpec((1, D_MODEL), lambda b, s: (0, 0)),
            pl.BlockSpec((D_MODEL, N_KVCOL), lambda b, s: (0, 0), pipeline_mode=pl.Buffered(1)),
            pl.BlockSpec((D_MODEL, LANES), lambda b, s: (0, 0), pipeline_mode=pl.Buffered(1)),
        ],
        out_specs=[
            pl.BlockSpec((None, N_KV, tm // CMP_STRIDE, CMP_STRIDE * HD), lambda b, s: (b, 0, tok(b, s), 0)),
            pl.BlockSpec((None, N_KV, tm // CMP_STRIDE, CMP_STRIDE * HD), lambda b, s: (b, 0, tok(b, s), 0)),
            pl.BlockSpec((None, N_KV, tm, 2 * HD), lambda b, s: (b, 0, s, 0)),
            pl.BlockSpec((None, N_KV, HD + ONES_ROWS, tm), lambda b, s: (b, 0, 0, s)),
            pl.BlockSpec((None, N_KV, tm, 2 * HD), lambda b, s: (b, 0, s, 0)),
            pl.BlockSpec((None, N_KV, HD + ONES_ROWS, tm), lambda b, s: (b, 0, 0, s)),
            pl.BlockSpec((None, LANES, tm), lambda b, s: (b, 0, tok(b, s))),
        ],
        out_shape=[
            jax.ShapeDtypeStruct((b_, N_KV, s_ // CMP_STRIDE, CMP_STRIDE * HD), BF16),
            jax.ShapeDtypeStruct((b_, N_KV, s_ // CMP_STRIDE, CMP_STRIDE * HD), BF16),
            jax.ShapeDtypeStruct((b_, N_KV, KV_PAD + s_, 2 * HD), BF16),
            jax.ShapeDtypeStruct((b_, N_KV, HD + ONES_ROWS, KV_PAD + s_), BF16),
            jax.ShapeDtypeStruct((b_, N_KV, KV_PAD + s_, 2 * HD), BF16),
            jax.ShapeDtypeStruct((b_, N_KV, HD + ONES_ROWS, KV_PAD + s_), BF16),
            jax.ShapeDtypeStruct((b_, LANES, s_), F32),
        ],
        scratch_shapes=[pltpu.VMEM((tm, 2 * HD), F32)],
        compiler_params=_params(("parallel", "arbitrary")),
        name="kv_proj",
    )(x2, g, w_kv, w_gate)


def _rel_bucket(dist):
    dist = jnp.maximum(dist, 0)
    log_ratio = jnp.log(jnp.maximum(dist, 1).astype(F32) / MAX_EXACT) / math.log(REL_MAX_DIST / MAX_EXACT)
    large = MAX_EXACT + (log_ratio * (N_BUCKETS - MAX_EXACT)).astype(jnp.int32)
    return jnp.where(dist < MAX_EXACT, dist, jnp.minimum(large, N_BUCKETS - 1))


def _bias_body(tab_ref, bt_ref, bc_ref, tt_ref, bco_ref, *, n_cmp):
    h = pl.program_id(0)

    def lookup(bk):
        acc = jnp.zeros(bk.shape, F32)
        for b in range(N_BUCKETS):
            acc = jnp.where(bk == b, tab_ref[b, h], acc)
        return acc

    kk = lax.broadcasted_iota(jnp.int32, (LANES, LANES), 0)
    qq = lax.broadcasted_iota(jnp.int32, (LANES, LANES), 1)
    far = lookup(bt_ref[2])
    tt_ref[0] = (lookup(bt_ref[1]) - far) * LOG2E
    tt_ref[1] = jnp.where(qq >= kk, (lookup(bt_ref[0]) - far) * LOG2E, NEG_INF)
    tt_ref[2] = jnp.where(kk > qq, 0.0, NEG_INF)
    for i in range(bc_ref.shape[0]):
        ok = (i * Q_BLOCK + qq - (kk * CMP_STRIDE + CMP_LEN - 1) >= 0) & (kk < n_cmp)
        bco_ref[i] = jnp.where(ok, lookup(bc_ref[i]), NEG_INF)


def _bias_tiles(rel_bias, n_qb):
    kk = jnp.arange(LANES)[:, None]
    qq = jnp.arange(LANES)[None, :]
    bt = jnp.stack([_rel_bucket(LANES * d + qq - kk) for d in range(3)]).astype(jnp.int32)
    cmp_end = CMP_STRIDE * jnp.arange(LANES) + CMP_LEN - 1
    t = (Q_BLOCK * jnp.arange(n_qb))[:, None, None] + qq[None]
    bc = _rel_bucket(t - cmp_end[None, :, None]).astype(jnp.int32)
    n_cmp = (n_qb * Q_BLOCK - CMP_LEN) // CMP_STRIDE + 1
    return pl.pallas_call(
        functools.partial(_bias_body, n_cmp=n_cmp),
        grid=(N_HEADS,),
        in_specs=[
            pl.BlockSpec(memory_space=pltpu.SMEM),
            pl.BlockSpec((3, LANES, LANES), lambda h: (0, 0, 0)),
            pl.BlockSpec((n_qb, LANES, LANES), lambda h: (0, 0, 0)),
        ],
        out_specs=[
            pl.BlockSpec((None, 3, LANES, LANES), lambda h: (h // Q_PER_KV, 0, 0, h % Q_PER_KV)),
            pl.BlockSpec((None, n_qb, LANES, LANES), lambda h: (h // Q_PER_KV, 0, 0, h % Q_PER_KV)),
        ],
        out_shape=[
            jax.ShapeDtypeStruct((N_KV, 3, LANES, Q_PER_KV * LANES), F32),
            jax.ShapeDtypeStruct((N_KV, n_qb, LANES, Q_PER_KV * LANES), F32),
        ],
        compiler_params=_params(("arbitrary",)),
        name="bias_tiles",
    )(rel_bias.astype(F32), bt, bc)


def _compress_body(k3_ref, v3_ref, pos_ref, w1_ref, w2_ref, ko_ref, vo_ref):
    half = CMP_STRIDE * HD
    for which, (src, dst) in enumerate(((k3_ref, ko_ref), (v3_ref, vo_ref))):
        x = src[...].astype(F32)
        pos = pos_ref[which]
        xa = (x + pos[:, :half]).astype(BF16)
        xb = (x + pos[:, half:]).astype(BF16)
        a = jnp.dot(xa, w1_ref[which, :half, :], preferred_element_type=F32)
        b = jnp.dot(xb, w1_ref[which, half:, :], preferred_element_type=F32)
        hid = jax.nn.gelu(a + pltpu.roll(b, LANES - 1, 0))
        dst[...] = jnp.dot(hid.astype(BF16), w2_ref[which], preferred_element_type=F32).astype(dst.dtype)


def _compress(k3, v3, pos, w1, w2):
    b_, g_, nm, width = k3.shape
    spec3 = pl.BlockSpec((None, None, nm, width), lambda b, g: (b, g, 0, 0))
    ospec = pl.BlockSpec((None, None, nm, HD), lambda b, g: (b, g, 0, 0))
    return pl.pallas_call(
        _compress_body,
        grid=(b_, g_),
        in_specs=[
            spec3, spec3,
            pl.BlockSpec((2, 1, CMP_LEN * HD), lambda b, g: (0, 0, 0)),
            pl.BlockSpec((2, CMP_LEN * HD, CMP_HID), lambda b, g: (0, 0, 0)),
            pl.BlockSpec((2, CMP_HID, HD), lambda b, g: (0, 0, 0)),
        ],
        out_specs=[ospec, ospec],
        out_shape=[jax.ShapeDtypeStruct((b_, g_, nm, HD), BF16)] * 2,
        compiler_params=_params(("parallel", "parallel")),
        name="compress",
    )(k3, v3, pos, w1, w2)


FAR_TILES = 2
PAD_COL = 32
ONES_ROWS = 16


def _nsa_body(qa_ref, qb_ref, kc_ref, vct_ref, ksa_ref, vst_ref, kwa_ref, vwt_ref, gta_ref, gtb_ref,
              tt_ref, bca_ref, bcb_ref, ovt_ref, o_ref, sbuf_ref, qf_ref):
    p_id = pl.program_id(2)
    n_qb = 2 * pl.num_programs(2)
    nl = Q_PER_KV * LANES
    n_win = WINDOW // LANES + 1
    n_blk = ovt_ref.shape[0]
    max_chunks = sbuf_ref.shape[0]
    pad_tiles = KV_PAD // LANES
    j_io = lax.broadcasted_iota(jnp.int32, (n_blk, LANES), 0)
    q_io = lax.broadcasted_iota(jnp.int32, (n_blk, LANES), 1)
    col_w = lax.broadcasted_iota(jnp.int32, (nl, HD), 1)
    pad_cols = jnp.where(col_w == PAD_COL, NEG_INF, 0.0).astype(BF16)

    def compressed(q_ref, bc_ref):
        qt = q_ref[...]
        q = jnp.concatenate([qt[:, r * HD:(r + 1) * HD] for r in range(Q_PER_KV)], axis=0) * SCALE
        s = lax.dot_general(kc_ref[...], q, _NT, preferred_element_type=F32) + bc_ref[...]
        m_c = jnp.max(s, axis=0, keepdims=True)
        p = jnp.exp(s - m_c)
        l_c = jnp.sum(p, axis=0, keepdims=True)
        p_c = p * (jnp.where(m_c > 0.5 * NEG_INF, 1.0, 0.0) / l_c)
        o_c = jnp.dot(vct_ref[...], p_c.astype(BF16), preferred_element_type=F32)
        return o_c, p_c, (q.astype(F32) * LOG2E).astype(BF16)

    def select(i, p_c, q2, which):
        p_sum = p_c[:, 0:LANES]
        for r in range(1, Q_PER_KV):
            p_sum = p_sum + p_c[:, r * LANES:(r + 1) * LANES]
        imp = jnp.dot(ovt_ref[...], p_sum, precision=lax.Precision.HIGHEST, preferred_element_type=F32)
        t_b = i * Q_BLOCK + q_io
        cur = t_b >> 6
        forced = (j_io == 0) | (j_io == cur) | (j_io == cur - 1)
        valid = j_io * SEL_BLOCK <= t_b
        score = jnp.where(forced, FORCE, jnp.where(valid, imp, -FORCE))
        rank = jnp.zeros((n_blk, LANES), jnp.int32)
        for ii in range(n_blk):
            row = score[ii:ii + 1, :]
            beats = (row > score) | ((row == score) & (j_io > ii))
            rank = rank + beats.astype(jnp.int32)
        keep = (rank < min(SEL_TOP_N, n_blk)) & (score >= 0.0)

        def augment(kept):
            cols = jnp.concatenate([jnp.where(kept, 0.0, NEG_INF), jnp.full((8, LANES), NEG_INF, F32),
                                    jnp.zeros((LANES - n_blk - 8, LANES), F32)], axis=0)
            cols_t = cols.T[:, 0:HD].astype(BF16)
            return jnp.concatenate([q2, jnp.concatenate([cols_t] * Q_PER_KV, axis=0)], axis=1)

        qf_ref[which] = augment(keep & (j_io < 2 * (i - 1)))
        return augment(keep)

    def softmax_av(tiles, vt):
        s_all = jnp.concatenate(tiles, axis=0)
        m = jnp.max(s_all, axis=0, keepdims=True)
        p = jnp.exp2(s_all - m)
        return m, jnp.dot(vt, p.astype(BF16), preferred_element_type=F32)

    def normalise(av):
        return av[0:HD] / av[HD:HD + 1]

    def logit_tiles(k_ref, qq, row0, n):
        s_all = lax.dot_general(k_ref[pl.ds(row0, n * LANES), :], qq, _NT, preferred_element_type=F32)
        return [s_all[t * LANES:(t + 1) * LANES] for t in range(n)]

    def window_logits(i, q2):
        r0 = pl.multiple_of((i + pad_tiles + 1 - n_win) * LANES, LANES)
        tw = logit_tiles(kwa_ref, jnp.concatenate([q2, pad_cols], axis=1), r0, n_win)
        tw[0] = tw[0] + tt_ref[2]
        tw[n_win - 2] = tw[n_win - 2] + tt_ref[0]
        tw[n_win - 1] = tw[n_win - 1] + tt_ref[1]
        return tw

    def window_softmax(i, tw):
        r0 = pl.multiple_of((i + pad_tiles + 1 - n_win) * LANES, LANES)
        return normalise(softmax_av(tw, vwt_ref[:, pl.ds(r0, n_win * LANES)])[1])

    def near_logits(i, q_near):
        r1 = pl.multiple_of((i + pad_tiles - 1) * LANES, LANES)
        tn = logit_tiles(ksa_ref, q_near, r1, 2)
        tn[0] = tn[0] + tt_ref[0]
        tn[1] = tn[1] + tt_ref[1]
        return tn

    def near_softmax(i, tn):
        r1 = pl.multiple_of((i + pad_tiles - 1) * LANES, LANES)
        return softmax_av(tn, vst_ref[:, pl.ds(r1, 2 * LANES)])

    i_a = p_id
    i_b = n_qb - 1 - p_id
    oc_a, pc_a, q2_a = compressed(qa_ref, bca_ref)
    oc_b, pc_b, q2_b = compressed(qb_ref, bcb_ref)
    tw_a = window_logits(i_a, q2_a)
    tw_b = window_logits(i_b, q2_b)
    qn_a = select(i_a, pc_a, q2_a, 0)
    ow_a = window_softmax(i_a, tw_a)
    qn_b = select(i_b, pc_b, q2_b, 1)
    ow_b = window_softmax(i_b, tw_b)
    tn_a = near_logits(i_a, qn_a)
    tn_b = near_logits(i_b, qn_b)
    mn_a, avn_a = near_softmax(i_a, tn_a)
    mn_b, avn_b = near_softmax(i_b, tn_b)

    n_a = (jnp.maximum(i_a - 1, 0) + FAR_TILES - 1) // FAR_TILES
    is_a, rows, mx = [], [], []
    for k in range(max_chunks):
        own = k < n_a
        c = jnp.where(own, k, k - n_a)
        r = pl.multiple_of((c * FAR_TILES + pad_tiles) * LANES, LANES)
        s_k = lax.dot_general(ksa_ref[pl.ds(r, FAR_TILES * LANES), :], qf_ref[jnp.where(own, 0, 1)], _NT,
                              preferred_element_type=F32)
        sbuf_ref[k] = s_k
        is_a.append(own)
        rows.append(r)
        mx.append(jnp.max(s_k, axis=0, keepdims=True))
    m_a, m_b = mn_a, mn_b
    for k in range(max_chunks):
        m_a = jnp.where(is_a[k], jnp.maximum(m_a, mx[k]), m_a)
        m_b = jnp.where(is_a[k], m_b, jnp.maximum(m_b, mx[k]))
    av_a = jnp.exp2(mn_a - m_a) * avn_a
    av_b = jnp.exp2(mn_b - m_b) * avn_b
    for k in range(max_chunks):
        p = jnp.exp2(sbuf_ref[k] - jnp.where(is_a[k], m_a, m_b))
        pv = jnp.dot(vst_ref[:, pl.ds(rows[k], FAR_TILES * LANES)], p.astype(BF16), preferred_element_type=F32)
        av_a = av_a + jnp.where(is_a[k], pv, 0.0)
        av_b = av_b + jnp.where(is_a[k], 0.0, pv)

    def finish(half, gt_ref, o_c, o_s, o_w):
        def gate(branch):
            rows = gt_ref[pl.ds(branch * N_HEADS + pl.program_id(1) * Q_PER_KV, Q_PER_KV), :]
            return jnp.concatenate([rows[r:r + 1] for r in range(Q_PER_KV)], axis=1)

        o_t = gate(0) * o_c + gate(1) * o_s + gate(2) * o_w
        for pair in range(Q_PER_KV // 2):
            blk = jnp.concatenate([o_t[:, (2 * pair) * LANES:(2 * pair + 1) * LANES],
                                   o_t[:, (2 * pair + 1) * LANES:(2 * pair + 2) * LANES]], axis=0)
            o_ref[half, :, pair * LANES:(pair + 1) * LANES] = blk.T.astype(o_ref.dtype)

    finish(0, gta_ref, oc_a, normalise(av_a), ow_a)
    finish(1, gtb_ref, oc_b, normalise(av_b), ow_b)


def _nsa(proj, kc, vct, ksa, vst, kwa, vwt, gt, tt, bc, ovt, s_):
    b_, g_ = kwa.shape[:2]
    n_qb = s_ // Q_BLOCK
    n_p = n_qb // 2
    m = proj.shape[0]
    nl = Q_PER_KV * LANES
    qcol = SEG_Q * D_MODEL // (Q_PER_KV * HD)
    def far_chunks(i):
        return (max(i - 1, 0) + FAR_TILES - 1) // FAR_TILES

    max_chunks = far_chunks(0) + far_chunks(n_qb - 1)
    assert n_qb % 2 == 0 and all(far_chunks(p) + far_chunks(n_qb - 1 - p) == max_chunks for p in range(n_p))

    def bg(shape):
        return pl.BlockSpec((None, None) + shape, lambda b, g, p: (b, g, 0, 0))

    def blk_a(p):
        return p

    def blk_b(p):
        return n_qb - 1 - p

    def per_block(which):
        return [
            pl.BlockSpec((Q_BLOCK, Q_PER_KV * HD), lambda b, g, p: (b * n_qb + which(p), qcol + g)),
            pl.BlockSpec((None, LANES, Q_BLOCK), lambda b, g, p: (b, 0, which(p))),
            pl.BlockSpec((None, None, LANES, nl), lambda b, g, p: (g, which(p), 0, 0)),
        ]

    qa, gta, bca = per_block(blk_a)
    qb, gtb, bcb = per_block(blk_b)
    out = pl.pallas_call(
        _nsa_body,
        grid=(b_, g_, n_p),
        in_specs=[
            qa, qb,
            bg((LANES, HD)), bg((HD, LANES)),
            bg((s_ + KV_PAD, 2 * HD)), bg((HD + ONES_ROWS, s_ + KV_PAD)),
            bg((s_ + KV_PAD, 2 * HD)), bg((HD + ONES_ROWS, s_ + KV_PAD)),
            gta, gtb,
            pl.BlockSpec((None, 3, LANES, nl), lambda b, g, p: (g, 0, 0, 0)),
            bca, bcb,
            pl.BlockSpec((s_ // SEL_BLOCK, LANES), lambda b, g, p: (0, 0)),
        ],
        out_specs=pl.BlockSpec((2, None, Q_BLOCK, Q_PER_KV * HD), lambda b, g, p: (0, b * n_p + p, 0, g)),
        out_shape=jax.ShapeDtypeStruct((2, b_ * n_p, Q_BLOCK, D_MODEL), BF16),
        scratch_shapes=[pltpu.VMEM((max_chunks, FAR_TILES * LANES, nl), F32), pltpu.VMEM((2, nl, 2 * HD), BF16)],
        compiler_params=_params(("parallel", "parallel", "arbitrary")),
        name="nsa",
    )(proj, proj, kc, vct, ksa, vst, kwa, vwt, gt, gt, tt, bc, bc, ovt)
    return out.reshape(m, D_MODEL)


def _attn_block_row(b, qb, n_b, n_qb):
    n_p = n_qb // 2
    upper = qb >= n_p
    return jnp.where(upper, n_b * n_p, 0) + b * n_p + jnp.where(upper, n_qb - 1 - qb, qb)


def _mix_body(gb_ref, gc_ref, va_ref, gch_ref, vah_ref, uc_ref, vc_ref, g0_ref, g1_ref, g2_ref, x_ref,
              ca_ref, sw_ref, sb_ref, lng_ref, lnb_ref, wa_ref, wb_ref, wc_ref, wo_ref, *rest):
    tm = gb_ref.shape[0]
    ob_refs = rest[:tm // Q_BLOCK]
    o_ref, xa_ref, mx_ref = rest[tm // Q_BLOCK:]
    first = pl.program_id(1) == 0

    halo = gch_ref[...].astype(F32) * vah_ref[...].astype(F32)
    xa_ref[0:HALO, :] = jnp.where(first, 0.0, halo)
    xa_ref[HALO:, :] = gc_ref[...].astype(F32) * va_ref[...].astype(F32)
    ca = ca_ref[...]
    row = lax.broadcasted_iota(jnp.int32, (CHUNK, CHUNK), 0)
    col = lax.broadcasted_iota(jnp.int32, (CHUNK, CHUNK), 1)
    sw = [jnp.where(row >= col, sw_ref[gi], 0.0).astype(BF16) for gi in range(SGU_GROUPS)]

    for r in range(0, tm, ROW_CHUNK):
        rows = slice(r, r + ROW_CHUNK)
        conv = (ca[0:1] * xa_ref[pl.ds(r + HALO - 2, ROW_CHUNK), :]
                + ca[1:2] * xa_ref[pl.ds(r + HALO - 1, ROW_CHUNK), :]
                + ca[2:3] * xa_ref[pl.ds(r + HALO, ROW_CHUNK), :])
        out_a = gb_ref[rows, :].astype(F32) * conv
        y_a = jnp.dot(out_a.astype(BF16), wa_ref[...], preferred_element_type=F32)

        v = vc_ref[rows, :].astype(F32)
        mu = jnp.mean(v, axis=-1, keepdims=True)
        vz = v - mu
        var = jnp.mean(vz * vz, axis=-1, keepdims=True)
        vn = (vz * lax.rsqrt(var + EPS) * lng_ref[...] + lnb_ref[...]).astype(BF16)
        for gi in range(SGU_GROUPS):
            cols = slice(gi * LANES, (gi + 1) * LANES)
            for c in range(0, ROW_CHUNK, CHUNK):
                mx_ref[r + c:r + c + CHUNK, cols] = (
                    jnp.dot(sw[gi], vn[c:c + CHUNK, cols], preferred_element_type=F32) + sb_ref[:, cols])
        out_c = uc_ref[rows, :].astype(F32) * mx_ref[rows, :]
        y_c = jnp.dot(out_c.astype(BF16), wc_ref[...], preferred_element_type=F32)

        out_b = jnp.concatenate([ob_refs[k][...] for k in range(r // Q_BLOCK, (r + ROW_CHUNK) // Q_BLOCK)], axis=0)
        y_b = jnp.dot(out_b, wb_ref[...], preferred_element_type=F32)

        merged = (g0_ref[rows, :].astype(F32) * y_a + g1_ref[rows, :].astype(F32) * y_b
                  + g2_ref[rows, :].astype(F32) * y_c)
        o_ref[rows, :] = x_ref[rows, :] + jnp.dot(merged.astype(BF16), wo_ref[...], preferred_element_type=F32)


def _mix(proj, out_b, x2, seq, conv_a, sgu_w, sgu_b2, ln_g, ln_b, w_a, w_b, w_c, w_o, tm=512):
    m = x2.shape[0]
    n_t = seq // tm
    hb = tm // HALO

    def seg(k):
        return pl.BlockSpec((tm, D_MODEL), lambda b, i: (b * n_t + i, k))

    def halo(k):
        return pl.BlockSpec((HALO, D_MODEL), lambda b, i: (jnp.maximum((b * n_t + i) * hb - 1, 0), k))

    def full(shape):
        return pl.BlockSpec(shape, lambda b, i: (0,) * len(shape))

    def attn(k):
        return pl.BlockSpec((Q_BLOCK, D_MODEL), lambda b, i: (
            _attn_block_row(b, i * (tm // Q_BLOCK) + k, m // seq, seq // Q_BLOCK), 0))

    row = pl.BlockSpec((tm, D_MODEL), lambda b, i: (b * n_t + i, 0))
    wspec = pl.BlockSpec((D_MODEL, D_MODEL), lambda b, i: (0, 0), pipeline_mode=pl.Buffered(1))
    return pl.pallas_call(
        _mix_body,
        grid=(m // seq, n_t),
        in_specs=[
            seg(SEG_GB), seg(SEG_GC), seg(SEG_VA), halo(SEG_GC), halo(SEG_VA), seg(SEG_U), seg(SEG_V),
            seg(SEG_G0), seg(SEG_G0 + 1), seg(SEG_G0 + 2), row,
            full((CONV_W, D_MODEL)), full((SGU_GROUPS, CHUNK, CHUNK)), full((CHUNK, D_MODEL)),
            full((1, D_MODEL)), full((1, D_MODEL)), wspec, wspec, wspec, wspec,
        ] + [attn(k) for k in range(tm // Q_BLOCK)],
        out_specs=row,
        out_shape=jax.ShapeDtypeStruct((m, D_MODEL), F32),
        scratch_shapes=[pltpu.VMEM((tm + HALO, D_MODEL), F32), pltpu.VMEM((tm, D_MODEL), F32)],
        compiler_params=_params(("parallel", "arbitrary")),
        name="mix",
    )(proj, proj, proj, proj, proj, proj, proj, proj, proj, proj, x2,
      conv_a, sgu_w, sgu_b2, ln_g, ln_b, w_a, w_b, w_c, w_o, *([out_b] * (tm // Q_BLOCK)))


def _ffn_body(x_ref, xh_ref, g_ref, wg_ref, wv_ref, cw_ref, wd_ref, o_ref, h_ref, gt_ref, acc_ref, *, tiles_per_seq):
    i = pl.program_id(0)
    j = pl.program_id(1)
    tm = x_ref.shape[0]

    @pl.when(j == 0)
    def _():
        h_ref[0:HALO, :] = _rms(xh_ref[...], g_ref[...]).astype(BF16)
        h_ref[HALO:, :] = _rms(x_ref[...], g_ref[...]).astype(BF16)
        acc_ref[...] = jnp.zeros_like(acc_ref)

    gate = jnp.dot(h_ref[...], wg_ref[...], preferred_element_type=F32)
    rows = lax.broadcasted_iota(jnp.int32, gate.shape, 0)
    seq_start = (i % tiles_per_seq) == 0
    gt_ref[...] = jnp.where(seq_start & (rows < HALO), 0.0, gate)
    cw = cw_ref[...]
    conv = (cw[0:1] * gt_ref[pl.ds(HALO - 2, tm), :] + cw[1:2] * gt_ref[pl.ds(HALO - 1, tm), :]
            + cw[2:3] * gt_ref[pl.ds(HALO, tm), :])
    val = jnp.dot(h_ref[HALO:, :], wv_ref[...], preferred_element_type=F32)
    act = (jax.nn.gelu(conv) * val).astype(BF16)
    acc_ref[...] += jnp.dot(act, wd_ref[...], preferred_element_type=F32)

    @pl.when(j == pl.num_programs(1) - 1)
    def _():
        o_ref[...] = x_ref[...] + acc_ref[...]


def _ffn(x2, seq, g, w_up, conv_w, w_down, tm=1024, tf=512):
    m = x2.shape[0]
    n_f = D_FF // tf
    w_up = w_up.reshape(D_MODEL, 2 * n_f, tf).transpose(1, 0, 2)
    hb = tm // HALO
    return pl.pallas_call(
        functools.partial(_ffn_body, tiles_per_seq=seq // tm),
        grid=(m // tm, n_f),
        in_specs=[
            pl.BlockSpec((tm, D_MODEL), lambda i, j: (i, 0)),
            pl.BlockSpec((HALO, D_MODEL), lambda i, j: (jnp.maximum(i * hb - 1, 0), 0)),
            pl.BlockSpec((1, D_MODEL), lambda i, j: (0, 0)),
            pl.BlockSpec((None, D_MODEL, tf), lambda i, j: (j, 0, 0)),
            pl.BlockSpec((None, D_MODEL, tf), lambda i, j: (n_f + j, 0, 0)),
            pl.BlockSpec((CONV_W, tf), lambda i, j: (0, j)),
            pl.BlockSpec((tf, D_MODEL), lambda i, j: (j, 0)),
        ],
        out_specs=pl.BlockSpec((tm, D_MODEL), lambda i, j: (i, 0)),
        out_shape=jax.ShapeDtypeStruct((m, D_MODEL), F32),
        scratch_shapes=[
            pltpu.VMEM((tm + HALO, D_MODEL), BF16),
            pltpu.VMEM((tm + HALO, tf), F32),
            pltpu.VMEM((tm, D_MODEL), F32),
        ],
        compiler_params=_params(("parallel", "arbitrary")),
        name="ffn",
    )(x2, x2, g, w_up, w_up, conv_w, w_down)


def _norm_body(x_ref, g_ref, o_ref):
    o_ref[...] = _rms(x_ref[...], g_ref[...])


def _final_norm(x2, g, tm=1024):
    m = x2.shape[0]
    return pl.pallas_call(
        _norm_body,
        grid=(m // tm,),
        in_specs=[pl.BlockSpec((tm, D_MODEL), lambda i: (i, 0)), pl.BlockSpec((1, D_MODEL), lambda i: (0, 0))],
        out_specs=pl.BlockSpec((tm, D_MODEL), lambda i: (i, 0)),
        out_shape=jax.ShapeDtypeStruct((m, D_MODEL), F32),
        compiler_params=_params(("parallel",)),
        name="final_norm",
    )(x2, g)


def _split_w_in(w_in):
    c = [0]
    for sz in [D_MODEL] * 4 + [N_KV * HD] * 6 + [3 * N_HEADS] + [D_MODEL] * 5:
        c.append(c[-1] + sz)
    w_in = w_in.astype(BF16)
    main = jnp.concatenate([w_in[:, c[0]:c[4]], w_in[:, c[11]:c[16]]], axis=1)
    main = main.reshape(D_MODEL, N_SEG, D_MODEL).transpose(1, 0, 2)
    kv = w_in[:, c[4]:c[10]].reshape(D_MODEL, 6, N_KV, HD)[:, jnp.array([2, 3, 4, 5, 0, 1])]
    kv = kv.transpose(0, 2, 1, 3).reshape(D_MODEL, N_KVCOL)
    gate = jnp.pad(w_in[:, c[10]:c[11]], ((0, 0), (0, LANES - 3 * N_HEADS)))
    return main, kv, gate


def _mixer_layer(x2, b_, s_, tt, bc, ovt, norm_g, w_in, conv_a, cmp_pos, cmp_w1, cmp_w2, sgu_w, sgu_b,
                 sgu_norm_g, sgu_norm_b, w_br_a, w_br_b, w_br_c, w_o):
    w_main, w_kv, w_gate = _split_w_in(w_in)
    norm_g = norm_g.reshape(1, D_MODEL)
    proj = _in_proj(x2, norm_g, w_main)
    kc, vc, ksa, vst, kwa, vwt, gt = _kv_proj(x2, b_, s_, norm_g, w_kv, w_gate)

    k_cmp, v_cmp = _compress(kc, vc, cmp_pos.reshape(2, 1, CMP_LEN * HD),
                             cmp_w1.astype(BF16), cmp_w2.astype(BF16))
    out_b = _nsa(proj, k_cmp, v_cmp.transpose(0, 1, 3, 2), ksa, vst, kwa, vwt, gt, tt, bc, ovt, s_)

    sgu_b2 = jnp.broadcast_to(sgu_b.T[:, :, None], (CHUNK, SGU_GROUPS, D_MODEL // SGU_GROUPS)).reshape(CHUNK, D_MODEL)
    return _mix(proj, out_b, x2, s_, conv_a, sgu_w, sgu_b2,
                sgu_norm_g.reshape(1, D_MODEL), sgu_norm_b.reshape(1, D_MODEL),
                w_br_a.astype(BF16), w_br_b.astype(BF16), w_br_c.astype(BF16), w_o.astype(BF16))


def _overlap_t(s_):
    n_blk = s_ // SEL_BLOCK
    cmp_start = jnp.arange(LANES) * CMP_STRIDE
    cmp_end = cmp_start + CMP_LEN - 1
    blk_start = jnp.arange(n_blk) * SEL_BLOCK
    n_cmp = (s_ - CMP_LEN) // CMP_STRIDE + 1
    ov = ((cmp_start[None, :] < blk_start[:, None] + SEL_BLOCK) & (cmp_end[None, :] >= blk_start[:, None])
          & (jnp.arange(LANES)[None, :] < n_cmp))
    return ov.astype(F32)


def kernel(x, rel_bias, norm_mix, w_in, conv_a, cmp_pos, cmp_w1, cmp_w2, sgu_w, sgu_b, sgu_norm_g, sgu_norm_b,
           w_br_a, w_br_b, w_br_c, w_o, norm_ffn, ffn_w_up, ffn_conv, ffn_w_down, norm_final):
    b_, s_, _ = x.shape
    depth = w_in.shape[0]
    x2 = x.reshape(b_ * s_, D_MODEL)
    tt, bc = _bias_tiles(rel_bias, s_ // Q_BLOCK)
    ovt = _overlap_t(s_)
    for l in range(depth):
        x2 = _mixer_layer(x2, b_, s_, tt, bc, ovt, norm_mix[l], w_in[l], conv_a[l], cmp_pos[l], cmp_w1[l],
                          cmp_w2[l], sgu_w[l], sgu_b[l], sgu_norm_g[l], sgu_norm_b[l],
                          w_br_a[l], w_br_b[l], w_br_c[l], w_o[l])
        x2 = _ffn(x2, s_, norm_ffn[l].reshape(1, D_MODEL), ffn_w_up[l].astype(BF16), ffn_conv[l],
                  ffn_w_down[l].astype(BF16))
    return _final_norm(x2, norm_final.reshape(1, D_MODEL)).reshape(b_, s_, D_MODEL)
```

```python
import functools
import math

import jax
import jax.numpy as jnp
from jax import lax
from jax.experimental import pallas as pl
from jax.experimental.pallas import tpu as pltpu

D_MODEL = 1024
HD = 64
N_HEADS = D_MODEL // HD
N_KV = 4
Q_PER_KV = N_HEADS // N_KV
SGU_GROUPS = 8
CHUNK = 128
CONV_W = 3
CMP_LEN = 32
CMP_STRIDE = 16
CMP_HID = 2 * HD
SEL_BLOCK = 64
SEL_TOP_N = 8
WINDOW = 512
Q_BLOCK = 128
D_FF = 3 * D_MODEL
N_BUCKETS = 32
MAX_EXACT = N_BUCKETS // 2
REL_MAX_DIST = 128
SCALE = HD ** -0.5
EPS = 1e-6
NEG_INF = -1e30
FORCE = 1e9
LOG2E = 1.4426950408889634

LANES = 128
ROW_CHUNK = 256
HALO = 16
N_SEG = 9
SEG_GB, SEG_GC, SEG_VA, SEG_Q, SEG_U, SEG_V, SEG_G0 = 0, 1, 2, 3, 4, 5, 6
N_MAIN = N_SEG * D_MODEL
N_KVCOL = 6 * N_KV * HD
KV_PAD = WINDOW
VMEM_LIMIT = 56 * 1024 * 1024

BF16 = jnp.bfloat16
F32 = jnp.float32
_NT = (((1,), (1,)), ((), ()))


def _params(sem):
    return pltpu.CompilerParams(dimension_semantics=sem, vmem_limit_bytes=VMEM_LIMIT)


def _rms(x, g):
    return x * lax.rsqrt(jnp.mean(x * x, axis=-1, keepdims=True) + EPS) * g


def _in_proj_body(x_ref, g_ref, w_ref, o_ref, h_ref, *, gelu_tiles, sigmoid_tiles):
    j = pl.program_id(1)

    @pl.when(j == 0)
    def _():
        h_ref[...] = _rms(x_ref[...], g_ref[...]).astype(BF16)

    def tile(act):
        for r in range(0, h_ref.shape[0], ROW_CHUNK):
            y = jnp.dot(h_ref[r:r + ROW_CHUNK, :], w_ref[...], preferred_element_type=F32)
            o_ref[r:r + ROW_CHUNK, :] = act(y).astype(o_ref.dtype)

    is_gelu = (j >= gelu_tiles[0]) & (j < gelu_tiles[1])
    is_sigmoid = (j >= sigmoid_tiles[0]) & (j < sigmoid_tiles[1])
    pl.when(is_gelu)(lambda: tile(jax.nn.gelu))
    pl.when(is_sigmoid)(lambda: tile(jax.nn.sigmoid))
    pl.when(jnp.logical_not(is_gelu | is_sigmoid))(lambda: tile(lambda y: y))


def _in_proj(x2, g, w_main, tm=1024):
    m = x2.shape[0]
    tn = D_MODEL
    body = functools.partial(_in_proj_body, gelu_tiles=(SEG_U, SEG_V + 1), sigmoid_tiles=(SEG_G0, SEG_G0 + 3))
    return pl.pallas_call(
        body,
        grid=(m // tm, N_SEG),
        in_specs=[
            pl.BlockSpec((tm, D_MODEL), lambda i, j: (i, 0)),
            pl.BlockSpec((1, D_MODEL), lambda i, j: (0, 0)),
            pl.BlockSpec((None, D_MODEL, tn), lambda i, j: (j, 0, 0)),
        ],
        out_specs=pl.BlockSpec((tm, tn), lambda i, j: (i, j)),
        out_shape=jax.ShapeDtypeStruct((m, N_MAIN), BF16),
        scratch_shapes=[pltpu.VMEM((tm, D_MODEL), BF16)],
        compiler_params=_params(("parallel", "arbitrary")),
        name="in_proj",
    )(x2, g, w_main)


def _kv_proj_body(x_ref, g_ref, w_ref, wg_ref, kc_ref, vc_ref, ksa_ref, vst_ref, kwa_ref, vwt_ref, gt_ref, cmp_ref):
    s_id = pl.program_id(1)
    tm = x_ref.shape[0]
    lane = lax.broadcasted_iota(jnp.int32, (tm, LANES), 1)
    ones_rows = (lax.broadcasted_iota(jnp.int32, (ONES_ROWS, tm), 0) == 0).astype(BF16)

    @pl.when(s_id == 0)
    def _():
        pad_keys = jnp.where(lane == HD + PAD_COL, 1.0, 0.0).astype(BF16)
        for g in range(N_KV):
            ksa_ref[g] = pad_keys
            kwa_ref[g] = pad_keys
            vst_ref[g] = jnp.zeros(vst_ref.shape[1:], BF16)
            vwt_ref[g] = jnp.zeros(vwt_ref.shape[1:], BF16)

    @pl.when(s_id > 0)
    def _():
        h = _rms(x_ref[...], g_ref[...]).astype(BF16)
        y = jnp.dot(h, w_ref[...], preferred_element_type=F32)
        tok = (s_id - 1) * tm + lax.broadcasted_iota(jnp.int32, (tm, LANES), 0)
        blk_id = jnp.where(lane - HD == tok // SEL_BLOCK, 1.0, 0.0)
        for g in range(N_KV):
            base = g * 6 * HD
            sel = y[:, base:base + 2 * HD]
            win = y[:, base + 2 * HD:base + 4 * HD]
            cmp = y[:, base + 4 * HD:base + 6 * HD]
            ksa_ref[g] = jnp.where(lane < HD, sel, blk_id).astype(BF16)
            kwa_ref[g] = jnp.where(lane < HD, win, 0.0).astype(BF16)
            vst_ref[g, 0:HD, :] = sel.T[HD:].astype(BF16)
            vst_ref[g, HD:, :] = ones_rows
            vwt_ref[g, 0:HD, :] = win.T[HD:].astype(BF16)
            vwt_ref[g, HD:, :] = ones_rows
            cmp_ref[...] = cmp
            for c in range(CMP_STRIDE):
                every = cmp_ref[pl.ds(c, tm // CMP_STRIDE, stride=CMP_STRIDE), :].astype(BF16)
                kc_ref[g, :, c * HD:(c + 1) * HD] = every[:, 0:HD]
                vc_ref[g, :, c * HD:(c + 1) * HD] = every[:, HD:]
        gt_ref[...] = jax.nn.sigmoid(jnp.dot(h, wg_ref[...], preferred_element_type=F32)).T


def _kv_proj(x2, b_, s_, g, w_kv, w_gate, tm=512):
    n_t = s_ // tm
    assert KV_PAD == tm

    def tok(b, s):
        return jnp.maximum(s - 1, 0)

    return pl.pallas_call(
        _kv_proj_body,
        grid=(b_, n_t + 1),
        in_specs=[
            pl.BlockSpec((tm, D_MODEL), lambda b, s: (b * n_t + tok(b, s), 0)),
            pl.BlockSpec((1, D_MODEL), lambda b, s: (0, 0)),
            pl.BlockSpec((D_MODEL, N_KVCOL), lambda b, s: (0, 0), pipeline_mode=pl.Buffered(1)),
            pl.BlockSpec((D_MODEL, LANES), lambda b, s: (0, 0), pipeline_mode=pl.Buffered(1)),
        ],
        out_specs=[
            pl.BlockSpec((None, N_KV, tm // CMP_STRIDE, CMP_STRIDE * HD), lambda b, s: (b, 0, tok(b, s), 0)),
            pl.BlockSpec((None, N_KV, tm // CMP_STRIDE, CMP_STRIDE * HD), lambda b, s: (b, 0, tok(b, s), 0)),
            pl.BlockSpec((None, N_KV, tm, 2 * HD), lambda b, s: (b, 0, s, 0)),
            pl.BlockSpec((None, N_KV, HD + ONES_ROWS, tm), lambda b, s: (b, 0, 0, s)),
            pl.BlockSpec((None, N_KV, tm, 2 * HD), lambda b, s: (b, 0, s, 0)),
            pl.BlockSpec((None, N_KV, HD + ONES_ROWS, tm), lambda b, s: (b, 0, 0, s)),
            pl.BlockSpec((None, LANES, tm), lambda b, s: (b, 0, tok(b, s))),
        ],
        out_shape=[
            jax.ShapeDtypeStruct((b_, N_KV, s_ // CMP_STRIDE, CMP_STRIDE * HD), BF16),
            jax.ShapeDtypeStruct((b_, N_KV, s_ // CMP_STRIDE, CMP_STRIDE * HD), BF16),
            jax.ShapeDtypeStruct((b_, N_KV, KV_PAD + s_, 2 * HD), BF16),
            jax.ShapeDtypeStruct((b_, N_KV, HD + ONES_ROWS, KV_PAD + s_), BF16),
            jax.ShapeDtypeStruct((b_, N_KV, KV_PAD + s_, 2 * HD), BF16),
            jax.ShapeDtypeStruct((b_, N_KV, HD + ONES_ROWS, KV_PAD + s_), BF16),
            jax.ShapeDtypeStruct((b_, LANES, s_), F32),
        ],
        scratch_shapes=[pltpu.VMEM((tm, 2 * HD), F32)],
        compiler_params=_params(("parallel", "arbitrary")),
        name="kv_proj",
    )(x2, g, w_kv, w_gate)


def _rel_bucket(dist):
    dist = jnp.maximum(dist, 0)
    log_ratio = jnp.log(jnp.maximum(dist, 1).astype(F32) / MAX_EXACT) / math.log(REL_MAX_DIST / MAX_EXACT)
    large = MAX_EXACT + (log_ratio * (N_BUCKETS - MAX_EXACT)).astype(jnp.int32)
    return jnp.where(dist < MAX_EXACT, dist, jnp.minimum(large, N_BUCKETS - 1))


def _bias_body(tab_ref, bt_ref, bc_ref, tt_ref, bco_ref, *, n_cmp):
    h = pl.program_id(0)

    def lookup(bk):
        acc = jnp.zeros(bk.shape, F32)
        for b in range(N_BUCKETS):
            acc = jnp.where(bk == b, tab_ref[b, h], acc)
        return acc

    kk = lax.broadcasted_iota(jnp.int32, (LANES, LANES), 0)
    qq = lax.broadcasted_iota(jnp.int32, (LANES, LANES), 1)
    far = lookup(bt_ref[2])
    tt_ref[0] = (lookup(bt_ref[1]) - far) * LOG2E
    tt_ref[1] = jnp.where(qq >= kk, (lookup(bt_ref[0]) - far) * LOG2E, NEG_INF)
    tt_ref[2] = jnp.where(kk > qq, 0.0, NEG_INF)
    for i in range(bc_ref.shape[0]):
        ok = (i * Q_BLOCK + qq - (kk * CMP_STRIDE + CMP_LEN - 1) >= 0) & (kk < n_cmp)
        bco_ref[i] = jnp.where(ok, lookup(bc_ref[i]), NEG_INF)


def _bias_tiles(rel_bias, n_qb):
    kk = jnp.arange(LANES)[:, None]
    qq = jnp.arange(LANES)[None, :]
    bt = jnp.stack([_rel_bucket(LANES * d + qq - kk) for d in range(3)]).astype(jnp.int32)
    cmp_end = CMP_STRIDE * jnp.arange(LANES) + CMP_LEN - 1
    t = (Q_BLOCK * jnp.arange(n_qb))[:, None, None] + qq[None]
    bc = _rel_bucket(t - cmp_end[None, :, None]).astype(jnp.int32)
    n_cmp = (n_qb * Q_BLOCK - CMP_LEN) // CMP_STRIDE + 1
    return pl.pallas_call(
        functools.partial(_bias_body, n_cmp=n_cmp),
        grid=(N_HEADS,),
        in_specs=[
            pl.BlockSpec(memory_space=pltpu.SMEM),
            pl.BlockSpec((3, LANES, LANES), lambda h: (0, 0, 0)),
            pl.BlockSpec((n_qb, LANES, LANES), lambda h: (0, 0, 0)),
        ],
        out_specs=[
            pl.BlockSpec((None, 3, LANES, LANES), lambda h: (h // Q_PER_KV, 0, 0, h % Q_PER_KV)),
            pl.BlockSpec((None, n_qb, LANES, LANES), lambda h: (h // Q_PER_KV, 0, 0, h % Q_PER_KV)),
        ],
        out_shape=[
            jax.ShapeDtypeStruct((N_KV, 3, LANES, Q_PER_KV * LANES), F32),
            jax.ShapeDtypeStruct((N_KV, n_qb, LANES, Q_PER_KV * LANES), F32),
        ],
        compiler_params=_params(("arbitrary",)),
        name="bias_tiles",
    )(rel_bias.astype(F32), bt, bc)


def _compress_body(k3_ref, v3_ref, pos_ref, w1_ref, w2_ref, ko_ref, vo_ref):
    half = CMP_STRIDE * HD
    for which, (src, dst) in enumerate(((k3_ref, ko_ref), (v3_ref, vo_ref))):
        x = src[...].astype(F32)
        pos = pos_ref[which]
        xa = (x + pos[:, :half]).astype(BF16)
        xb = (x + pos[:, half:]).astype(BF16)
        a = jnp.dot(xa, w1_ref[which, :half, :], preferred_element_type=F32)
        b = jnp.dot(xb, w1_ref[which, half:, :], preferred_element_type=F32)
        hid = jax.nn.gelu(a + pltpu.roll(b, LANES - 1, 0))
        dst[...] = jnp.dot(hid.astype(BF16), w2_ref[which], preferred_element_type=F32).astype(dst.dtype)


def _compress(k3, v3, pos, w1, w2):
    b_, g_, nm, width = k3.shape
    spec3 = pl.BlockSpec((None, None, nm, width), lambda b, g: (b, g, 0, 0))
    ospec = pl.BlockSpec((None, None, nm, HD), lambda b, g: (b, g, 0, 0))
    return pl.pallas_call(
        _compress_body,
        grid=(b_, g_),
        in_specs=[
            spec3, spec3,
            pl.BlockSpec((2, 1, CMP_LEN * HD), lambda b, g: (0, 0, 0)),
            pl.BlockSpec((2, CMP_LEN * HD, CMP_HID), lambda b, g: (0, 0, 0)),
            pl.BlockSpec((2, CMP_HID, HD), lambda b, g: (0, 0, 0)),
        ],
        out_specs=[ospec, ospec],
        out_shape=[jax.ShapeDtypeStruct((b_, g_, nm, HD), BF16)] * 2,
        compiler_params=_params(("parallel", "parallel")),
        name="compress",
    )(k3, v3, pos, w1, w2)


FAR_TILES = 2
PAD_COL = 32
ONES_ROWS = 16


def _nsa_body(qa_ref, qb_ref, kc_ref, vct_ref, ksa_ref, vst_ref, kwa_ref, vwt_ref, gta_ref, gtb_ref,
              tt_ref, bca_ref, bcb_ref, ovt_ref, o_ref, sbuf_ref, qf_ref):
    p_id = pl.program_id(2)
    n_qb = 2 * pl.num_programs(2)
    nl = Q_PER_KV * LANES
    n_win = WINDOW // LANES + 1
    n_blk = ovt_ref.shape[0]
    max_chunks = sbuf_ref.shape[0]
    pad_tiles = KV_PAD // LANES
    j_io = lax.broadcasted_iota(jnp.int32, (n_blk, LANES), 0)
    q_io = lax.broadcasted_iota(jnp.int32, (n_blk, LANES), 1)
    col_w = lax.broadcasted_iota(jnp.int32, (nl, HD), 1)
    pad_cols = jnp.where(col_w == PAD_COL, NEG_INF, 0.0).astype(BF16)

    def compressed(q_ref, bc_ref):
        qt = q_ref[...]
        q = jnp.concatenate([qt[:, r * HD:(r + 1) * HD] for r in range(Q_PER_KV)], axis=0) * SCALE
        s = lax.dot_general(kc_ref[...], q, _NT, preferred_element_type=F32) + bc_ref[...]
        m_c = jnp.max(s, axis=0, keepdims=True)
        p = jnp.exp(s - m_c)
        l_c = jnp.sum(p, axis=0, keepdims=True)
        p_c = p * (jnp.where(m_c > 0.5 * NEG_INF, 1.0, 0.0) / l_c)
        o_c = jnp.dot(vct_ref[...], p_c.astype(BF16), preferred_element_type=F32)
        return o_c, p_c, (q.astype(F32) * LOG2E).astype(BF16)

    def select(i, p_c, q2, which):
        p_sum = p_c[:, 0:LANES]
        for r in range(1, Q_PER_KV):
            p_sum = p_sum + p_c[:, r * LANES:(r + 1) * LANES]
        imp = jnp.dot(ovt_ref[...], p_sum, precision=lax.Precision.HIGHEST, preferred_element_type=F32)
        t_b = i * Q_BLOCK + q_io
        cur = t_b >> 6
        forced = (j_io == 0) | (j_io == cur) | (j_io == cur - 1)
        valid = j_io * SEL_BLOCK <= t_b
        score = jnp.where(forced, FORCE, jnp.where(valid, imp, -FORCE))
        rank = jnp.zeros((n_blk, LANES), jnp.int32)
        for ii in range(n_blk):
            row = score[ii:ii + 1, :]
            beats = (row > score) | ((row == score) & (j_io > ii))
            rank = rank + beats.astype(jnp.int32)
        keep = (rank < min(SEL_TOP_N, n_blk)) & (score >= 0.0)

        def augment(kept):
            cols = jnp.concatenate([jnp.where(kept, 0.0, NEG_INF), jnp.full((8, LANES), NEG_INF, F32),
                                    jnp.zeros((LANES - n_blk - 8, LANES), F32)], axis=0)
            cols_t = cols.T[:, 0:HD].astype(BF16)
            return jnp.concatenate([q2, jnp.concatenate([cols_t] * Q_PER_KV, axis=0)], axis=1)

        qf_ref[which] = augment(keep & (j_io < 2 * (i - 1)))
        return augment(keep)

    def softmax_av(tiles, vt):
        s_all = jnp.concatenate(tiles, axis=0)
        m = jnp.max(s_all, axis=0, keepdims=True)
        p = jnp.exp2(s_all - m)
        return m, jnp.dot(vt, p.astype(BF16), preferred_element_type=F32)

    def normalise(av):
        return av[0:HD] / av[HD:HD + 1]

    def logit_tiles(k_ref, qq, row0, n):
        s_all = lax.dot_general(k_ref[pl.ds(row0, n * LANES), :], qq, _NT, preferred_element_type=F32)
        return [s_all[t * LANES:(t + 1) * LANES] for t in range(n)]

    def window_logits(i, q2):
        r0 = pl.multiple_of((i + pad_tiles + 1 - n_win) * LANES, LANES)
        tw = logit_tiles(kwa_ref, jnp.concatenate([q2, pad_cols], axis=1), r0, n_win)
        tw[0] = tw[0] + tt_ref[2]
        tw[n_win - 2] = tw[n_win - 2] + tt_ref[0]
        tw[n_win - 1] = tw[n_win - 1] + tt_ref[1]
        return tw

    def window_softmax(i, tw):
        r0 = pl.multiple_of((i + pad_tiles + 1 - n_win) * LANES, LANES)
        return normalise(softmax_av(tw, vwt_ref[:, pl.ds(r0, n_win * LANES)])[1])

    def near_logits(i, q_near):
        r1 = pl.multiple_of((i + pad_tiles - 1) * LANES, LANES)
        tn = logit_tiles(ksa_ref, q_near, r1, 2)
        tn[0] = tn[0] + tt_ref[0]
        tn[1] = tn[1] + tt_ref[1]
        return tn

    def near_softmax(i, tn):
        r1 = pl.multiple_of((i + pad_tiles - 1) * LANES, LANES)
        return softmax_av(tn, vst_ref[:, pl.ds(r1, 2 * LANES)])

    i_a = p_id
    i_b = n_qb - 1 - p_id
    oc_a, pc_a, q2_a = compressed(qa_ref, bca_ref)
    oc_b, pc_b, q2_b = compressed(qb_ref, bcb_ref)
    tw_a = window_logits(i_a, q2_a)
    tw_b = window_logits(i_b, q2_b)
    qn_a = select(i_a, pc_a, q2_a, 0)
    ow_a = window_softmax(i_a, tw_a)
    qn_b = select(i_b, pc_b, q2_b, 1)
    ow_b = window_softmax(i_b, tw_b)
    tn_a = near_logits(i_a, qn_a)
    tn_b = near_logits(i_b, qn_b)
    mn_a, avn_a = near_softmax(i_a, tn_a)
    mn_b, avn_b = near_softmax(i_b, tn_b)

    n_a = (jnp.maximum(i_a - 1, 0) + FAR_TILES - 1) // FAR_TILES
    is_a, rows, mx = [], [], []
    for k in range(max_chunks):
        own = k < n_a
        c = jnp.where(own, k, k - n_a)
        r = pl.multiple_of((c * FAR_TILES + pad_tiles) * LANES, LANES)
        s_k = lax.dot_general(ksa_ref[pl.ds(r, FAR_TILES * LANES), :], qf_ref[jnp.where(own, 0, 1)], _NT,
                              preferred_element_type=F32)
        sbuf_ref[k] = s_k
        is_a.append(own)
        rows.append(r)
        mx.append(jnp.max(s_k, axis=0, keepdims=True))
    m_a, m_b = mn_a, mn_b
    for k in range(max_chunks):
        m_a = jnp.where(is_a[k], jnp.maximum(m_a, mx[k]), m_a)
        m_b = jnp.where(is_a[k], m_b, jnp.maximum(m_b, mx[k]))
    av_a = jnp.exp2(mn_a - m_a) * avn_a
    av_b = jnp.exp2(mn_b - m_b) * avn_b
    for k in range(max_chunks):
        p = jnp.exp2(sbuf_ref[k] - jnp.where(is_a[k], m_a, m_b))
        pv = jnp.dot(vst_ref[:, pl.ds(rows[k], FAR_TILES * LANES)], p.astype(BF16), preferred_element_type=F32)
        av_a = av_a + jnp.where(is_a[k], pv, 0.0)
        av_b = av_b + jnp.where(is_a[k], 0.0, pv)

    def finish(half, gt_ref, o_c, o_s, o_w):
        def gate(branch):
            rows = gt_ref[pl.ds(branch * N_HEADS + pl.program_id(1) * Q_PER_KV, Q_PER_KV), :]
            return jnp.concatenate([rows[r:r + 1] for r in range(Q_PER_KV)], axis=1)

        o_t = gate(0) * o_c + gate(1) * o_s + gate(2) * o_w
        for pair in range(Q_PER_KV // 2):
            blk = jnp.concatenate([o_t[:, (2 * pair) * LANES:(2 * pair + 1) * LANES],
                                   o_t[:, (2 * pair + 1) * LANES:(2 * pair + 2) * LANES]], axis=0)
            o_ref[half, :, pair * LANES:(pair + 1) * LANES] = blk.T.astype(o_ref.dtype)

    finish(0, gta_ref, oc_a, normalise(av_a), ow_a)
    finish(1, gtb_ref, oc_b, normalise(av_b), ow_b)


def _nsa(proj, kc, vct, ksa, vst, kwa, vwt, gt, tt, bc, ovt, s_):
    b_, g_ = kwa.shape[:2]
    n_qb = s_ // Q_BLOCK
    n_p = n_qb // 2
    m = proj.shape[0]
    nl = Q_PER_KV * LANES
    qcol = SEG_Q * D_MODEL // (Q_PER_KV * HD)
    def far_chunks(i):
        return (max(i - 1, 0) + FAR_TILES - 1) // FAR_TILES

    max_chunks = far_chunks(0) + far_chunks(n_qb - 1)
    assert n_qb % 2 == 0 and all(far_chunks(p) + far_chunks(n_qb - 1 - p) == max_chunks for p in range(n_p))

    def bg(shape):
        return pl.BlockSpec((None, None) + shape, lambda b, g, p: (b, g, 0, 0))

    def blk_a(p):
        return p

    def blk_b(p):
        return n_qb - 1 - p

    def per_block(which):
        return [
            pl.BlockSpec((Q_BLOCK, Q_PER_KV * HD), lambda b, g, p: (b * n_qb + which(p), qcol + g)),
            pl.BlockSpec((None, LANES, Q_BLOCK), lambda b, g, p: (b, 0, which(p))),
            pl.BlockSpec((None, None, LANES, nl), lambda b, g, p: (g, which(p), 0, 0)),
        ]

    qa, gta, bca = per_block(blk_a)
    qb, gtb, bcb = per_block(blk_b)
    out = pl.pallas_call(
        _nsa_body,
        grid=(b_, g_, n_p),
        in_specs=[
            qa, qb,
            bg((LANES, HD)), bg((HD, LANES)),
            bg((s_ + KV_PAD, 2 * HD)), bg((HD + ONES_ROWS, s_ + KV_PAD)),
            bg((s_ + KV_PAD, 2 * HD)), bg((HD + ONES_ROWS, s_ + KV_PAD)),
            gta, gtb,
            pl.BlockSpec((None, 3, LANES, nl), lambda b, g, p: (g, 0, 0, 0)),
            bca, bcb,
            pl.BlockSpec((s_ // SEL_BLOCK, LANES), lambda b, g, p: (0, 0)),
        ],
        out_specs=pl.BlockSpec((2, None, Q_BLOCK, Q_PER_KV * HD), lambda b, g, p: (0, b * n_p + p, 0, g)),
        out_shape=jax.ShapeDtypeStruct((2, b_ * n_p, Q_BLOCK, D_MODEL), BF16),
        scratch_shapes=[pltpu.VMEM((max_chunks, FAR_TILES * LANES, nl), F32), pltpu.VMEM((2, nl, 2 * HD), BF16)],
        compiler_params=_params(("parallel", "parallel", "arbitrary")),
        name="nsa",
    )(proj, proj, kc, vct, ksa, vst, kwa, vwt, gt, gt, tt, bc, bc, ovt)
    return out.reshape(m, D_MODEL)


def _attn_block_row(b, qb, n_b, n_qb):
    n_p = n_qb // 2
    upper = qb >= n_p
    return jnp.where(upper, n_b * n_p, 0) + b * n_p + jnp.where(upper, n_qb - 1 - qb, qb)


def _mix_body(gb_ref, gc_ref, va_ref, gch_ref, vah_ref, uc_ref, vc_ref, g0_ref, g1_ref, g2_ref, x_ref,
              ca_ref, sw_ref, sb_ref, lng_ref, lnb_ref, wa_ref, wb_ref, wc_ref, wo_ref, *rest):
    tm = gb_ref.shape[0]
    ob_refs = rest[:tm // Q_BLOCK]
    o_ref, xa_ref, mx_ref = rest[tm // Q_BLOCK:]
    first = pl.program_id(1) == 0

    halo = gch_ref[...].astype(F32) * vah_ref[...].astype(F32)
    xa_ref[0:HALO, :] = jnp.where(first, 0.0, halo)
    xa_ref[HALO:, :] = gc_ref[...].astype(F32) * va_ref[...].astype(F32)
    ca = ca_ref[...]
    row = lax.broadcasted_iota(jnp.int32, (CHUNK, CHUNK), 0)
    col = lax.broadcasted_iota(jnp.int32, (CHUNK, CHUNK), 1)
    sw = [jnp.where(row >= col, sw_ref[gi], 0.0).astype(BF16) for gi in range(SGU_GROUPS)]

    for r in range(0, tm, ROW_CHUNK):
        rows = slice(r, r + ROW_CHUNK)
        conv = (ca[0:1] * xa_ref[pl.ds(r + HALO - 2, ROW_CHUNK), :]
                + ca[1:2] * xa_ref[pl.ds(r + HALO - 1, ROW_CHUNK), :]
                + ca[2:3] * xa_ref[pl.ds(r + HALO, ROW_CHUNK), :])
        out_a = gb_ref[rows, :].astype(F32) * conv
        y_a = jnp.dot(out_a.astype(BF16), wa_ref[...], preferred_element_type=F32)

        v = vc_ref[rows, :].astype(F32)
        mu = jnp.mean(v, axis=-1, keepdims=True)
        vz = v - mu
        var = jnp.mean(vz * vz, axis=-1, keepdims=True)
        vn = (vz * lax.rsqrt(var + EPS) * lng_ref[...] + lnb_ref[...]).astype(BF16)
        for gi in range(SGU_GROUPS):
            cols = slice(gi * LANES, (gi + 1) * LANES)
            for c in range(0, ROW_CHUNK, CHUNK):
                mx_ref[r + c:r + c + CHUNK, cols] = (
                    jnp.dot(sw[gi], vn[c:c + CHUNK, cols], preferred_element_type=F32) + sb_ref[:, cols])
        out_c = uc_ref[rows, :].astype(F32) * mx_ref[rows, :]
        y_c = jnp.dot(out_c.astype(BF16), wc_ref[...], preferred_element_type=F32)

        out_b = jnp.concatenate([ob_refs[k][...] for k in range(r // Q_BLOCK, (r + ROW_CHUNK) // Q_BLOCK)], axis=0)
        y_b = jnp.dot(out_b, wb_ref[...], preferred_element_type=F32)

        merged = (g0_ref[rows, :].astype(F32) * y_a + g1_ref[rows, :].astype(F32) * y_b
                  + g2_ref[rows, :].astype(F32) * y_c)
        o_ref[rows, :] = x_ref[rows, :] + jnp.dot(merged.astype(BF16), wo_ref[...], preferred_element_type=F32)


def _mix(proj, out_b, x2, seq, conv_a, sgu_w, sgu_b2, ln_g, ln_b, w_a, w_b, w_c, w_o, tm=512):
    m = x2.shape[0]
    n_t = seq // tm
    hb = tm // HALO

    def seg(k):
        return pl.BlockSpec((tm, D_MODEL), lambda b, i: (b * n_t + i, k))

    def halo(k):
        return pl.BlockSpec((HALO, D_MODEL), lambda b, i: (jnp.maximum((b * n_t + i) * hb - 1, 0), k))

    def full(shape):
        return pl.BlockSpec(shape, lambda b, i: (0,) * len(shape))

    def attn(k):
        return pl.BlockSpec((Q_BLOCK, D_MODEL), lambda b, i: (
            _attn_block_row(b, i * (tm // Q_BLOCK) + k, m // seq, seq // Q_BLOCK), 0))

    row = pl.BlockSpec((tm, D_MODEL), lambda b, i: (b * n_t + i, 0))
    wspec = pl.BlockSpec((D_MODEL, D_MODEL), lambda b, i: (0, 0), pipeline_mode=pl.Buffered(1))
    return pl.pallas_call(
        _mix_body,
        grid=(m // seq, n_t),
        in_specs=[
            seg(SEG_GB), seg(SEG_GC), seg(SEG_VA), halo(SEG_GC), halo(SEG_VA), seg(SEG_U), seg(SEG_V),
            seg(SEG_G0), seg(SEG_G0 + 1), seg(SEG_G0 + 2), row,
            full((CONV_W, D_MODEL)), full((SGU_GROUPS, CHUNK, CHUNK)), full((CHUNK, D_MODEL)),
            full((1, D_MODEL)), full((1, D_MODEL)), wspec, wspec, wspec, wspec,
        ] + [attn(k) for k in range(tm // Q_BLOCK)],
        out_specs=row,
        out_shape=jax.ShapeDtypeStruct((m, D_MODEL), F32),
        scratch_shapes=[pltpu.VMEM((tm + HALO, D_MODEL), F32), pltpu.VMEM((tm, D_MODEL), F32)],
        compiler_params=_params(("parallel", "arbitrary")),
        name="mix",
    )(proj, proj, proj, proj, proj, proj, proj, proj, proj, proj, x2,
      conv_a, sgu_w, sgu_b2, ln_g, ln_b, w_a, w_b, w_c, w_o, *([out_b] * (tm // Q_BLOCK)))


def _ffn_body(x_ref, xh_ref, g_ref, wg_ref, wv_ref, cw_ref, wd_ref, fg_ref, o_ref, h_ref, gt_ref, acc_ref, *,
              tiles_per_seq, final):
    i = pl.program_id(0)
    j = pl.program_id(1)
    tm = x_ref.shape[0]

    @pl.when(j == 0)
    def _():
        h_ref[0:HALO, :] = _rms(xh_ref[...], g_ref[...]).astype(BF16)
        h_ref[HALO:, :] = _rms(x_ref[...], g_ref[...]).astype(BF16)
        acc_ref[...] = jnp.zeros_like(acc_ref)

    gate = jnp.dot(h_ref[...], wg_ref[...], preferred_element_type=F32)
    rows = lax.broadcasted_iota(jnp.int32, gate.shape, 0)
    seq_start = (i % tiles_per_seq) == 0
    gt_ref[...] = jnp.where(seq_start & (rows < HALO), 0.0, gate)
    cw = cw_ref[...]
    conv = (cw[0:1] * gt_ref[pl.ds(HALO - 2, tm), :] + cw[1:2] * gt_ref[pl.ds(HALO - 1, tm), :]
            + cw[2:3] * gt_ref[pl.ds(HALO, tm), :])
    val = jnp.dot(h_ref[HALO:, :], wv_ref[...], preferred_element_type=F32)
    act = (jax.nn.gelu(conv) * val).astype(BF16)
    acc_ref[...] += jnp.dot(act, wd_ref[...], preferred_element_type=F32)

    @pl.when(j == pl.num_programs(1) - 1)
    def _():
        y = x_ref[...] + acc_ref[...]
        o_ref[...] = _rms(y, fg_ref[...]) if final else y


def _ffn(x2, seq, g, w_up, conv_w, w_down, final_g, final, tm=1024, tf=512):
    m = x2.shape[0]
    n_f = D_FF // tf
    w_up = jnp.stack([w_up[:, k * tf:(k + 1) * tf] for k in range(2 * n_f)]).astype(BF16)
    hb = tm // HALO
    return pl.pallas_call(
        functools.partial(_ffn_body, tiles_per_seq=seq // tm, final=final),
        grid=(m // tm, n_f),
        in_specs=[
            pl.BlockSpec((tm, D_MODEL), lambda i, j: (i, 0)),
            pl.BlockSpec((HALO, D_MODEL), lambda i, j: (jnp.maximum(i * hb - 1, 0), 0)),
            pl.BlockSpec((1, D_MODEL), lambda i, j: (0, 0)),
            pl.BlockSpec((None, D_MODEL, tf), lambda i, j: (j, 0, 0)),
            pl.BlockSpec((None, D_MODEL, tf), lambda i, j: (n_f + j, 0, 0)),
            pl.BlockSpec((CONV_W, tf), lambda i, j: (0, j)),
            pl.BlockSpec((tf, D_MODEL), lambda i, j: (j, 0)),
            pl.BlockSpec((1, D_MODEL), lambda i, j: (0, 0)),
        ],
        out_specs=pl.BlockSpec((tm, D_MODEL), lambda i, j: (i, 0)),
        out_shape=jax.ShapeDtypeStruct((m, D_MODEL), F32),
        scratch_shapes=[
            pltpu.VMEM((tm + HALO, D_MODEL), BF16),
            pltpu.VMEM((tm + HALO, tf), F32),
            pltpu.VMEM((tm, D_MODEL), F32),
        ],
        compiler_params=_params(("parallel", "arbitrary")),
        name="ffn",
    )(x2, x2, g, w_up, w_up, conv_w, w_down, final_g)


def _split_w_in(w_in):
    c = [0]
    for sz in [D_MODEL] * 4 + [N_KV * HD] * 6 + [3 * N_HEADS] + [D_MODEL] * 5:
        c.append(c[-1] + sz)
    starts = c[0:4] + c[11:16]
    main = jnp.stack([w_in[:, a:a + D_MODEL] for a in starts]).astype(BF16)
    kv = w_in[:, c[4]:c[10]].reshape(D_MODEL, 6, N_KV, HD)[:, jnp.array([2, 3, 4, 5, 0, 1])]
    kv = kv.transpose(0, 2, 1, 3).reshape(D_MODEL, N_KVCOL)
    gate = jnp.pad(w_in[:, c[10]:c[11]], ((0, 0), (0, LANES - 3 * N_HEADS)))
    return main, kv.astype(BF16), gate.astype(BF16)


def _mixer_layer(x2, b_, s_, tt, bc, ovt, norm_g, w_in, conv_a, cmp_pos, cmp_w1, cmp_w2, sgu_w, sgu_b,
                 sgu_norm_g, sgu_norm_b, w_br_a, w_br_b, w_br_c, w_o):
    w_main, w_kv, w_gate = _split_w_in(w_in)
    norm_g = norm_g.reshape(1, D_MODEL)
    proj = _in_proj(x2, norm_g, w_main)
    kc, vc, ksa, vst, kwa, vwt, gt = _kv_proj(x2, b_, s_, norm_g, w_kv, w_gate)

    k_cmp, v_cmp = _compress(kc, vc, cmp_pos.reshape(2, 1, CMP_LEN * HD),
                             cmp_w1.astype(BF16), cmp_w2.astype(BF16))
    out_b = _nsa(proj, k_cmp, v_cmp.transpose(0, 1, 3, 2), ksa, vst, kwa, vwt, gt, tt, bc, ovt, s_)

    sgu_b2 = jnp.broadcast_to(sgu_b.T[:, :, None], (CHUNK, SGU_GROUPS, D_MODEL // SGU_GROUPS)).reshape(CHUNK, D_MODEL)
    return _mix(proj, out_b, x2, s_, conv_a, sgu_w, sgu_b2,
                sgu_norm_g.reshape(1, D_MODEL), sgu_norm_b.reshape(1, D_MODEL),
                w_br_a.astype(BF16), w_br_b.astype(BF16), w_br_c.astype(BF16), w_o.astype(BF16))


def _overlap_t(s_):
    n_blk = s_ // SEL_BLOCK
    cmp_start = jnp.arange(LANES) * CMP_STRIDE
    cmp_end = cmp_start + CMP_LEN - 1
    blk_start = jnp.arange(n_blk) * SEL_BLOCK
    n_cmp = (s_ - CMP_LEN) // CMP_STRIDE + 1
    ov = ((cmp_start[None, :] < blk_start[:, None] + SEL_BLOCK) & (cmp_end[None, :] >= blk_start[:, None])
          & (jnp.arange(LANES)[None, :] < n_cmp))
    return ov.astype(F32)


def kernel(x, rel_bias, norm_mix, w_in, conv_a, cmp_pos, cmp_w1, cmp_w2, sgu_w, sgu_b, sgu_norm_g, sgu_norm_b,
           w_br_a, w_br_b, w_br_c, w_o, norm_ffn, ffn_w_up, ffn_conv, ffn_w_down, norm_final):
    b_, s_, _ = x.shape
    depth = w_in.shape[0]
    x2 = x.reshape(b_ * s_, D_MODEL)
    tt, bc = _bias_tiles(rel_bias, s_ // Q_BLOCK)
    ovt = _overlap_t(s_)
    for l in range(depth):
        x2 = _mixer_layer(x2, b_, s_, tt, bc, ovt, norm_mix[l], w_in[l], conv_a[l], cmp_pos[l], cmp_w1[l],
                          cmp_w2[l], sgu_w[l], sgu_b[l], sgu_norm_g[l], sgu_norm_b[l],
                          w_br_a[l], w_br_b[l], w_br_c[l], w_o[l])
        x2 = _ffn(x2, s_, norm_ffn[l].reshape(1, D_MODEL), ffn_w_up[l], ffn_conv[l], ffn_w_down[l].astype(BF16),
                  norm_final.reshape(1, D_MODEL), final=(l == depth - 1))
    return x2.reshape(b_, s_, D_MODEL)
```

```python
import functools
import math

import jax
import jax.numpy as jnp
from jax import lax
from jax.experimental import pallas as pl
from jax.experimental.pallas import tpu as pltpu

D_MODEL = 1024
HD = 64
N_HEADS = D_MODEL // HD
N_KV = 4
Q_PER_KV = N_HEADS // N_KV
SGU_GROUPS = 8
CHUNK = 128
CONV_W = 3
CMP_LEN = 32
CMP_STRIDE = 16
CMP_HID = 2 * HD
SEL_BLOCK = 64
SEL_TOP_N = 8
WINDOW = 512
Q_BLOCK = 128
D_FF = 3 * D_MODEL
N_BUCKETS = 32
MAX_EXACT = N_BUCKETS // 2
REL_MAX_DIST = 128
SCALE = HD ** -0.5
EPS = 1e-6
NEG_INF = -1e30
FORCE = 1e9
LOG2E = 1.4426950408889634

LANES = 128
ROW_CHUNK = 256
HALO = 16
N_SEG = 9
SEG_GB, SEG_GC, SEG_VA, SEG_Q, SEG_U, SEG_V, SEG_G0 = 0, 1, 2, 3, 4, 5, 6
N_MAIN = N_SEG * D_MODEL
N_KVCOL = 6 * N_KV * HD
KV_PAD = WINDOW
VMEM_LIMIT = 56 * 1024 * 1024

BF16 = jnp.bfloat16
F32 = jnp.float32
_NT = (((1,), (1,)), ((), ()))


def _params(sem):
    return pltpu.CompilerParams(dimension_semantics=sem, vmem_limit_bytes=VMEM_LIMIT)


def _rms(x, g):
    return x * lax.rsqrt(jnp.mean(x * x, axis=-1, keepdims=True) + EPS) * g


def _in_proj_body(x_ref, g_ref, w_ref, o_ref, h_ref, *, gelu_tiles, sigmoid_tiles):
    j = pl.program_id(1)

    @pl.when(j == 0)
    def _():
        h_ref[...] = _rms(x_ref[...], g_ref[...]).astype(BF16)

    def tile(act):
        for r in range(0, h_ref.shape[0], ROW_CHUNK):
            y = jnp.dot(h_ref[r:r + ROW_CHUNK, :], w_ref[...], preferred_element_type=F32)
            o_ref[r:r + ROW_CHUNK, :] = act(y).astype(o_ref.dtype)

    is_gelu = (j >= gelu_tiles[0]) & (j < gelu_tiles[1])
    is_sigmoid = (j >= sigmoid_tiles[0]) & (j < sigmoid_tiles[1])
    pl.when(is_gelu)(lambda: tile(jax.nn.gelu))
    pl.when(is_sigmoid)(lambda: tile(jax.nn.sigmoid))
    pl.when(jnp.logical_not(is_gelu | is_sigmoid))(lambda: tile(lambda y: y))


def _in_proj(x2, g, w_main, tm=2048):
    m = x2.shape[0]
    tn = D_MODEL
    body = functools.partial(_in_proj_body, gelu_tiles=(SEG_U, SEG_V + 1), sigmoid_tiles=(SEG_G0, SEG_G0 + 3))
    return pl.pallas_call(
        body,
        grid=(m // tm, N_SEG),
        in_specs=[
            pl.BlockSpec((tm, D_MODEL), lambda i, j: (i, 0)),
            pl.BlockSpec((1, D_MODEL), lambda i, j: (0, 0)),
            pl.BlockSpec((None, D_MODEL, tn), lambda i, j: (j, 0, 0)),
        ],
        out_specs=pl.BlockSpec((tm, tn), lambda i, j: (i, j)),
        out_shape=jax.ShapeDtypeStruct((m, N_MAIN), BF16),
        scratch_shapes=[pltpu.VMEM((tm, D_MODEL), BF16)],
        compiler_params=_params(("parallel", "arbitrary")),
        name="in_proj",
    )(x2, g, w_main)


def _kv_proj_body(x_ref, g_ref, w_ref, wg_ref, kc_ref, vc_ref, ksa_ref, vst_ref, kwa_ref, vwt_ref, gt_ref, cmp_ref):
    s_id = pl.program_id(1)
    tm = x_ref.shape[0]
    lane = lax.broadcasted_iota(jnp.int32, (tm, LANES), 1)
    ones_rows = (lax.broadcasted_iota(jnp.int32, (ONES_ROWS, tm), 0) == 0).astype(BF16)

    @pl.when(s_id == 0)
    def _():
        pad_keys = jnp.where(lane == HD + PAD_COL, 1.0, 0.0).astype(BF16)
        for g in range(N_KV):
            ksa_ref[g] = pad_keys
            kwa_ref[g] = pad_keys
            vst_ref[g] = jnp.zeros(vst_ref.shape[1:], BF16)
            vwt_ref[g] = jnp.zeros(vwt_ref.shape[1:], BF16)

    @pl.when(s_id > 0)
    def _():
        h = _rms(x_ref[...], g_ref[...]).astype(BF16)
        y = jnp.dot(h, w_ref[...], preferred_element_type=F32)
        tok = (s_id - 1) * tm + lax.broadcasted_iota(jnp.int32, (tm, LANES), 0)
        blk_id = jnp.where(lane - HD == tok // SEL_BLOCK, 1.0, 0.0)
        for g in range(N_KV):
            base = g * 6 * HD
            sel = y[:, base:base + 2 * HD]
            win = y[:, base + 2 * HD:base + 4 * HD]
            cmp = y[:, base + 4 * HD:base + 6 * HD]
            ksa_ref[g] = jnp.where(lane < HD, sel, blk_id).astype(BF16)
            kwa_ref[g] = jnp.where(lane < HD, win, 0.0).astype(BF16)
            vst_ref[g, 0:HD, :] = sel.T[HD:].astype(BF16)
            vst_ref[g, HD:, :] = ones_rows
            vwt_ref[g, 0:HD, :] = win.T[HD:].astype(BF16)
            vwt_ref[g, HD:, :] = ones_rows
            cmp_ref[...] = cmp
            for c in range(CMP_STRIDE):
                every = cmp_ref[pl.ds(c, tm // CMP_STRIDE, stride=CMP_STRIDE), :].astype(BF16)
                kc_ref[g, :, c * HD:(c + 1) * HD] = every[:, 0:HD]
                vc_ref[g, :, c * HD:(c + 1) * HD] = every[:, HD:]
        gt_ref[...] = jax.nn.sigmoid(jnp.dot(h, wg_ref[...], preferred_element_type=F32)).T


def _kv_proj(x2, b_, s_, g, w_kv, w_gate, tm=512):
    n_t = s_ // tm
    assert KV_PAD == tm

    def tok(b, s):
        return jnp.maximum(s - 1, 0)

    return pl.pallas_call(
        _kv_proj_body,
        grid=(b_, n_t + 1),
        in_specs=[
            pl.BlockSpec((tm, D_MODEL), lambda b, s: (b * n_t + tok(b, s), 0)),
            pl.BlockSpec((1, D_MODEL), lambda b, s: (0, 0)),
            pl.BlockSpec((D_MODEL, N_KVCOL), lambda b, s: (0, 0), pipeline_mode=pl.Buffered(1)),
            pl.BlockSpec((D_MODEL, LANES), lambda b, s: (0, 0), pipeline_mode=pl.Buffered(1)),
        ],
        out_specs=[
            pl.BlockSpec((None, N_KV, tm // CMP_STRIDE, CMP_STRIDE * HD), lambda b, s: (b, 0, tok(b, s), 0)),
            pl.BlockSpec((None, N_KV, tm // CMP_STRIDE, CMP_STRIDE * HD), lambda b, s: (b, 0, tok(b, s), 0)),
            pl.BlockSpec((None, N_KV, tm, 2 * HD), lambda b, s: (b, 0, s, 0)),
            pl.BlockSpec((None, N_KV, HD + ONES_ROWS, tm), lambda b, s: (b, 0, 0, s)),
            pl.BlockSpec((None, N_KV, tm, 2 * HD), lambda b, s: (b, 0, s, 0)),
            pl.BlockSpec((None, N_KV, HD + ONES_ROWS, tm), lambda b, s: (b, 0, 0, s)),
            pl.BlockSpec((None, LANES, tm), lambda b, s: (b, 0, tok(b, s))),
        ],
        out_shape=[
            jax.ShapeDtypeStruct((b_, N_KV, s_ // CMP_STRIDE, CMP_STRIDE * HD), BF16),
            jax.ShapeDtypeStruct((b_, N_KV, s_ // CMP_STRIDE, CMP_STRIDE * HD), BF16),
            jax.ShapeDtypeStruct((b_, N_KV, KV_PAD + s_, 2 * HD), BF16),
            jax.ShapeDtypeStruct((b_, N_KV, HD + ONES_ROWS, KV_PAD + s_), BF16),
            jax.ShapeDtypeStruct((b_, N_KV, KV_PAD + s_, 2 * HD), BF16),
            jax.ShapeDtypeStruct((b_, N_KV, HD + ONES_ROWS, KV_PAD + s_), BF16),
            jax.ShapeDtypeStruct((b_, LANES, s_), F32),
        ],
        scratch_shapes=[pltpu.VMEM((tm, 2 * HD), F32)],
        compiler_params=_params(("parallel", "arbitrary")),
        name="kv_proj",
    )(x2, g, w_kv, w_gate)


def _rel_bucket(dist):
    dist = jnp.maximum(dist, 0)
    log_ratio = jnp.log(jnp.maximum(dist, 1).astype(F32) / MAX_EXACT) / math.log(REL_MAX_DIST / MAX_EXACT)
    large = MAX_EXACT + (log_ratio * (N_BUCKETS - MAX_EXACT)).astype(jnp.int32)
    return jnp.where(dist < MAX_EXACT, dist, jnp.minimum(large, N_BUCKETS - 1))


def _bias_body(tab_ref, bt_ref, bc_ref, tt_ref, bco_ref, *, n_cmp):
    h = pl.program_id(0)

    def lookup(bk):
        acc = jnp.zeros(bk.shape, F32)
        for b in range(N_BUCKETS):
            acc = jnp.where(bk == b, tab_ref[b, h], acc)
        return acc

    kk = lax.broadcasted_iota(jnp.int32, (LANES, LANES), 0)
    qq = lax.broadcasted_iota(jnp.int32, (LANES, LANES), 1)
    far = lookup(bt_ref[2])
    tt_ref[0] = (lookup(bt_ref[1]) - far) * LOG2E
    tt_ref[1] = jnp.where(qq >= kk, (lookup(bt_ref[0]) - far) * LOG2E, NEG_INF)
    tt_ref[2] = jnp.where(kk > qq, 0.0, NEG_INF)
    for i in range(bc_ref.shape[0]):
        ok = (i * Q_BLOCK + qq - (kk * CMP_STRIDE + CMP_LEN - 1) >= 0) & (kk < n_cmp)
        bco_ref[i] = jnp.where(ok, lookup(bc_ref[i]), NEG_INF)


def _bias_tiles(rel_bias, n_qb):
    kk = jnp.arange(LANES)[:, None]
    qq = jnp.arange(LANES)[None, :]
    bt = jnp.stack([_rel_bucket(LANES * d + qq - kk) for d in range(3)]).astype(jnp.int32)
    cmp_end = CMP_STRIDE * jnp.arange(LANES) + CMP_LEN - 1
    t = (Q_BLOCK * jnp.arange(n_qb))[:, None, None] + qq[None]
    bc = _rel_bucket(t - cmp_end[None, :, None]).astype(jnp.int32)
    n_cmp = (n_qb * Q_BLOCK - CMP_LEN) // CMP_STRIDE + 1
    return pl.pallas_call(
        functools.partial(_bias_body, n_cmp=n_cmp),
        grid=(N_HEADS,),
        in_specs=[
            pl.BlockSpec(memory_space=pltpu.SMEM),
            pl.BlockSpec((3, LANES, LANES), lambda h: (0, 0, 0)),
            pl.BlockSpec((n_qb, LANES, LANES), lambda h: (0, 0, 0)),
        ],
        out_specs=[
            pl.BlockSpec((None, 3, LANES, LANES), lambda h: (h // Q_PER_KV, 0, 0, h % Q_PER_KV)),
            pl.BlockSpec((None, n_qb, LANES, LANES), lambda h: (h // Q_PER_KV, 0, 0, h % Q_PER_KV)),
        ],
        out_shape=[
            jax.ShapeDtypeStruct((N_KV, 3, LANES, Q_PER_KV * LANES), F32),
            jax.ShapeDtypeStruct((N_KV, n_qb, LANES, Q_PER_KV * LANES), F32),
        ],
        compiler_params=_params(("arbitrary",)),
        name="bias_tiles",
    )(rel_bias.astype(F32), bt, bc)


def _compress_body(k3_ref, v3_ref, pos_ref, w1_ref, w2_ref, ko_ref, vo_ref):
    half = CMP_STRIDE * HD
    for which, (src, dst) in enumerate(((k3_ref, ko_ref), (v3_ref, vo_ref))):
        x = src[...].astype(F32)
        pos = pos_ref[which]
        xa = (x + pos[:, :half]).astype(BF16)
        xb = (x + pos[:, half:]).astype(BF16)
        a = jnp.dot(xa, w1_ref[which, :half, :], preferred_element_type=F32)
        b = jnp.dot(xb, w1_ref[which, half:, :], preferred_element_type=F32)
        hid = jax.nn.gelu(a + pltpu.roll(b, LANES - 1, 0))
        dst[...] = jnp.dot(hid.astype(BF16), w2_ref[which], preferred_element_type=F32).astype(dst.dtype)


def _compress(k3, v3, pos, w1, w2):
    b_, g_, nm, width = k3.shape
    spec3 = pl.BlockSpec((None, None, nm, width), lambda b, g: (b, g, 0, 0))
    ospec = pl.BlockSpec((None, None, nm, HD), lambda b, g: (b, g, 0, 0))
    return pl.pallas_call(
        _compress_body,
        grid=(b_, g_),
        in_specs=[
            spec3, spec3,
            pl.BlockSpec((2, 1, CMP_LEN * HD), lambda b, g: (0, 0, 0)),
            pl.BlockSpec((2, CMP_LEN * HD, CMP_HID), lambda b, g: (0, 0, 0)),
            pl.BlockSpec((2, CMP_HID, HD), lambda b, g: (0, 0, 0)),
        ],
        out_specs=[ospec, ospec],
        out_shape=[jax.ShapeDtypeStruct((b_, g_, nm, HD), BF16)] * 2,
        compiler_params=_params(("parallel", "parallel")),
        name="compress",
    )(k3, v3, pos, w1, w2)


FAR_TILES = 2
PAD_COL = 32
ONES_ROWS = 16


def _nsa_body(qa_ref, qb_ref, kc_ref, vct_ref, ksa_ref, vst_ref, kwa_ref, vwt_ref, gta_ref, gtb_ref,
              tt_ref, bca_ref, bcb_ref, ovt_ref, o_ref, sbuf_ref, qf_ref):
    p_id = pl.program_id(2)
    n_qb = 2 * pl.num_programs(2)
    nl = Q_PER_KV * LANES
    n_win = WINDOW // LANES + 1
    n_blk = ovt_ref.shape[0]
    max_chunks = sbuf_ref.shape[0]
    pad_tiles = KV_PAD // LANES
    j_io = lax.broadcasted_iota(jnp.int32, (n_blk, LANES), 0)
    q_io = lax.broadcasted_iota(jnp.int32, (n_blk, LANES), 1)
    col_w = lax.broadcasted_iota(jnp.int32, (nl, HD), 1)
    pad_cols = jnp.where(col_w == PAD_COL, NEG_INF, 0.0).astype(BF16)

    def compressed(q_ref, bc_ref):
        qt = q_ref[...]
        q = jnp.concatenate([qt[:, r * HD:(r + 1) * HD] for r in range(Q_PER_KV)], axis=0) * SCALE
        s = lax.dot_general(kc_ref[...], q, _NT, preferred_element_type=F32) + bc_ref[...]
        m_c = jnp.max(s, axis=0, keepdims=True)
        p = jnp.exp(s - m_c)
        l_c = jnp.sum(p, axis=0, keepdims=True)
        p_c = p * (jnp.where(m_c > 0.5 * NEG_INF, 1.0, 0.0) / l_c)
        o_c = jnp.dot(vct_ref[...], p_c.astype(BF16), preferred_element_type=F32)
        return o_c, p_c, (q.astype(F32) * LOG2E).astype(BF16)

    def select(i, p_c, q2, which):
        p_sum = p_c[:, 0:LANES]
        for r in range(1, Q_PER_KV):
            p_sum = p_sum + p_c[:, r * LANES:(r + 1) * LANES]
        imp = jnp.dot(ovt_ref[...], p_sum, precision=lax.Precision.HIGHEST, preferred_element_type=F32)
        t_b = i * Q_BLOCK + q_io
        cur = t_b >> 6
        forced = (j_io == 0) | (j_io == cur) | (j_io == cur - 1)
        valid = j_io * SEL_BLOCK <= t_b
        score = jnp.where(forced, FORCE, jnp.where(valid, imp, -FORCE))
        rank = jnp.zeros((n_blk, LANES), jnp.int32)
        for ii in range(n_blk):
            row = score[ii:ii + 1, :]
            beats = (row > score) | ((row == score) & (j_io > ii))
            rank = rank + beats.astype(jnp.int32)
        keep = (rank < min(SEL_TOP_N, n_blk)) & (score >= 0.0)

        def augment(kept):
            cols = jnp.concatenate([jnp.where(kept, 0.0, NEG_INF), jnp.full((8, LANES), NEG_INF, F32),
                                    jnp.zeros((LANES - n_blk - 8, LANES), F32)], axis=0)
            cols_t = cols.T[:, 0:HD].astype(BF16)
            return jnp.concatenate([q2, jnp.concatenate([cols_t] * Q_PER_KV, axis=0)], axis=1)

        qf_ref[which] = augment(keep & (j_io < 2 * (i - 1)))
        return augment(keep)

    def softmax_av(tiles, vt):
        s_all = jnp.concatenate(tiles, axis=0)
        m = jnp.max(s_all, axis=0, keepdims=True)
        p = jnp.exp2(s_all - m)
        return m, jnp.dot(vt, p.astype(BF16), preferred_element_type=F32)

    def normalise(av):
        return av[0:HD] / av[HD:HD + 1]

    def logit_tiles(k_ref, qq, row0, n):
        s_all = lax.dot_general(k_ref[pl.ds(row0, n * LANES), :], qq, _NT, preferred_element_type=F32)
        return [s_all[t * LANES:(t + 1) * LANES] for t in range(n)]

    def window_logits(i, q2):
        r0 = pl.multiple_of((i + pad_tiles + 1 - n_win) * LANES, LANES)
        tw = logit_tiles(kwa_ref, jnp.concatenate([q2, pad_cols], axis=1), r0, n_win)
        tw[0] = tw[0] + tt_ref[2]
        tw[n_win - 2] = tw[n_win - 2] + tt_ref[0]
        tw[n_win - 1] = tw[n_win - 1] + tt_ref[1]
        return tw

    def window_softmax(i, tw):
        r0 = pl.multiple_of((i + pad_tiles + 1 - n_win) * LANES, LANES)
        return normalise(softmax_av(tw, vwt_ref[:, pl.ds(r0, n_win * LANES)])[1])

    def near_logits(i, q_near):
        r1 = pl.multiple_of((i + pad_tiles - 1) * LANES, LANES)
        tn = logit_tiles(ksa_ref, q_near, r1, 2)
        tn[0] = tn[0] + tt_ref[0]
        tn[1] = tn[1] + tt_ref[1]
        return tn

    def near_softmax(i, tn):
        r1 = pl.multiple_of((i + pad_tiles - 1) * LANES, LANES)
        return softmax_av(tn, vst_ref[:, pl.ds(r1, 2 * LANES)])

    i_a = p_id
    i_b = n_qb - 1 - p_id
    oc_a, pc_a, q2_a = compressed(qa_ref, bca_ref)
    oc_b, pc_b, q2_b = compressed(qb_ref, bcb_ref)
    tw_a = window_logits(i_a, q2_a)
    tw_b = window_logits(i_b, q2_b)
    qn_a = select(i_a, pc_a, q2_a, 0)
    ow_a = window_softmax(i_a, tw_a)
    qn_b = select(i_b, pc_b, q2_b, 1)
    ow_b = window_softmax(i_b, tw_b)
    tn_a = near_logits(i_a, qn_a)
    tn_b = near_logits(i_b, qn_b)
    mn_a, avn_a = near_softmax(i_a, tn_a)
    mn_b, avn_b = near_softmax(i_b, tn_b)

    n_a = (jnp.maximum(i_a - 1, 0) + FAR_TILES - 1) // FAR_TILES
    is_a, rows, mx = [], [], []
    for k in range(max_chunks):
        own = k < n_a
        c = jnp.where(own, k, k - n_a)
        r = pl.multiple_of((c * FAR_TILES + pad_tiles) * LANES, LANES)
        s_k = lax.dot_general(ksa_ref[pl.ds(r, FAR_TILES * LANES), :], qf_ref[jnp.where(own, 0, 1)], _NT,
                              preferred_element_type=F32)
        sbuf_ref[k] = s_k
        is_a.append(own)
        rows.append(r)
        mx.append(jnp.max(s_k, axis=0, keepdims=True))
    m_a, m_b = mn_a, mn_b
    for k in range(max_chunks):
        m_a = jnp.where(is_a[k], jnp.maximum(m_a, mx[k]), m_a)
        m_b = jnp.where(is_a[k], m_b, jnp.maximum(m_b, mx[k]))
    av_a = jnp.exp2(mn_a - m_a) * avn_a
    av_b = jnp.exp2(mn_b - m_b) * avn_b
    for k in range(max_chunks):
        p = jnp.exp2(sbuf_ref[k] - jnp.where(is_a[k], m_a, m_b))
        pv = jnp.dot(vst_ref[:, pl.ds(rows[k], FAR_TILES * LANES)], p.astype(BF16), preferred_element_type=F32)
        av_a = av_a + jnp.where(is_a[k], pv, 0.0)
        av_b = av_b + jnp.where(is_a[k], 0.0, pv)

    def finish(half, gt_ref, o_c, o_s, o_w):
        def gate(branch):
            rows = gt_ref[pl.ds(branch * N_HEADS + pl.program_id(1) * Q_PER_KV, Q_PER_KV), :]
            return jnp.concatenate([rows[r:r + 1] for r in range(Q_PER_KV)], axis=1)

        o_t = gate(0) * o_c + gate(1) * o_s + gate(2) * o_w
        for pair in range(Q_PER_KV // 2):
            blk = jnp.concatenate([o_t[:, (2 * pair) * LANES:(2 * pair + 1) * LANES],
                                   o_t[:, (2 * pair + 1) * LANES:(2 * pair + 2) * LANES]], axis=0)
            o_ref[half, :, pair * LANES:(pair + 1) * LANES] = blk.T.astype(o_ref.dtype)

    finish(0, gta_ref, oc_a, normalise(av_a), ow_a)
    finish(1, gtb_ref, oc_b, normalise(av_b), ow_b)


def _nsa(proj, kc, vct, ksa, vst, kwa, vwt, gt, tt, bc, ovt, s_):
    b_, g_ = kwa.shape[:2]
    n_qb = s_ // Q_BLOCK
    n_p = n_qb // 2
    m = proj.shape[0]
    nl = Q_PER_KV * LANES
    qcol = SEG_Q * D_MODEL // (Q_PER_KV * HD)
    def far_chunks(i):
        return (max(i - 1, 0) + FAR_TILES - 1) // FAR_TILES

    max_chunks = far_chunks(0) + far_chunks(n_qb - 1)
    assert n_qb % 2 == 0 and all(far_chunks(p) + far_chunks(n_qb - 1 - p) == max_chunks for p in range(n_p))

    def bg(shape):
        return pl.BlockSpec((None, None) + shape, lambda b, g, p: (b, g, 0, 0))

    def blk_a(p):
        return p

    def blk_b(p):
        return n_qb - 1 - p

    def per_block(which):
        return [
            pl.BlockSpec((Q_BLOCK, Q_PER_KV * HD), lambda b, g, p: (b * n_qb + which(p), qcol + g)),
            pl.BlockSpec((None, LANES, Q_BLOCK), lambda b, g, p: (b, 0, which(p))),
            pl.BlockSpec((None, None, LANES, nl), lambda b, g, p: (g, which(p), 0, 0)),
        ]

    qa, gta, bca = per_block(blk_a)
    qb, gtb, bcb = per_block(blk_b)
    out = pl.pallas_call(
        _nsa_body,
        grid=(b_, g_, n_p),
        in_specs=[
            qa, qb,
            bg((LANES, HD)), bg((HD, LANES)),
            bg((s_ + KV_PAD, 2 * HD)), bg((HD + ONES_ROWS, s_ + KV_PAD)),
            bg((s_ + KV_PAD, 2 * HD)), bg((HD + ONES_ROWS, s_ + KV_PAD)),
            gta, gtb,
            pl.BlockSpec((None, 3, LANES, nl), lambda b, g, p: (g, 0, 0, 0)),
            bca, bcb,
            pl.BlockSpec((s_ // SEL_BLOCK, LANES), lambda b, g, p: (0, 0)),
        ],
        out_specs=pl.BlockSpec((2, None, Q_BLOCK, Q_PER_KV * HD), lambda b, g, p: (0, b * n_p + p, 0, g)),
        out_shape=jax.ShapeDtypeStruct((2, b_ * n_p, Q_BLOCK, D_MODEL), BF16),
        scratch_shapes=[pltpu.VMEM((max_chunks, FAR_TILES * LANES, nl), F32), pltpu.VMEM((2, nl, 2 * HD), BF16)],
        compiler_params=_params(("parallel", "parallel", "arbitrary")),
        name="nsa",
    )(proj, proj, kc, vct, ksa, vst, kwa, vwt, gt, gt, tt, bc, bc, ovt)
    return out.reshape(m, D_MODEL)


def _attn_block_row(b, qb, n_b, n_qb):
    n_p = n_qb // 2
    upper = qb >= n_p
    return jnp.where(upper, n_b * n_p, 0) + b * n_p + jnp.where(upper, n_qb - 1 - qb, qb)


def _mix_body(gb_ref, gc_ref, va_ref, gch_ref, vah_ref, uc_ref, vc_ref, g0_ref, g1_ref, g2_ref, x_ref,
              ca_ref, sw_ref, sb_ref, lng_ref, lnb_ref, wa_ref, wb_ref, wc_ref, wo_ref, *rest):
    tm = gb_ref.shape[0]
    ob_refs = rest[:tm // Q_BLOCK]
    o_ref, xa_ref, mx_ref = rest[tm // Q_BLOCK:]
    first = pl.program_id(1) == 0

    halo = gch_ref[...].astype(F32) * vah_ref[...].astype(F32)
    xa_ref[0:HALO, :] = jnp.where(first, 0.0, halo)
    xa_ref[HALO:, :] = gc_ref[...].astype(F32) * va_ref[...].astype(F32)
    ca = ca_ref[...]
    row = lax.broadcasted_iota(jnp.int32, (CHUNK, CHUNK), 0)
    col = lax.broadcasted_iota(jnp.int32, (CHUNK, CHUNK), 1)
    sw = [jnp.where(row >= col, sw_ref[gi], 0.0).astype(BF16) for gi in range(SGU_GROUPS)]

    for r in range(0, tm, ROW_CHUNK):
        rows = slice(r, r + ROW_CHUNK)
        conv = (ca[0:1] * xa_ref[pl.ds(r + HALO - 2, ROW_CHUNK), :]
                + ca[1:2] * xa_ref[pl.ds(r + HALO - 1, ROW_CHUNK), :]
                + ca[2:3] * xa_ref[pl.ds(r + HALO, ROW_CHUNK), :])
        out_a = gb_ref[rows, :].astype(F32) * conv
        y_a = jnp.dot(out_a.astype(BF16), wa_ref[...], preferred_element_type=F32)

        v = vc_ref[rows, :].astype(F32)
        mu = jnp.mean(v, axis=-1, keepdims=True)
        vz = v - mu
        var = jnp.mean(vz * vz, axis=-1, keepdims=True)
        vn = (vz * lax.rsqrt(var + EPS) * lng_ref[...] + lnb_ref[...]).astype(BF16)
        for gi in range(SGU_GROUPS):
            cols = slice(gi * LANES, (gi + 1) * LANES)
            for c in range(0, ROW_CHUNK, CHUNK):
                mx_ref[r + c:r + c + CHUNK, cols] = (
                    jnp.dot(sw[gi], vn[c:c + CHUNK, cols], preferred_element_type=F32) + sb_ref[:, cols])
        out_c = uc_ref[rows, :].astype(F32) * mx_ref[rows, :]
        y_c = jnp.dot(out_c.astype(BF16), wc_ref[...], preferred_element_type=F32)

        out_b = jnp.concatenate([ob_refs[k][...] for k in range(r // Q_BLOCK, (r + ROW_CHUNK) // Q_BLOCK)], axis=0)
        y_b = jnp.dot(out_b, wb_ref[...], preferred_element_type=F32)

        merged = (g0_ref[rows, :].astype(F32) * y_a + g1_ref[rows, :].astype(F32) * y_b
                  + g2_ref[rows, :].astype(F32) * y_c)
        o_ref[rows, :] = x_ref[rows, :] + jnp.dot(merged.astype(BF16), wo_ref[...], preferred_element_type=F32)


def _mix(proj, out_b, x2, seq, conv_a, sgu_w, sgu_b2, ln_g, ln_b, w_a, w_b, w_c, w_o, tm=512):
    m = x2.shape[0]
    n_t = seq // tm
    hb = tm // HALO

    def seg(k):
        return pl.BlockSpec((tm, D_MODEL), lambda b, i: (b * n_t + i, k))

    def halo(k):
        return pl.BlockSpec((HALO, D_MODEL), lambda b, i: (jnp.maximum((b * n_t + i) * hb - 1, 0), k))

    def full(shape):
        return pl.BlockSpec(shape, lambda b, i: (0,) * len(shape))

    def attn(k):
        return pl.BlockSpec((Q_BLOCK, D_MODEL), lambda b, i: (
            _attn_block_row(b, i * (tm // Q_BLOCK) + k, m // seq, seq // Q_BLOCK), 0))

    row = pl.BlockSpec((tm, D_MODEL), lambda b, i: (b * n_t + i, 0))
    wspec = pl.BlockSpec((D_MODEL, D_MODEL), lambda b, i: (0, 0), pipeline_mode=pl.Buffered(1))
    return pl.pallas_call(
        _mix_body,
        grid=(m // seq, n_t),
        in_specs=[
            seg(SEG_GB), seg(SEG_GC), seg(SEG_VA), halo(SEG_GC), halo(SEG_VA), seg(SEG_U), seg(SEG_V),
            seg(SEG_G0), seg(SEG_G0 + 1), seg(SEG_G0 + 2), row,
            full((CONV_W, D_MODEL)), full((SGU_GROUPS, CHUNK, CHUNK)), full((CHUNK, D_MODEL)),
            full((1, D_MODEL)), full((1, D_MODEL)), wspec, wspec, wspec, wspec,
        ] + [attn(k) for k in range(tm // Q_BLOCK)],
        out_specs=row,
        out_shape=jax.ShapeDtypeStruct((m, D_MODEL), F32),
        scratch_shapes=[pltpu.VMEM((tm + HALO, D_MODEL), F32), pltpu.VMEM((tm, D_MODEL), F32)],
        compiler_params=_params(("parallel", "arbitrary")),
        name="mix",
    )(proj, proj, proj, proj, proj, proj, proj, proj, proj, proj, x2,
      conv_a, sgu_w, sgu_b2, ln_g, ln_b, w_a, w_b, w_c, w_o, *([out_b] * (tm // Q_BLOCK)))


def _ffn_body(x_ref, xh_ref, g_ref, wg_ref, wv_ref, cw_ref, wd_ref, fg_ref, o_ref, h_ref, gt_ref, acc_ref, *,
              tiles_per_seq, final):
    i = pl.program_id(0)
    j = pl.program_id(1)
    tm = x_ref.shape[0]

    @pl.when(j == 0)
    def _():
        h_ref[0:HALO, :] = _rms(xh_ref[...], g_ref[...]).astype(BF16)
        h_ref[HALO:, :] = _rms(x_ref[...], g_ref[...]).astype(BF16)
        acc_ref[...] = jnp.zeros_like(acc_ref)

    gate = jnp.dot(h_ref[...], wg_ref[...], preferred_element_type=F32)
    rows = lax.broadcasted_iota(jnp.int32, gate.shape, 0)
    seq_start = (i % tiles_per_seq) == 0
    gt_ref[...] = jnp.where(seq_start & (rows < HALO), 0.0, gate)
    cw = cw_ref[...]
    conv = (cw[0:1] * gt_ref[pl.ds(HALO - 2, tm), :] + cw[1:2] * gt_ref[pl.ds(HALO - 1, tm), :]
            + cw[2:3] * gt_ref[pl.ds(HALO, tm), :])
    val = jnp.dot(h_ref[HALO:, :], wv_ref[...], preferred_element_type=F32)
    act = (jax.nn.gelu(conv) * val).astype(BF16)
    acc_ref[...] += jnp.dot(act, wd_ref[...], preferred_element_type=F32)

    @pl.when(j == pl.num_programs(1) - 1)
    def _():
        y = x_ref[...] + acc_ref[...]
        o_ref[...] = _rms(y, fg_ref[...]) if final else y


def _ffn(x2, seq, g, w_up, conv_w, w_down, final_g, final, tm=1024, tf=512):
    m = x2.shape[0]
    n_f = D_FF // tf
    w_up = jnp.stack([w_up[:, k * tf:(k + 1) * tf] for k in range(2 * n_f)]).astype(BF16)
    hb = tm // HALO
    return pl.pallas_call(
        functools.partial(_ffn_body, tiles_per_seq=seq // tm, final=final),
        grid=(m // tm, n_f),
        in_specs=[
            pl.BlockSpec((tm, D_MODEL), lambda i, j: (i, 0)),
            pl.BlockSpec((HALO, D_MODEL), lambda i, j: (jnp.maximum(i * hb - 1, 0), 0)),
            pl.BlockSpec((1, D_MODEL), lambda i, j: (0, 0)),
            pl.BlockSpec((None, D_MODEL, tf), lambda i, j: (j, 0, 0)),
            pl.BlockSpec((None, D_MODEL, tf), lambda i, j: (n_f + j, 0, 0)),
            pl.BlockSpec((CONV_W, tf), lambda i, j: (0, j)),
            pl.BlockSpec((tf, D_MODEL), lambda i, j: (j, 0)),
            pl.BlockSpec((1, D_MODEL), lambda i, j: (0, 0)),
        ],
        out_specs=pl.BlockSpec((tm, D_MODEL), lambda i, j: (i, 0)),
        out_shape=jax.ShapeDtypeStruct((m, D_MODEL), F32),
        scratch_shapes=[
            pltpu.VMEM((tm + HALO, D_MODEL), BF16),
            pltpu.VMEM((tm + HALO, tf), F32),
            pltpu.VMEM((tm, D_MODEL), F32),
        ],
        compiler_params=_params(("parallel", "arbitrary")),
        name="ffn",
    )(x2, x2, g, w_up, w_up, conv_w, w_down, final_g)


def _split_w_in(w_in):
    c = [0]
    for sz in [D_MODEL] * 4 + [N_KV * HD] * 6 + [3 * N_HEADS] + [D_MODEL] * 5:
        c.append(c[-1] + sz)
    starts = c[0:4] + c[11:16]
    main = jnp.stack([w_in[:, a:a + D_MODEL] for a in starts]).astype(BF16)
    kv = w_in[:, c[4]:c[10]].reshape(D_MODEL, 6, N_KV, HD)[:, jnp.array([2, 3, 4, 5, 0, 1])]
    kv = kv.transpose(0, 2, 1, 3).reshape(D_MODEL, N_KVCOL)
    gate = jnp.pad(w_in[:, c[10]:c[11]], ((0, 0), (0, LANES - 3 * N_HEADS)))
    return main, kv.astype(BF16), gate.astype(BF16)


def _mixer_layer(x2, b_, s_, tt, bc, ovt, norm_g, w_in, conv_a, cmp_pos, cmp_w1, cmp_w2, sgu_w, sgu_b,
                 sgu_norm_g, sgu_norm_b, w_br_a, w_br_b, w_br_c, w_o):
    w_main, w_kv, w_gate = _split_w_in(w_in)
    norm_g = norm_g.reshape(1, D_MODEL)
    proj = _in_proj(x2, norm_g, w_main)
    kc, vc, ksa, vst, kwa, vwt, gt = _kv_proj(x2, b_, s_, norm_g, w_kv, w_gate)

    k_cmp, v_cmp = _compress(kc, vc, cmp_pos.reshape(2, 1, CMP_LEN * HD),
                             cmp_w1.astype(BF16), cmp_w2.astype(BF16))
    out_b = _nsa(proj, k_cmp, v_cmp.transpose(0, 1, 3, 2), ksa, vst, kwa, vwt, gt, tt, bc, ovt, s_)

    sgu_b2 = jnp.broadcast_to(sgu_b.T[:, :, None], (CHUNK, SGU_GROUPS, D_MODEL // SGU_GROUPS)).reshape(CHUNK, D_MODEL)
    return _mix(proj, out_b, x2, s_, conv_a, sgu_w, sgu_b2,
                sgu_norm_g.reshape(1, D_MODEL), sgu_norm_b.reshape(1, D_MODEL),
                w_br_a.astype(BF16), w_br_b.astype(BF16), w_br_c.astype(BF16), w_o.astype(BF16))


def _overlap_t(s_):
    n_blk = s_ // SEL_BLOCK
    cmp_start = jnp.arange(LANES) * CMP_STRIDE
    cmp_end = cmp_start + CMP_LEN - 1
    blk_start = jnp.arange(n_blk) * SEL_BLOCK
    n_cmp = (s_ - CMP_LEN) // CMP_STRIDE + 1
    ov = ((cmp_start[None, :] < blk_start[:, None] + SEL_BLOCK) & (cmp_end[None, :] >= blk_start[:, None])
          & (jnp.arange(LANES)[None, :] < n_cmp))
    return ov.astype(F32)


def kernel(x, rel_bias, norm_mix, w_in, conv_a, cmp_pos, cmp_w1, cmp_w2, sgu_w, sgu_b, sgu_norm_g, sgu_norm_b,
           w_br_a, w_br_b, w_br_c, w_o, norm_ffn, ffn_w_up, ffn_conv, ffn_w_down, norm_final):
    b_, s_, _ = x.shape
    depth = w_in.shape[0]
    x2 = x.reshape(b_ * s_, D_MODEL)
    tt, bc = _bias_tiles(rel_bias, s_ // Q_BLOCK)
    ovt = _overlap_t(s_)
    for l in range(depth):
        x2 = _mixer_layer(x2, b_, s_, tt, bc, ovt, norm_mix[l], w_in[l], conv_a[l], cmp_pos[l], cmp_w1[l],
                          cmp_w2[l], sgu_w[l], sgu_b[l], sgu_norm_g[l], sgu_norm_b[l],
                          w_br_a[l], w_br_b[l], w_br_c[l], w_o[l])
        x2 = _ffn(x2, s_, norm_ffn[l].reshape(1, D_MODEL), ffn_w_up[l], ffn_conv[l], ffn_w_down[l].astype(BF16),
                  norm_final.reshape(1, D_MODEL), final=(l == depth - 1))
    return x2.reshape(b_, s_, D_MODEL)
```

```python
import functools
import math

import jax
import jax.numpy as jnp
from jax import lax
from jax.experimental import pallas as pl
from jax.experimental.pallas import tpu as pltpu

D_MODEL = 1024
HD = 64
N_HEADS = D_MODEL // HD
N_KV = 4
Q_PER_KV = N_HEADS // N_KV
SGU_GROUPS = 8
CHUNK = 128
CONV_W = 3
CMP_LEN = 32
CMP_STRIDE = 16
CMP_HID = 2 * HD
SEL_BLOCK = 64
SEL_TOP_N = 8
WINDOW = 512
Q_BLOCK = 128
D_FF = 3 * D_MODEL
N_BUCKETS = 32
MAX_EXACT = N_BUCKETS // 2
REL_MAX_DIST = 128
SCALE = HD ** -0.5
EPS = 1e-6
NEG_INF = -1e30
FORCE = 1e9
LOG2E = 1.4426950408889634

LANES = 128
SUBLANES = 8
ROW_CHUNK = 256
HALO = 16
N_SEG = 9
SEG_GB, SEG_GC, SEG_VA, SEG_Q, SEG_U, SEG_V, SEG_G0 = 0, 1, 2, 3, 4, 5, 6
N_MAIN = N_SEG * D_MODEL
N_KVCOL = 6 * N_KV * HD
KV_PAD = WINDOW
VMEM_LIMIT = 56 * 1024 * 1024

BF16 = jnp.bfloat16
F32 = jnp.float32
_NT = (((1,), (1,)), ((), ()))


def _params(sem):
    return pltpu.CompilerParams(dimension_semantics=sem, vmem_limit_bytes=VMEM_LIMIT)


def _rms(x, g):
    return x * lax.rsqrt(jnp.mean(x * x, axis=-1, keepdims=True) + EPS) * g


def _in_proj_body(x_ref, g_ref, w_ref, o_ref, h_ref, *, gelu_tiles, sigmoid_tiles):
    j = pl.program_id(1)

    @pl.when(j == 0)
    def _():
        h_ref[...] = _rms(x_ref[...], g_ref[...]).astype(BF16)

    def tile(act):
        for r in range(0, h_ref.shape[0], ROW_CHUNK):
            y = jnp.dot(h_ref[r:r + ROW_CHUNK, :], w_ref[...], preferred_element_type=F32)
            o_ref[r:r + ROW_CHUNK, :] = act(y).astype(o_ref.dtype)

    is_gelu = (j >= gelu_tiles[0]) & (j < gelu_tiles[1])
    is_sigmoid = (j >= sigmoid_tiles[0]) & (j < sigmoid_tiles[1])
    pl.when(is_gelu)(lambda: tile(jax.nn.gelu))
    pl.when(is_sigmoid)(lambda: tile(jax.nn.sigmoid))
    pl.when(jnp.logical_not(is_gelu | is_sigmoid))(lambda: tile(lambda y: y))


def _in_proj(x2, g, w_main, tm=2048):
    m = x2.shape[0]
    tn = D_MODEL
    body = functools.partial(_in_proj_body, gelu_tiles=(SEG_U, SEG_V + 1), sigmoid_tiles=(SEG_G0, SEG_G0 + 3))
    return pl.pallas_call(
        body,
        grid=(m // tm, N_SEG),
        in_specs=[
            pl.BlockSpec((tm, D_MODEL), lambda i, j: (i, 0)),
            pl.BlockSpec((1, D_MODEL), lambda i, j: (0, 0)),
            pl.BlockSpec((None, D_MODEL, tn), lambda i, j: (j, 0, 0)),
        ],
        out_specs=pl.BlockSpec((tm, tn), lambda i, j: (i, j)),
        out_shape=jax.ShapeDtypeStruct((m, N_MAIN), BF16),
        scratch_shapes=[pltpu.VMEM((tm, D_MODEL), BF16)],
        compiler_params=_params(("parallel", "arbitrary")),
        name="in_proj",
    )(x2, g, w_main)


def _kv_proj_body(x_ref, g_ref, w_ref, wg_ref, kc_ref, vc_ref, ksa_ref, vst_ref, kwa_ref, vwt_ref, gt_ref, cmp_ref):
    s_id = pl.program_id(1)
    tm = x_ref.shape[0]
    lane = lax.broadcasted_iota(jnp.int32, (tm, LANES), 1)
    ones_rows = (lax.broadcasted_iota(jnp.int32, (ONES_ROWS, tm), 0) == 0).astype(BF16)

    @pl.when(s_id == 0)
    def _():
        pad_keys = jnp.where(lane == HD + PAD_COL, 1.0, 0.0).astype(BF16)
        for g in range(N_KV):
            ksa_ref[g] = pad_keys
            kwa_ref[g] = pad_keys
            vst_ref[g] = jnp.zeros(vst_ref.shape[1:], BF16)
            vwt_ref[g] = jnp.zeros(vwt_ref.shape[1:], BF16)

    @pl.when(s_id > 0)
    def _():
        h = _rms(x_ref[...], g_ref[...]).astype(BF16)
        y = jnp.dot(h, w_ref[...], preferred_element_type=F32)
        tok = (s_id - 1) * tm + lax.broadcasted_iota(jnp.int32, (tm, LANES), 0)
        blk_id = jnp.where(lane - HD == tok // SEL_BLOCK, 1.0, 0.0)
        for g in range(N_KV):
            base = g * 6 * HD
            sel = y[:, base:base + 2 * HD]
            win = y[:, base + 2 * HD:base + 4 * HD]
            cmp = y[:, base + 4 * HD:base + 6 * HD]
            ksa_ref[g] = jnp.where(lane < HD, sel, blk_id).astype(BF16)
            kwa_ref[g] = jnp.where(lane < HD, win, 0.0).astype(BF16)
            vst_ref[g, 0:HD, :] = sel.T[HD:].astype(BF16)
            vst_ref[g, HD:, :] = ones_rows
            vwt_ref[g, 0:HD, :] = win.T[HD:].astype(BF16)
            vwt_ref[g, HD:, :] = ones_rows
            cmp_ref[...] = cmp
            for c in range(CMP_STRIDE):
                every = cmp_ref[pl.ds(c, tm // CMP_STRIDE, stride=CMP_STRIDE), :].astype(BF16)
                kc_ref[g, :, c * HD:(c + 1) * HD] = every[:, 0:HD]
                vc_ref[g, :, c * HD:(c + 1) * HD] = every[:, HD:]
        gt_ref[...] = jax.nn.sigmoid(jnp.dot(h, wg_ref[...], preferred_element_type=F32)).T


def _kv_proj(x2, b_, s_, g, w_kv, w_gate, tm=512):
    n_t = s_ // tm
    assert KV_PAD == tm

    def tok(b, s):
        return jnp.maximum(s - 1, 0)

    return pl.pallas_call(
        _kv_proj_body,
        grid=(b_, n_t + 1),
        in_specs=[
            pl.BlockSpec((tm, D_MODEL), lambda b, s: (b * n_t + tok(b, s), 0)),
            pl.BlockSpec((1, D_MODEL), lambda b, s: (0, 0)),
            pl.BlockSpec((D_MODEL, N_KVCOL), lambda b, s: (0, 0), pipeline_mode=pl.Buffered(1)),
            pl.BlockSpec((D_MODEL, LANES), lambda b, s: (0, 0), pipeline_mode=pl.Buffered(1)),
        ],
        out_specs=[
            pl.BlockSpec((None, N_KV, tm // CMP_STRIDE, CMP_STRIDE * HD), lambda b, s: (b, 0, tok(b, s), 0)),
            pl.BlockSpec((None, N_KV, tm // CMP_STRIDE, CMP_STRIDE * HD), lambda b, s: (b, 0, tok(b, s), 0)),
            pl.BlockSpec((None, N_KV, tm, 2 * HD), lambda b, s: (b, 0, s, 0)),
            pl.BlockSpec((None, N_KV, HD + ONES_ROWS, tm), lambda b, s: (b, 0, 0, s)),
            pl.BlockSpec((None, N_KV, tm, 2 * HD), lambda b, s: (b, 0, s, 0)),
            pl.BlockSpec((None, N_KV, HD + ONES_ROWS, tm), lambda b, s: (b, 0, 0, s)),
            pl.BlockSpec((None, LANES, tm), lambda b, s: (b, 0, tok(b, s))),
        ],
        out_shape=[
            jax.ShapeDtypeStruct((b_, N_KV, s_ // CMP_STRIDE, CMP_STRIDE * HD), BF16),
            jax.ShapeDtypeStruct((b_, N_KV, s_ // CMP_STRIDE, CMP_STRIDE * HD), BF16),
            jax.ShapeDtypeStruct((b_, N_KV, KV_PAD + s_, 2 * HD), BF16),
            jax.ShapeDtypeStruct((b_, N_KV, HD + ONES_ROWS, KV_PAD + s_), BF16),
            jax.ShapeDtypeStruct((b_, N_KV, KV_PAD + s_, 2 * HD), BF16),
            jax.ShapeDtypeStruct((b_, N_KV, HD + ONES_ROWS, KV_PAD + s_), BF16),
            jax.ShapeDtypeStruct((b_, LANES, s_), F32),
        ],
        scratch_shapes=[pltpu.VMEM((tm, 2 * HD), F32)],
        compiler_params=_params(("parallel", "arbitrary")),
        name="kv_proj",
    )(x2, g, w_kv, w_gate)


def _rel_bucket(dist):
    dist = jnp.maximum(dist, 0)
    log_ratio = jnp.log(jnp.maximum(dist, 1).astype(F32) / MAX_EXACT) / math.log(REL_MAX_DIST / MAX_EXACT)
    large = MAX_EXACT + (log_ratio * (N_BUCKETS - MAX_EXACT)).astype(jnp.int32)
    return jnp.where(dist < MAX_EXACT, dist, jnp.minimum(large, N_BUCKETS - 1))


def _bias_body(tab_ref, bt_ref, bc_ref, tt_ref, bco_ref, *, n_cmp):
    h = pl.program_id(0)

    def lookup(bk):
        acc = jnp.zeros(bk.shape, F32)
        for b in range(N_BUCKETS):
            acc = jnp.where(bk == b, tab_ref[b, h], acc)
        return acc

    kk = lax.broadcasted_iota(jnp.int32, (LANES, LANES), 0)
    qq = lax.broadcasted_iota(jnp.int32, (LANES, LANES), 1)
    far = lookup(bt_ref[2])
    tt_ref[0] = (lookup(bt_ref[1]) - far) * LOG2E
    tt_ref[1] = jnp.where(qq >= kk, (lookup(bt_ref[0]) - far) * LOG2E, NEG_INF)
    tt_ref[2] = jnp.where(kk > qq, 0.0, NEG_INF)
    for i in range(bc_ref.shape[0]):
        ok = (i * Q_BLOCK + qq - (kk * CMP_STRIDE + CMP_LEN - 1) >= 0) & (kk < n_cmp)
        bco_ref[i] = jnp.where(ok, lookup(bc_ref[i]), NEG_INF)


def _bias_tiles(rel_bias, n_qb):
    kk = jnp.arange(LANES)[:, None]
    qq = jnp.arange(LANES)[None, :]
    bt = jnp.stack([_rel_bucket(LANES * d + qq - kk) for d in range(3)]).astype(jnp.int32)
    cmp_end = CMP_STRIDE * jnp.arange(LANES) + CMP_LEN - 1
    t = (Q_BLOCK * jnp.arange(n_qb))[:, None, None] + qq[None]
    bc = _rel_bucket(t - cmp_end[None, :, None]).astype(jnp.int32)
    n_cmp = (n_qb * Q_BLOCK - CMP_LEN) // CMP_STRIDE + 1
    return pl.pallas_call(
        functools.partial(_bias_body, n_cmp=n_cmp),
        grid=(N_HEADS,),
        in_specs=[
            pl.BlockSpec(memory_space=pltpu.SMEM),
            pl.BlockSpec((3, LANES, LANES), lambda h: (0, 0, 0)),
            pl.BlockSpec((n_qb, LANES, LANES), lambda h: (0, 0, 0)),
        ],
        out_specs=[
            pl.BlockSpec((None, 3, LANES, LANES), lambda h: (h // Q_PER_KV, 0, 0, h % Q_PER_KV)),
            pl.BlockSpec((None, n_qb, LANES, LANES), lambda h: (h // Q_PER_KV, 0, 0, h % Q_PER_KV)),
        ],
        out_shape=[
            jax.ShapeDtypeStruct((N_KV, 3, LANES, Q_PER_KV * LANES), F32),
            jax.ShapeDtypeStruct((N_KV, n_qb, LANES, Q_PER_KV * LANES), F32),
        ],
        compiler_params=_params(("arbitrary",)),
        name="bias_tiles",
    )(rel_bias.astype(F32), bt, bc)


def _compress_body(k3_ref, v3_ref, pos_ref, w1_ref, w2_ref, ko_ref, vo_ref):
    half = CMP_STRIDE * HD
    for which, (src, dst) in enumerate(((k3_ref, ko_ref), (v3_ref, vo_ref))):
        x = src[...].astype(F32)
        pos = pos_ref[which]
        xa = (x + pos[:, :half]).astype(BF16)
        xb = (x + pos[:, half:]).astype(BF16)
        a = jnp.dot(xa, w1_ref[which, :half, :], preferred_element_type=F32)
        b = jnp.dot(xb, w1_ref[which, half:, :], preferred_element_type=F32)
        hid = jax.nn.gelu(a + pltpu.roll(b, LANES - 1, 0))
        dst[...] = jnp.dot(hid.astype(BF16), w2_ref[which], preferred_element_type=F32).astype(dst.dtype)


def _compress(k3, v3, pos, w1, w2):
    b_, g_, nm, width = k3.shape
    spec3 = pl.BlockSpec((None, None, nm, width), lambda b, g: (b, g, 0, 0))
    ospec = pl.BlockSpec((None, None, nm, HD), lambda b, g: (b, g, 0, 0))
    return pl.pallas_call(
        _compress_body,
        grid=(b_, g_),
        in_specs=[
            spec3, spec3,
            pl.BlockSpec((2, 1, CMP_LEN * HD), lambda b, g: (0, 0, 0)),
            pl.BlockSpec((2, CMP_LEN * HD, CMP_HID), lambda b, g: (0, 0, 0)),
            pl.BlockSpec((2, CMP_HID, HD), lambda b, g: (0, 0, 0)),
        ],
        out_specs=[ospec, ospec],
        out_shape=[jax.ShapeDtypeStruct((b_, g_, nm, HD), BF16)] * 2,
        compiler_params=_params(("parallel", "parallel")),
        name="compress",
    )(k3, v3, pos, w1, w2)


FAR_TILES = 2
PAD_COL = 32
ONES_ROWS = 16


def _nsa_body(qa_ref, qb_ref, kc_ref, vct_ref, ksa_ref, vst_ref, kwa_ref, vwt_ref, gta_ref, gtb_ref,
              tt_ref, bca_ref, bcb_ref, ovt_ref, o_ref, sbuf_ref, qf_ref):
    p_id = pl.program_id(2)
    n_qb = 2 * pl.num_programs(2)
    nl = Q_PER_KV * LANES
    n_win = WINDOW // LANES + 1
    n_blk = ovt_ref.shape[0]
    max_chunks = sbuf_ref.shape[0]
    pad_tiles = KV_PAD // LANES
    j_io = lax.broadcasted_iota(jnp.int32, (n_blk, LANES), 0)
    q_io = lax.broadcasted_iota(jnp.int32, (n_blk, LANES), 1)
    col_w = lax.broadcasted_iota(jnp.int32, (nl, HD), 1)
    pad_cols = jnp.where(col_w == PAD_COL, NEG_INF, 0.0).astype(BF16)

    def compressed(q_ref, bc_ref):
        qt = q_ref[...]
        q = jnp.concatenate([qt[:, r * HD:(r + 1) * HD] for r in range(Q_PER_KV)], axis=0) * SCALE
        s = lax.dot_general(kc_ref[...], q, _NT, preferred_element_type=F32) + bc_ref[...]
        m_c = jnp.max(s, axis=0, keepdims=True)
        p = jnp.exp(s - m_c)
        l_c = jnp.sum(p, axis=0, keepdims=True)
        p_c = p * (jnp.where(m_c > 0.5 * NEG_INF, 1.0, 0.0) / l_c)
        o_c = jnp.dot(vct_ref[...], p_c.astype(BF16), preferred_element_type=F32)
        return o_c, p_c, (q.astype(F32) * LOG2E).astype(BF16)

    def select(i, p_c, q2, which):
        p_sum = p_c[:, 0:LANES]
        for r in range(1, Q_PER_KV):
            p_sum = p_sum + p_c[:, r * LANES:(r + 1) * LANES]
        imp = jnp.dot(ovt_ref[...], p_sum, precision=lax.Precision.HIGHEST, preferred_element_type=F32)
        t_b = i * Q_BLOCK + q_io
        cur = lax.shift_right_logical(t_b, SEL_BLOCK.bit_length() - 1)
        forced = (j_io == 0) | (j_io == cur) | (j_io == cur - 1)
        valid = j_io * SEL_BLOCK <= t_b
        score = jnp.where(forced, FORCE, jnp.where(valid, imp, -FORCE))
        rank = jnp.zeros((n_blk, LANES), jnp.int32)
        for ii in range(n_blk):
            row = score[ii:ii + 1, :]
            beats = (row > score) | ((row == score) & (j_io > ii))
            rank = rank + beats.astype(jnp.int32)
        keep = (rank < min(SEL_TOP_N, n_blk)) & (score >= 0.0)

        def augment(kept):
            cols = jnp.concatenate([jnp.where(kept, 0.0, NEG_INF), jnp.full((SUBLANES, LANES), NEG_INF, F32),
                                    jnp.zeros((LANES - n_blk - SUBLANES, LANES), F32)], axis=0)
            cols_t = cols.T[:, 0:HD].astype(BF16)
            return jnp.concatenate([q2, jnp.concatenate([cols_t] * Q_PER_KV, axis=0)], axis=1)

        qf_ref[which] = augment(keep & (j_io < 2 * (i - 1)))
        return augment(keep)

    def softmax_av(tiles, vt):
        s_all = jnp.concatenate(tiles, axis=0)
        m = jnp.max(s_all, axis=0, keepdims=True)
        p = jnp.exp2(s_all - m)
        return m, jnp.dot(vt, p.astype(BF16), preferred_element_type=F32)

    def normalise(av):
        return av[0:HD] / av[HD:HD + 1]

    def logit_tiles(k_ref, qq, row0, n):
        s_all = lax.dot_general(k_ref[pl.ds(row0, n * LANES), :], qq, _NT, preferred_element_type=F32)
        return [s_all[t * LANES:(t + 1) * LANES] for t in range(n)]

    def window_logits(i, q2):
        r0 = pl.multiple_of((i + pad_tiles + 1 - n_win) * LANES, LANES)
        tw = logit_tiles(kwa_ref, jnp.concatenate([q2, pad_cols], axis=1), r0, n_win)
        tw[0] = tw[0] + tt_ref[2]
        tw[n_win - 2] = tw[n_win - 2] + tt_ref[0]
        tw[n_win - 1] = tw[n_win - 1] + tt_ref[1]
        return tw

    def window_softmax(i, tw):
        r0 = pl.multiple_of((i + pad_tiles + 1 - n_win) * LANES, LANES)
        return normalise(softmax_av(tw, vwt_ref[:, pl.ds(r0, n_win * LANES)])[1])

    def near_logits(i, q_near):
        r1 = pl.multiple_of((i + pad_tiles - 1) * LANES, LANES)
        tn = logit_tiles(ksa_ref, q_near, r1, 2)
        tn[0] = tn[0] + tt_ref[0]
        tn[1] = tn[1] + tt_ref[1]
        return tn

    def near_softmax(i, tn):
        r1 = pl.multiple_of((i + pad_tiles - 1) * LANES, LANES)
        return softmax_av(tn, vst_ref[:, pl.ds(r1, 2 * LANES)])

    i_a = p_id
    i_b = n_qb - 1 - p_id
    oc_a, pc_a, q2_a = compressed(qa_ref, bca_ref)
    oc_b, pc_b, q2_b = compressed(qb_ref, bcb_ref)
    tw_a = window_logits(i_a, q2_a)
    tw_b = window_logits(i_b, q2_b)
    qn_a = select(i_a, pc_a, q2_a, 0)
    ow_a = window_softmax(i_a, tw_a)
    qn_b = select(i_b, pc_b, q2_b, 1)
    ow_b = window_softmax(i_b, tw_b)
    tn_a = near_logits(i_a, qn_a)
    tn_b = near_logits(i_b, qn_b)
    mn_a, avn_a = near_softmax(i_a, tn_a)
    mn_b, avn_b = near_softmax(i_b, tn_b)

    n_a = (jnp.maximum(i_a - 1, 0) + FAR_TILES - 1) // FAR_TILES
    is_a, rows, mx = [], [], []
    for k in range(max_chunks):
        own = k < n_a
        c = jnp.where(own, k, k - n_a)
        r = pl.multiple_of((c * FAR_TILES + pad_tiles) * LANES, LANES)
        s_k = lax.dot_general(ksa_ref[pl.ds(r, FAR_TILES * LANES), :], qf_ref[jnp.where(own, 0, 1)], _NT,
                              preferred_element_type=F32)
        sbuf_ref[k] = s_k
        is_a.append(own)
        rows.append(r)
        mx.append(jnp.max(s_k, axis=0, keepdims=True))
    m_a, m_b = mn_a, mn_b
    for k in range(max_chunks):
        m_a = jnp.where(is_a[k], jnp.maximum(m_a, mx[k]), m_a)
        m_b = jnp.where(is_a[k], m_b, jnp.maximum(m_b, mx[k]))
    av_a = jnp.exp2(mn_a - m_a) * avn_a
    av_b = jnp.exp2(mn_b - m_b) * avn_b
    for k in range(max_chunks):
        p = jnp.exp2(sbuf_ref[k] - jnp.where(is_a[k], m_a, m_b))
        pv = jnp.dot(vst_ref[:, pl.ds(rows[k], FAR_TILES * LANES)], p.astype(BF16), preferred_element_type=F32)
        av_a = av_a + jnp.where(is_a[k], pv, 0.0)
        av_b = av_b + jnp.where(is_a[k], 0.0, pv)

    def finish(half, gt_ref, o_c, o_s, o_w):
        def gate(branch):
            rows = gt_ref[pl.ds(branch * N_HEADS + pl.program_id(1) * Q_PER_KV, Q_PER_KV), :]
            return jnp.concatenate([rows[r:r + 1] for r in range(Q_PER_KV)], axis=1)

        o_t = gate(0) * o_c + gate(1) * o_s + gate(2) * o_w
        for pair in range(Q_PER_KV // 2):
            blk = jnp.concatenate([o_t[:, (2 * pair) * LANES:(2 * pair + 1) * LANES],
                                   o_t[:, (2 * pair + 1) * LANES:(2 * pair + 2) * LANES]], axis=0)
            o_ref[half, :, pair * LANES:(pair + 1) * LANES] = blk.T.astype(o_ref.dtype)

    finish(0, gta_ref, oc_a, normalise(av_a), ow_a)
    finish(1, gtb_ref, oc_b, normalise(av_b), ow_b)


def _nsa(proj, kc, vct, ksa, vst, kwa, vwt, gt, tt, bc, ovt, s_):
    b_, g_ = kwa.shape[:2]
    n_qb = s_ // Q_BLOCK
    n_p = n_qb // 2
    m = proj.shape[0]
    nl = Q_PER_KV * LANES
    qcol = SEG_Q * D_MODEL // (Q_PER_KV * HD)
    def far_chunks(i):
        return (max(i - 1, 0) + FAR_TILES - 1) // FAR_TILES

    max_chunks = far_chunks(0) + far_chunks(n_qb - 1)
    assert n_qb % 2 == 0 and all(far_chunks(p) + far_chunks(n_qb - 1 - p) == max_chunks for p in range(n_p))

    def bg(shape):
        return pl.BlockSpec((None, None) + shape, lambda b, g, p: (b, g, 0, 0))

    def blk_a(p):
        return p

    def blk_b(p):
        return n_qb - 1 - p

    def per_block(which):
        return [
            pl.BlockSpec((Q_BLOCK, Q_PER_KV * HD), lambda b, g, p: (b * n_qb + which(p), qcol + g)),
            pl.BlockSpec((None, LANES, Q_BLOCK), lambda b, g, p: (b, 0, which(p))),
            pl.BlockSpec((None, None, LANES, nl), lambda b, g, p: (g, which(p), 0, 0)),
        ]

    qa, gta, bca = per_block(blk_a)
    qb, gtb, bcb = per_block(blk_b)
    out = pl.pallas_call(
        _nsa_body,
        grid=(b_, g_, n_p),
        in_specs=[
            qa, qb,
            bg((LANES, HD)), bg((HD, LANES)),
            bg((s_ + KV_PAD, 2 * HD)), bg((HD + ONES_ROWS, s_ + KV_PAD)),
            bg((s_ + KV_PAD, 2 * HD)), bg((HD + ONES_ROWS, s_ + KV_PAD)),
            gta, gtb,
            pl.BlockSpec((None, 3, LANES, nl), lambda b, g, p: (g, 0, 0, 0)),
            bca, bcb,
            pl.BlockSpec((s_ // SEL_BLOCK, LANES), lambda b, g, p: (0, 0)),
        ],
        out_specs=pl.BlockSpec((2, None, Q_BLOCK, Q_PER_KV * HD), lambda b, g, p: (0, b * n_p + p, 0, g)),
        out_shape=jax.ShapeDtypeStruct((2, b_ * n_p, Q_BLOCK, D_MODEL), BF16),
        scratch_shapes=[pltpu.VMEM((max_chunks, FAR_TILES * LANES, nl), F32), pltpu.VMEM((2, nl, 2 * HD), BF16)],
        compiler_params=_params(("parallel", "parallel", "arbitrary")),
        name="nsa",
    )(proj, proj, kc, vct, ksa, vst, kwa, vwt, gt, gt, tt, bc, bc, ovt)
    return out.reshape(m, D_MODEL)


def _attn_block_row(b, qb, n_b, n_qb):
    n_p = n_qb // 2
    upper = qb >= n_p
    return jnp.where(upper, n_b * n_p, 0) + b * n_p + jnp.where(upper, n_qb - 1 - qb, qb)


def _mix_body(gb_ref, gc_ref, va_ref, gch_ref, vah_ref, uc_ref, vc_ref, g0_ref, g1_ref, g2_ref, x_ref,
              ca_ref, sw_ref, sb_ref, lng_ref, lnb_ref, wa_ref, wb_ref, wc_ref, wo_ref, *rest):
    tm = gb_ref.shape[0]
    ob_refs = rest[:tm // Q_BLOCK]
    o_ref, xa_ref, mx_ref = rest[tm // Q_BLOCK:]
    first = pl.program_id(1) == 0

    halo = gch_ref[...].astype(F32) * vah_ref[...].astype(F32)
    xa_ref[0:HALO, :] = jnp.where(first, 0.0, halo)
    xa_ref[HALO:, :] = gc_ref[...].astype(F32) * va_ref[...].astype(F32)
    ca = ca_ref[...]
    row = lax.broadcasted_iota(jnp.int32, (CHUNK, CHUNK), 0)
    col = lax.broadcasted_iota(jnp.int32, (CHUNK, CHUNK), 1)
    sw = [jnp.where(row >= col, sw_ref[gi], 0.0).astype(BF16) for gi in range(SGU_GROUPS)]

    for r in range(0, tm, ROW_CHUNK):
        rows = slice(r, r + ROW_CHUNK)
        conv = (ca[0:1] * xa_ref[pl.ds(r + HALO - 2, ROW_CHUNK), :]
                + ca[1:2] * xa_ref[pl.ds(r + HALO - 1, ROW_CHUNK), :]
                + ca[2:3] * xa_ref[pl.ds(r + HALO, ROW_CHUNK), :])
        out_a = gb_ref[rows, :].astype(F32) * conv
        y_a = jnp.dot(out_a.astype(BF16), wa_ref[...], preferred_element_type=F32)

        v = vc_ref[rows, :].astype(F32)
        mu = jnp.mean(v, axis=-1, keepdims=True)
        vz = v - mu
        var = jnp.mean(vz * vz, axis=-1, keepdims=True)
        vn = (vz * lax.rsqrt(var + EPS) * lng_ref[...] + lnb_ref[...]).astype(BF16)
        for gi in range(SGU_GROUPS):
            cols = slice(gi * LANES, (gi + 1) * LANES)
            for c in range(0, ROW_CHUNK, CHUNK):
                mx_ref[r + c:r + c + CHUNK, cols] = (
                    jnp.dot(sw[gi], vn[c:c + CHUNK, cols], preferred_element_type=F32) + sb_ref[:, cols])
        out_c = uc_ref[rows, :].astype(F32) * mx_ref[rows, :]
        y_c = jnp.dot(out_c.astype(BF16), wc_ref[...], preferred_element_type=F32)

        out_b = jnp.concatenate([ob_refs[k][...] for k in range(r // Q_BLOCK, (r + ROW_CHUNK) // Q_BLOCK)], axis=0)
        y_b = jnp.dot(out_b, wb_ref[...], preferred_element_type=F32)

        merged = (g0_ref[rows, :].astype(F32) * y_a + g1_ref[rows, :].astype(F32) * y_b
                  + g2_ref[rows, :].astype(F32) * y_c)
        o_ref[rows, :] = x_ref[rows, :] + jnp.dot(merged.astype(BF16), wo_ref[...], preferred_element_type=F32)


def _mix(proj, out_b, x2, seq, conv_a, sgu_w, sgu_b2, ln_g, ln_b, w_a, w_b, w_c, w_o, tm=512):
    m = x2.shape[0]
    n_t = seq // tm
    hb = tm // HALO

    def seg(k):
        return pl.BlockSpec((tm, D_MODEL), lambda b, i: (b * n_t + i, k))

    def halo(k):
        return pl.BlockSpec((HALO, D_MODEL), lambda b, i: (jnp.maximum((b * n_t + i) * hb - 1, 0), k))

    def full(shape):
        return pl.BlockSpec(shape, lambda b, i: (0,) * len(shape))

    def attn(k):
        return pl.BlockSpec((Q_BLOCK, D_MODEL), lambda b, i: (
            _attn_block_row(b, i * (tm // Q_BLOCK) + k, m // seq, seq // Q_BLOCK), 0))

    row = pl.BlockSpec((tm, D_MODEL), lambda b, i: (b * n_t + i, 0))
    wspec = pl.BlockSpec((D_MODEL, D_MODEL), lambda b, i: (0, 0), pipeline_mode=pl.Buffered(1))
    return pl.pallas_call(
        _mix_body,
        grid=(m // seq, n_t),
        in_specs=[
            seg(SEG_GB), seg(SEG_GC), seg(SEG_VA), halo(SEG_GC), halo(SEG_VA), seg(SEG_U), seg(SEG_V),
            seg(SEG_G0), seg(SEG_G0 + 1), seg(SEG_G0 + 2), row,
            full((CONV_W, D_MODEL)), full((SGU_GROUPS, CHUNK, CHUNK)), full((CHUNK, D_MODEL)),
            full((1, D_MODEL)), full((1, D_MODEL)), wspec, wspec, wspec, wspec,
        ] + [attn(k) for k in range(tm // Q_BLOCK)],
        out_specs=row,
        out_shape=jax.ShapeDtypeStruct((m, D_MODEL), F32),
        scratch_shapes=[pltpu.VMEM((tm + HALO, D_MODEL), F32), pltpu.VMEM((tm, D_MODEL), F32)],
        compiler_params=_params(("parallel", "arbitrary")),
        name="mix",
    )(proj, proj, proj, proj, proj, proj, proj, proj, proj, proj, x2,
      conv_a, sgu_w, sgu_b2, ln_g, ln_b, w_a, w_b, w_c, w_o, *([out_b] * (tm // Q_BLOCK)))


def _ffn_body(x_ref, xh_ref, g_ref, wg_ref, wv_ref, cw_ref, wd_ref, fg_ref, o_ref, h_ref, gt_ref, acc_ref, *,
              tiles_per_seq, final):
    i = pl.program_id(0)
    j = pl.program_id(1)
    tm = x_ref.shape[0]

    @pl.when(j == 0)
    def _():
        h_ref[0:HALO, :] = _rms(xh_ref[...], g_ref[...]).astype(BF16)
        h_ref[HALO:, :] = _rms(x_ref[...], g_ref[...]).astype(BF16)
        acc_ref[...] = jnp.zeros_like(acc_ref)

    gate = jnp.dot(h_ref[...], wg_ref[...], preferred_element_type=F32)
    rows = lax.broadcasted_iota(jnp.int32, gate.shape, 0)
    seq_start = (i % tiles_per_seq) == 0
    gt_ref[...] = jnp.where(seq_start & (rows < HALO), 0.0, gate)
    cw = cw_ref[...]
    conv = (cw[0:1] * gt_ref[pl.ds(HALO - 2, tm), :] + cw[1:2] * gt_ref[pl.ds(HALO - 1, tm), :]
            + cw[2:3] * gt_ref[pl.ds(HALO, tm), :])
    val = jnp.dot(h_ref[HALO:, :], wv_ref[...], preferred_element_type=F32)
    act = (jax.nn.gelu(conv) * val).astype(BF16)
    acc_ref[...] += jnp.dot(act, wd_ref[...], preferred_element_type=F32)

    @pl.when(j == pl.num_programs(1) - 1)
    def _():
        y = x_ref[...] + acc_ref[...]
        o_ref[...] = _rms(y, fg_ref[...]) if final else y


def _ffn(x2, seq, g, w_up, conv_w, w_down, final_g, final, tm=1024, tf=512):
    m = x2.shape[0]
    n_f = D_FF // tf
    w_up = jnp.stack([w_up[:, k * tf:(k + 1) * tf] for k in range(2 * n_f)]).astype(BF16)
    hb = tm // HALO
    return pl.pallas_call(
        functools.partial(_ffn_body, tiles_per_seq=seq // tm, final=final),
        grid=(m // tm, n_f),
        in_specs=[
            pl.BlockSpec((tm, D_MODEL), lambda i, j: (i, 0)),
            pl.BlockSpec((HALO, D_MODEL), lambda i, j: (jnp.maximum(i * hb - 1, 0), 0)),
            pl.BlockSpec((1, D_MODEL), lambda i, j: (0, 0)),
            pl.BlockSpec((None, D_MODEL, tf), lambda i, j: (j, 0, 0)),
            pl.BlockSpec((None, D_MODEL, tf), lambda i, j: (n_f + j, 0, 0)),
            pl.BlockSpec((CONV_W, tf), lambda i, j: (0, j)),
            pl.BlockSpec((tf, D_MODEL), lambda i, j: (j, 0)),
            pl.BlockSpec((1, D_MODEL), lambda i, j: (0, 0)),
        ],
        out_specs=pl.BlockSpec((tm, D_MODEL), lambda i, j: (i, 0)),
        out_shape=jax.ShapeDtypeStruct((m, D_MODEL), F32),
        scratch_shapes=[
            pltpu.VMEM((tm + HALO, D_MODEL), BF16),
            pltpu.VMEM((tm + HALO, tf), F32),
            pltpu.VMEM((tm, D_MODEL), F32),
        ],
        compiler_params=_params(("parallel", "arbitrary")),
        name="ffn",
    )(x2, x2, g, w_up, w_up, conv_w, w_down, final_g)


def _split_w_in(w_in):
    c = [0]
    for sz in [D_MODEL] * 4 + [N_KV * HD] * 6 + [3 * N_HEADS] + [D_MODEL] * 5:
        c.append(c[-1] + sz)
    starts = c[0:4] + c[11:16]
    main = jnp.stack([w_in[:, a:a + D_MODEL] for a in starts]).astype(BF16)
    kv = w_in[:, c[4]:c[10]].reshape(D_MODEL, 6, N_KV, HD)[:, jnp.array([2, 3, 4, 5, 0, 1])]
    kv = kv.transpose(0, 2, 1, 3).reshape(D_MODEL, N_KVCOL)
    gate = jnp.pad(w_in[:, c[10]:c[11]], ((0, 0), (0, LANES - 3 * N_HEADS)))
    return main, kv.astype(BF16), gate.astype(BF16)


def _mixer_layer(x2, b_, s_, tt, bc, ovt, norm_g, w_in, conv_a, cmp_pos, cmp_w1, cmp_w2, sgu_w, sgu_b,
                 sgu_norm_g, sgu_norm_b, w_br_a, w_br_b, w_br_c, w_o):
    w_main, w_kv, w_gate = _split_w_in(w_in)
    norm_g = norm_g.reshape(1, D_MODEL)
    proj = _in_proj(x2, norm_g, w_main)
    kc, vc, ksa, vst, kwa, vwt, gt = _kv_proj(x2, b_, s_, norm_g, w_kv, w_gate)

    k_cmp, v_cmp = _compress(kc, vc, cmp_pos.reshape(2, 1, CMP_LEN * HD),
                             cmp_w1.astype(BF16), cmp_w2.astype(BF16))
    out_b = _nsa(proj, k_cmp, v_cmp.transpose(0, 1, 3, 2), ksa, vst, kwa, vwt, gt, tt, bc, ovt, s_)

    sgu_b2 = jnp.broadcast_to(sgu_b.T[:, :, None], (CHUNK, SGU_GROUPS, D_MODEL // SGU_GROUPS)).reshape(CHUNK, D_MODEL)
    return _mix(proj, out_b, x2, s_, conv_a, sgu_w, sgu_b2,
                sgu_norm_g.reshape(1, D_MODEL), sgu_norm_b.reshape(1, D_MODEL),
                w_br_a.astype(BF16), w_br_b.astype(BF16), w_br_c.astype(BF16), w_o.astype(BF16))


def _overlap_t(s_):
    n_blk = s_ // SEL_BLOCK
    cmp_start = jnp.arange(LANES) * CMP_STRIDE
    cmp_end = cmp_start + CMP_LEN - 1
    blk_start = jnp.arange(n_blk) * SEL_BLOCK
    n_cmp = (s_ - CMP_LEN) // CMP_STRIDE + 1
    ov = ((cmp_start[None, :] < blk_start[:, None] + SEL_BLOCK) & (cmp_end[None, :] >= blk_start[:, None])
          & (jnp.arange(LANES)[None, :] < n_cmp))
    return ov.astype(F32)


def kernel(x, rel_bias, norm_mix, w_in, conv_a, cmp_pos, cmp_w1, cmp_w2, sgu_w, sgu_b, sgu_norm_g, sgu_norm_b,
           w_br_a, w_br_b, w_br_c, w_o, norm_ffn, ffn_w_up, ffn_conv, ffn_w_down, norm_final):
    b_, s_, _ = x.shape
    depth = w_in.shape[0]
    x2 = x.reshape(b_ * s_, D_MODEL)
    tt, bc = _bias_tiles(rel_bias, s_ // Q_BLOCK)
    ovt = _overlap_t(s_)
    for l in range(depth):
        x2 = _mixer_layer(x2, b_, s_, tt, bc, ovt, norm_mix[l], w_in[l], conv_a[l], cmp_pos[l], cmp_w1[l],
                          cmp_w2[l], sgu_w[l], sgu_b[l], sgu_norm_g[l], sgu_norm_b[l],
                          w_br_a[l], w_br_b[l], w_br_c[l], w_o[l])
        x2 = _ffn(x2, s_, norm_ffn[l].reshape(1, D_MODEL), ffn_w_up[l], ffn_conv[l], ffn_w_down[l].astype(BF16),
                  norm_final.reshape(1, D_MODEL), final=(l == depth - 1))
    return x2.reshape(b_, s_, D_MODEL)
```

```python
import functools
import math

import jax
import jax.numpy as jnp
from jax import lax
from jax.experimental import pallas as pl
from jax.experimental.pallas import tpu as pltpu

D_MODEL = 1024
HD = 64
N_HEADS = D_MODEL // HD
N_KV = 4
Q_PER_KV = N_HEADS // N_KV
SGU_GROUPS = 8
CHUNK = 128
CONV_W = 3
CMP_LEN = 32
CMP_STRIDE = 16
CMP_HID = 2 * HD
SEL_BLOCK = 64
SEL_TOP_N = 8
WINDOW = 512
Q_BLOCK = 128
D_FF = 3 * D_MODEL
N_BUCKETS = 32
MAX_EXACT = N_BUCKETS // 2
REL_MAX_DIST = 128
SCALE = HD ** -0.5
EPS = 1e-6
NEG_INF = -1e30
FORCE = 1e9
LOG2E = 1.4426950408889634

LANES = 128
SUBLANES = 8
ROW_CHUNK = 256
HALO = 16
N_SEG = 9
SEG_GB, SEG_GC, SEG_VA, SEG_Q, SEG_U, SEG_V, SEG_G0 = 0, 1, 2, 3, 4, 5, 6
N_MAIN = N_SEG * D_MODEL
N_KVCOL = 6 * N_KV * HD
KV_PAD = WINDOW
VMEM_LIMIT = 56 * 1024 * 1024

BF16 = jnp.bfloat16
F32 = jnp.float32
_NT = (((1,), (1,)), ((), ()))


def _params(sem):
    return pltpu.CompilerParams(dimension_semantics=sem, vmem_limit_bytes=VMEM_LIMIT)


def _rms(x, g):
    return x * lax.rsqrt(jnp.mean(x * x, axis=-1, keepdims=True) + EPS) * g


def _in_proj_body(x_ref, g_ref, w_ref, o_ref, h_ref, *, gelu_tiles, sigmoid_tiles):
    j = pl.program_id(1)

    @pl.when(j == 0)
    def _():
        h_ref[...] = _rms(x_ref[...], g_ref[...]).astype(BF16)

    def tile(act):
        for r in range(0, h_ref.shape[0], ROW_CHUNK):
            y = jnp.dot(h_ref[r:r + ROW_CHUNK, :], w_ref[...], preferred_element_type=F32)
            o_ref[r:r + ROW_CHUNK, :] = act(y).astype(o_ref.dtype)

    is_gelu = (j >= gelu_tiles[0]) & (j < gelu_tiles[1])
    is_sigmoid = (j >= sigmoid_tiles[0]) & (j < sigmoid_tiles[1])
    pl.when(is_gelu)(lambda: tile(jax.nn.gelu))
    pl.when(is_sigmoid)(lambda: tile(jax.nn.sigmoid))
    pl.when(jnp.logical_not(is_gelu | is_sigmoid))(lambda: tile(lambda y: y))


def _in_proj(x2, g, w_main, tm=2048):
    m = x2.shape[0]
    tn = D_MODEL
    body = functools.partial(_in_proj_body, gelu_tiles=(SEG_U, SEG_V + 1), sigmoid_tiles=(SEG_G0, SEG_G0 + 3))
    return pl.pallas_call(
        body,
        grid=(m // tm, N_SEG),
        in_specs=[
            pl.BlockSpec((tm, D_MODEL), lambda i, j: (i, 0)),
            pl.BlockSpec((1, D_MODEL), lambda i, j: (0, 0)),
            pl.BlockSpec((None, D_MODEL, tn), lambda i, j: (j, 0, 0)),
        ],
        out_specs=pl.BlockSpec((tm, tn), lambda i, j: (i, j)),
        out_shape=jax.ShapeDtypeStruct((m, N_MAIN), BF16),
        scratch_shapes=[pltpu.VMEM((tm, D_MODEL), BF16)],
        compiler_params=_params(("parallel", "arbitrary")),
        name="in_proj",
    )(x2, g, w_main)


def _kv_proj_body(x_ref, g_ref, w_ref, wg_ref, kc_ref, vc_ref, ksa_ref, vst_ref, kwa_ref, vwt_ref, gt_ref, cmp_ref):
    s_id = pl.program_id(1)
    tm = x_ref.shape[0]
    lane = lax.broadcasted_iota(jnp.int32, (tm, LANES), 1)
    ones_rows = (lax.broadcasted_iota(jnp.int32, (ONES_ROWS, tm), 0) == 0).astype(BF16)

    @pl.when(s_id == 0)
    def _():
        pad_keys = jnp.where(lane == HD + PAD_COL, 1.0, 0.0).astype(BF16)
        for g in range(N_KV):
            ksa_ref[g] = pad_keys
            kwa_ref[g] = pad_keys
            vst_ref[g] = jnp.zeros(vst_ref.shape[1:], BF16)
            vwt_ref[g] = jnp.zeros(vwt_ref.shape[1:], BF16)

    @pl.when(s_id > 0)
    def _():
        h = _rms(x_ref[...], g_ref[...]).astype(BF16)
        y = jnp.dot(h, w_ref[...], preferred_element_type=F32)
        tok = (s_id - 1) * tm + lax.broadcasted_iota(jnp.int32, (tm, LANES), 0)
        blk_id = jnp.where(lane - HD == tok // SEL_BLOCK, 1.0, 0.0)
        for g in range(N_KV):
            base = g * 6 * HD
            sel = y[:, base:base + 2 * HD]
            win = y[:, base + 2 * HD:base + 4 * HD]
            cmp = y[:, base + 4 * HD:base + 6 * HD]
            ksa_ref[g] = jnp.where(lane < HD, sel, blk_id).astype(BF16)
            kwa_ref[g] = jnp.where(lane < HD, win, 0.0).astype(BF16)
            vst_ref[g, 0:HD, :] = sel.T[HD:].astype(BF16)
            vst_ref[g, HD:, :] = ones_rows
            vwt_ref[g, 0:HD, :] = win.T[HD:].astype(BF16)
            vwt_ref[g, HD:, :] = ones_rows
            cmp_ref[...] = cmp
            for c in range(CMP_STRIDE):
                every = cmp_ref[pl.ds(c, tm // CMP_STRIDE, stride=CMP_STRIDE), :].astype(BF16)
                kc_ref[g, :, c * HD:(c + 1) * HD] = every[:, 0:HD]
                vc_ref[g, :, c * HD:(c + 1) * HD] = every[:, HD:]
        gt_ref[...] = jax.nn.sigmoid(jnp.dot(h, wg_ref[...], preferred_element_type=F32)).T


def _kv_proj(x2, b_, s_, g, w_kv, w_gate, tm=512):
    n_t = s_ // tm
    assert KV_PAD == tm

    def tok(b, s):
        return jnp.maximum(s - 1, 0)

    return pl.pallas_call(
        _kv_proj_body,
        grid=(b_, n_t + 1),
        in_specs=[
            pl.BlockSpec((tm, D_MODEL), lambda b, s: (b * n_t + tok(b, s), 0)),
            pl.BlockSpec((1, D_MODEL), lambda b, s: (0, 0)),
            pl.BlockSpec((D_MODEL, N_KVCOL), lambda b, s: (0, 0), pipeline_mode=pl.Buffered(1)),
            pl.BlockSpec((D_MODEL, LANES), lambda b, s: (0, 0), pipeline_mode=pl.Buffered(1)),
        ],
        out_specs=[
            pl.BlockSpec((None, N_KV, tm // CMP_STRIDE, CMP_STRIDE * HD), lambda b, s: (b, 0, tok(b, s), 0)),
            pl.BlockSpec((None, N_KV, tm // CMP_STRIDE, CMP_STRIDE * HD), lambda b, s: (b, 0, tok(b, s), 0)),
            pl.BlockSpec((None, N_KV, tm, 2 * HD), lambda b, s: (b, 0, s, 0)),
            pl.BlockSpec((None, N_KV, HD + ONES_ROWS, tm), lambda b, s: (b, 0, 0, s)),
            pl.BlockSpec((None, N_KV, tm, 2 * HD), lambda b, s: (b, 0, s, 0)),
            pl.BlockSpec((None, N_KV, HD + ONES_ROWS, tm), lambda b, s: (b, 0, 0, s)),
            pl.BlockSpec((None, LANES, tm), lambda b, s: (b, 0, tok(b, s))),
        ],
        out_shape=[
            jax.ShapeDtypeStruct((b_, N_KV, s_ // CMP_STRIDE, CMP_STRIDE * HD), BF16),
            jax.ShapeDtypeStruct((b_, N_KV, s_ // CMP_STRIDE, CMP_STRIDE * HD), BF16),
            jax.ShapeDtypeStruct((b_, N_KV, KV_PAD + s_, 2 * HD), BF16),
            jax.ShapeDtypeStruct((b_, N_KV, HD + ONES_ROWS, KV_PAD + s_), BF16),
            jax.ShapeDtypeStruct((b_, N_KV, KV_PAD + s_, 2 * HD), BF16),
            jax.ShapeDtypeStruct((b_, N_KV, HD + ONES_ROWS, KV_PAD + s_), BF16),
            jax.ShapeDtypeStruct((b_, LANES, s_), F32),
        ],
        scratch_shapes=[pltpu.VMEM((tm, 2 * HD), F32)],
        compiler_params=_params(("parallel", "arbitrary")),
        name="kv_proj",
    )(x2, g, w_kv, w_gate)


def _rel_bucket(dist):
    dist = jnp.maximum(dist, 0)
    log_ratio = jnp.log(jnp.maximum(dist, 1).astype(F32) / MAX_EXACT) / math.log(REL_MAX_DIST / MAX_EXACT)
    large = MAX_EXACT + (log_ratio * (N_BUCKETS - MAX_EXACT)).astype(jnp.int32)
    return jnp.where(dist < MAX_EXACT, dist, jnp.minimum(large, N_BUCKETS - 1))


def _bias_body(tab_ref, bt_ref, bc_ref, tt_ref, bco_ref, *, n_cmp):
    h = pl.program_id(0)

    def lookup(bk):
        acc = jnp.zeros(bk.shape, F32)
        for b in range(N_BUCKETS):
            acc = jnp.where(bk == b, tab_ref[b, h], acc)
        return acc

    kk = lax.broadcasted_iota(jnp.int32, (LANES, LANES), 0)
    qq = lax.broadcasted_iota(jnp.int32, (LANES, LANES), 1)
    far = lookup(bt_ref[2])
    tt_ref[0] = (lookup(bt_ref[1]) - far) * LOG2E
    tt_ref[1] = jnp.where(qq >= kk, (lookup(bt_ref[0]) - far) * LOG2E, NEG_INF)
    tt_ref[2] = jnp.where(kk > qq, 0.0, NEG_INF)
    for i in range(bc_ref.shape[0]):
        ok = (i * Q_BLOCK + qq - (kk * CMP_STRIDE + CMP_LEN - 1) >= 0) & (kk < n_cmp)
        bco_ref[i] = jnp.where(ok, lookup(bc_ref[i]), NEG_INF)


def _bias_tiles(rel_bias, n_qb):
    kk = jnp.arange(LANES)[:, None]
    qq = jnp.arange(LANES)[None, :]
    bt = jnp.stack([_rel_bucket(LANES * d + qq - kk) for d in range(3)]).astype(jnp.int32)
    cmp_end = CMP_STRIDE * jnp.arange(LANES) + CMP_LEN - 1
    t = (Q_BLOCK * jnp.arange(n_qb))[:, None, None] + qq[None]
    bc = _rel_bucket(t - cmp_end[None, :, None]).astype(jnp.int32)
    n_cmp = (n_qb * Q_BLOCK - CMP_LEN) // CMP_STRIDE + 1
    return pl.pallas_call(
        functools.partial(_bias_body, n_cmp=n_cmp),
        grid=(N_HEADS,),
        in_specs=[
            pl.BlockSpec(memory_space=pltpu.SMEM),
            pl.BlockSpec((3, LANES, LANES), lambda h: (0, 0, 0)),
            pl.BlockSpec((n_qb, LANES, LANES), lambda h: (0, 0, 0)),
        ],
        out_specs=[
            pl.BlockSpec((None, 3, LANES, LANES), lambda h: (h // Q_PER_KV, 0, 0, h % Q_PER_KV)),
            pl.BlockSpec((None, n_qb, LANES, LANES), lambda h: (h // Q_PER_KV, 0, 0, h % Q_PER_KV)),
        ],
        out_shape=[
            jax.ShapeDtypeStruct((N_KV, 3, LANES, Q_PER_KV * LANES), F32),
            jax.ShapeDtypeStruct((N_KV, n_qb, LANES, Q_PER_KV * LANES), F32),
        ],
        compiler_params=_params(("arbitrary",)),
        name="bias_tiles",
    )(rel_bias.astype(F32), bt, bc)


def _compress_body(k3_ref, v3_ref, pos_ref, w1_ref, w2_ref, ko_ref, vo_ref):
    half = CMP_STRIDE * HD
    for which, (src, dst) in enumerate(((k3_ref, ko_ref), (v3_ref, vo_ref))):
        x = src[...].astype(F32)
        pos = pos_ref[which]
        xa = (x + pos[:, :half]).astype(BF16)
        xb = (x + pos[:, half:]).astype(BF16)
        a = jnp.dot(xa, w1_ref[which, :half, :], preferred_element_type=F32)
        b = jnp.dot(xb, w1_ref[which, half:, :], preferred_element_type=F32)
        hid = jax.nn.gelu(a + pltpu.roll(b, LANES - 1, 0))
        dst[...] = jnp.dot(hid.astype(BF16), w2_ref[which], preferred_element_type=F32).astype(dst.dtype)


def _compress(k3, v3, pos, w1, w2):
    b_, g_, nm, width = k3.shape
    spec3 = pl.BlockSpec((None, None, nm, width), lambda b, g: (b, g, 0, 0))
    ospec = pl.BlockSpec((None, None, nm, HD), lambda b, g: (b, g, 0, 0))
    return pl.pallas_call(
        _compress_body,
        grid=(b_, g_),
        in_specs=[
            spec3, spec3,
            pl.BlockSpec((2, 1, CMP_LEN * HD), lambda b, g: (0, 0, 0)),
            pl.BlockSpec((2, CMP_LEN * HD, CMP_HID), lambda b, g: (0, 0, 0)),
            pl.BlockSpec((2, CMP_HID, HD), lambda b, g: (0, 0, 0)),
        ],
        out_specs=[ospec, ospec],
        out_shape=[jax.ShapeDtypeStruct((b_, g_, nm, HD), BF16)] * 2,
        compiler_params=_params(("parallel", "parallel")),
        name="compress",
    )(k3, v3, pos, w1, w2)


FAR_TILES = 2
PAD_COL = 32
PAIRS = 4
ONES_ROWS = 16


def _nsa_body(qa_ref, qb_ref, kc_ref, vct_ref, ksa_ref, vst_ref, kwa_ref, vwt_ref, gta_ref, gtb_ref,
              tt_ref, bca_ref, bcb_ref, ovt_ref, o_ref, sbuf_ref, qf_ref):
    p_id = pl.program_id(2)
    n_qb = 2 * PAIRS * pl.num_programs(2)
    nl = Q_PER_KV * LANES
    n_win = WINDOW // LANES + 1
    n_blk = ovt_ref.shape[0]
    max_chunks = sbuf_ref.shape[0] // PAIRS
    pad_tiles = KV_PAD // LANES
    j_io = lax.broadcasted_iota(jnp.int32, (n_blk, LANES), 0)
    q_io = lax.broadcasted_iota(jnp.int32, (n_blk, LANES), 1)
    col_w = lax.broadcasted_iota(jnp.int32, (nl, HD), 1)
    pad_cols = jnp.where(col_w == PAD_COL, NEG_INF, 0.0).astype(BF16)

    def compressed(qt, bc):
        q = jnp.concatenate([qt[:, r * HD:(r + 1) * HD] for r in range(Q_PER_KV)], axis=0) * SCALE
        s = lax.dot_general(kc_ref[...], q, _NT, preferred_element_type=F32) + bc
        m_c = jnp.max(s, axis=0, keepdims=True)
        p = jnp.exp(s - m_c)
        l_c = jnp.sum(p, axis=0, keepdims=True)
        p_c = p * (jnp.where(m_c > 0.5 * NEG_INF, 1.0, 0.0) / l_c)
        o_c = jnp.dot(vct_ref[...], p_c.astype(BF16), preferred_element_type=F32)
        return o_c, p_c, (q.astype(F32) * LOG2E).astype(BF16)

    def select(i, p_c, q2, which):
        p_sum = p_c[:, 0:LANES]
        for r in range(1, Q_PER_KV):
            p_sum = p_sum + p_c[:, r * LANES:(r + 1) * LANES]
        imp = jnp.dot(ovt_ref[...], p_sum, precision=lax.Precision.HIGHEST, preferred_element_type=F32)
        t_b = i * Q_BLOCK + q_io
        cur = lax.shift_right_logical(t_b, SEL_BLOCK.bit_length() - 1)
        forced = (j_io == 0) | (j_io == cur) | (j_io == cur - 1)
        valid = j_io * SEL_BLOCK <= t_b
        score = jnp.where(forced, FORCE, jnp.where(valid, imp, -FORCE))
        rank = jnp.zeros((n_blk, LANES), jnp.int32)
        for ii in range(n_blk):
            row = score[ii:ii + 1, :]
            beats = (row > score) | ((row == score) & (j_io > ii))
            rank = rank + beats.astype(jnp.int32)
        keep = (rank < min(SEL_TOP_N, n_blk)) & (score >= 0.0)

        def augment(kept):
            cols = jnp.concatenate([jnp.where(kept, 0.0, NEG_INF), jnp.full((SUBLANES, LANES), NEG_INF, F32),
                                    jnp.zeros((LANES - n_blk - SUBLANES, LANES), F32)], axis=0)
            cols_t = cols.T[:, 0:HD].astype(BF16)
            return jnp.concatenate([q2, jnp.concatenate([cols_t] * Q_PER_KV, axis=0)], axis=1)

        qf_ref[which] = augment(keep & (j_io < 2 * (i - 1)))
        return augment(keep)

    def softmax_av(tiles, vt):
        s_all = jnp.concatenate(tiles, axis=0)
        m = jnp.max(s_all, axis=0, keepdims=True)
        p = jnp.exp2(s_all - m)
        return m, jnp.dot(vt, p.astype(BF16), preferred_element_type=F32)

    def normalise(av):
        return av[0:HD] / av[HD:HD + 1]

    def logit_tiles(k_ref, qq, row0, n):
        s_all = lax.dot_general(k_ref[pl.ds(row0, n * LANES), :], qq, _NT, preferred_element_type=F32)
        return [s_all[t * LANES:(t + 1) * LANES] for t in range(n)]

    def window_logits(i, q2):
        r0 = pl.multiple_of((i + pad_tiles + 1 - n_win) * LANES, LANES)
        tw = logit_tiles(kwa_ref, jnp.concatenate([q2, pad_cols], axis=1), r0, n_win)
        tw[0] = tw[0] + tt_ref[2]
        tw[n_win - 2] = tw[n_win - 2] + tt_ref[0]
        tw[n_win - 1] = tw[n_win - 1] + tt_ref[1]
        return tw

    def window_softmax(i, tw):
        r0 = pl.multiple_of((i + pad_tiles + 1 - n_win) * LANES, LANES)
        return normalise(softmax_av(tw, vwt_ref[:, pl.ds(r0, n_win * LANES)])[1])

    def near_logits(i, q_near):
        r1 = pl.multiple_of((i + pad_tiles - 1) * LANES, LANES)
        tn = logit_tiles(ksa_ref, q_near, r1, 2)
        tn[0] = tn[0] + tt_ref[0]
        tn[1] = tn[1] + tt_ref[1]
        return tn

    def near_softmax(i, tn):
        r1 = pl.multiple_of((i + pad_tiles - 1) * LANES, LANES)
        return softmax_av(tn, vst_ref[:, pl.ds(r1, 2 * LANES)])

    def pair(u):
        v = PAIRS - 1 - u
        i_a = PAIRS * p_id + u
        i_b = n_qb - 1 - i_a
        rows_a = slice(u * Q_BLOCK, (u + 1) * Q_BLOCK)
        rows_b = slice(v * Q_BLOCK, (v + 1) * Q_BLOCK)
        oc_a, pc_a, q2_a = compressed(qa_ref[rows_a, :], bca_ref[u])
        oc_b, pc_b, q2_b = compressed(qb_ref[rows_b, :], bcb_ref[v])
        tw_a = window_logits(i_a, q2_a)
        tw_b = window_logits(i_b, q2_b)
        qn_a = select(i_a, pc_a, q2_a, 2 * u)
        ow_a = window_softmax(i_a, tw_a)
        qn_b = select(i_b, pc_b, q2_b, 2 * u + 1)
        ow_b = window_softmax(i_b, tw_b)
        tn_a = near_logits(i_a, qn_a)
        tn_b = near_logits(i_b, qn_b)
        mn_a, avn_a = near_softmax(i_a, tn_a)
        mn_b, avn_b = near_softmax(i_b, tn_b)

        n_a = (jnp.maximum(i_a - 1, 0) + FAR_TILES - 1) // FAR_TILES
        is_a, rows, mx = [], [], []
        for k in range(max_chunks):
            own = k < n_a
            c = jnp.where(own, k, k - n_a)
            r = pl.multiple_of((c * FAR_TILES + pad_tiles) * LANES, LANES)
            s_k = lax.dot_general(ksa_ref[pl.ds(r, FAR_TILES * LANES), :], qf_ref[2 * u + jnp.where(own, 0, 1)],
                                  _NT, preferred_element_type=F32)
            sbuf_ref[u * max_chunks + k] = s_k
            is_a.append(own)
            rows.append(r)
            mx.append(jnp.max(s_k, axis=0, keepdims=True))
        m_a, m_b = mn_a, mn_b
        for k in range(max_chunks):
            m_a = jnp.where(is_a[k], jnp.maximum(m_a, mx[k]), m_a)
            m_b = jnp.where(is_a[k], m_b, jnp.maximum(m_b, mx[k]))
        av_a = jnp.exp2(mn_a - m_a) * avn_a
        av_b = jnp.exp2(mn_b - m_b) * avn_b
        for k in range(max_chunks):
            p = jnp.exp2(sbuf_ref[u * max_chunks + k] - jnp.where(is_a[k], m_a, m_b))
            pv = jnp.dot(vst_ref[:, pl.ds(rows[k], FAR_TILES * LANES)], p.astype(BF16),
                         preferred_element_type=F32)
            av_a = av_a + jnp.where(is_a[k], pv, 0.0)
            av_b = av_b + jnp.where(is_a[k], 0.0, pv)

        def finish(half, gt_ref, w, o_c, o_s, o_w):
            def gate(branch):
                g_id = pl.program_id(1)
                start = pl.multiple_of(branch * N_HEADS + (g_id // 2) * SUBLANES, SUBLANES)
                tile = gt_ref[pl.ds(start, SUBLANES), :]
                rows_g = jnp.where(g_id % 2 == 0, tile[0:Q_PER_KV], tile[Q_PER_KV:])
                return jnp.concatenate([rows_g[r:r + 1, w * Q_BLOCK:(w + 1) * Q_BLOCK] for r in range(Q_PER_KV)],
                                       axis=1)

            o_t = gate(0) * o_c + gate(1) * o_s + gate(2) * o_w
            for pr in range(Q_PER_KV // 2):
                blk = jnp.concatenate([o_t[:, (2 * pr) * LANES:(2 * pr + 1) * LANES],
                                       o_t[:, (2 * pr + 1) * LANES:(2 * pr + 2) * LANES]], axis=0)
                o_ref[half, u, :, pr * LANES:(pr + 1) * LANES] = blk.T.astype(o_ref.dtype)

        finish(0, gta_ref, u, oc_a, normalise(av_a), ow_a)
        finish(1, gtb_ref, v, oc_b, normalise(av_b), ow_b)

    for u in range(PAIRS):
        pair(u)


def _nsa(proj, kc, vct, ksa, vst, kwa, vwt, gt, tt, bc, ovt, s_):
    b_, g_ = kwa.shape[:2]
    n_qb = s_ // Q_BLOCK
    n_p = n_qb // 2
    n_s = n_p // PAIRS
    rows = PAIRS * Q_BLOCK
    m = proj.shape[0]
    nl = Q_PER_KV * LANES
    qcol = SEG_Q * D_MODEL // (Q_PER_KV * HD)

    def far_chunks(i):
        return (max(i - 1, 0) + FAR_TILES - 1) // FAR_TILES

    max_chunks = far_chunks(0) + far_chunks(n_qb - 1)
    assert n_qb % (2 * PAIRS) == 0 and all(
        far_chunks(p) + far_chunks(n_qb - 1 - p) == max_chunks for p in range(n_p))

    def bg(shape):
        return pl.BlockSpec((None, None) + shape, lambda b, g, p: (b, g, 0, 0))

    def blk_a(p):
        return p

    def blk_b(p):
        return 2 * n_s - 1 - p

    def per_block(which):
        return [
            pl.BlockSpec((rows, Q_PER_KV * HD), lambda b, g, p: (b * 2 * n_s + which(p), qcol + g)),
            pl.BlockSpec((None, LANES, rows), lambda b, g, p: (b, 0, which(p))),
            pl.BlockSpec((None, PAIRS, LANES, nl), lambda b, g, p: (g, which(p), 0, 0)),
        ]

    qa, gta, bca = per_block(blk_a)
    qb, gtb, bcb = per_block(blk_b)
    out = pl.pallas_call(
        _nsa_body,
        grid=(b_, g_, n_s),
        in_specs=[
            qa, qb,
            bg((LANES, HD)), bg((HD, LANES)),
            bg((s_ + KV_PAD, 2 * HD)), bg((HD + ONES_ROWS, s_ + KV_PAD)),
            bg((s_ + KV_PAD, 2 * HD)), bg((HD + ONES_ROWS, s_ + KV_PAD)),
            gta, gtb,
            pl.BlockSpec((None, 3, LANES, nl), lambda b, g, p: (g, 0, 0, 0)),
            bca, bcb,
            pl.BlockSpec((s_ // SEL_BLOCK, LANES), lambda b, g, p: (0, 0)),
        ],
        out_specs=pl.BlockSpec((2, PAIRS, Q_BLOCK, Q_PER_KV * HD), lambda b, g, p: (0, b * n_s + p, 0, g)),
        out_shape=jax.ShapeDtypeStruct((2, b_ * n_p, Q_BLOCK, D_MODEL), BF16),
        scratch_shapes=[pltpu.VMEM((PAIRS * max_chunks, FAR_TILES * LANES, nl), F32),
                        pltpu.VMEM((2 * PAIRS, nl, 2 * HD), BF16)],
        compiler_params=_params(("parallel", "parallel", "arbitrary")),
        name="nsa",
    )(proj, proj, kc, vct, ksa, vst, kwa, vwt, gt, gt, tt, bc, bc, ovt)
    return out.reshape(m, D_MODEL)


def _attn_block_row(b, qb, n_b, n_qb):
    n_p = n_qb // 2
    upper = qb >= n_p
    return jnp.where(upper, n_b * n_p, 0) + b * n_p + jnp.where(upper, n_qb - 1 - qb, qb)


def _mix_body(gb_ref, gc_ref, va_ref, gch_ref, vah_ref, uc_ref, vc_ref, g0_ref, g1_ref, g2_ref, x_ref,
              ca_ref, sw_ref, sb_ref, lng_ref, lnb_ref, wa_ref, wb_ref, wc_ref, wo_ref, *rest):
    tm = gb_ref.shape[0]
    ob_refs = rest[:tm // Q_BLOCK]
    o_ref, xa_ref, mx_ref = rest[tm // Q_BLOCK:]
    first = pl.program_id(1) == 0

    halo = gch_ref[...].astype(F32) * vah_ref[...].astype(F32)
    xa_ref[0:HALO, :] = jnp.where(first, 0.0, halo)
    xa_ref[HALO:, :] = gc_ref[...].astype(F32) * va_ref[...].astype(F32)
    ca = ca_ref[...]
    row = lax.broadcasted_iota(jnp.int32, (CHUNK, CHUNK), 0)
    col = lax.broadcasted_iota(jnp.int32, (CHUNK, CHUNK), 1)
    sw = [jnp.where(row >= col, sw_ref[gi], 0.0).astype(BF16) for gi in range(SGU_GROUPS)]

    for r in range(0, tm, ROW_CHUNK):
        rows = slice(r, r + ROW_CHUNK)
        conv = (ca[0:1] * xa_ref[pl.ds(r + HALO - 2, ROW_CHUNK), :]
                + ca[1:2] * xa_ref[pl.ds(r + HALO - 1, ROW_CHUNK), :]
                + ca[2:3] * xa_ref[pl.ds(r + HALO, ROW_CHUNK), :])
        out_a = gb_ref[rows, :].astype(F32) * conv
        y_a = jnp.dot(out_a.astype(BF16), wa_ref[...], preferred_element_type=F32)

        v = vc_ref[rows, :].astype(F32)
        mu = jnp.mean(v, axis=-1, keepdims=True)
        vz = v - mu
        var = jnp.mean(vz * vz, axis=-1, keepdims=True)
        vn = (vz * lax.rsqrt(var + EPS) * lng_ref[...] + lnb_ref[...]).astype(BF16)
        for gi in range(SGU_GROUPS):
            cols = slice(gi * LANES, (gi + 1) * LANES)
            for c in range(0, ROW_CHUNK, CHUNK):
                mx_ref[r + c:r + c + CHUNK, cols] = (
                    jnp.dot(sw[gi], vn[c:c + CHUNK, cols], preferred_element_type=F32) + sb_ref[:, cols])
        out_c = uc_ref[rows, :].astype(F32) * mx_ref[rows, :]
        y_c = jnp.dot(out_c.astype(BF16), wc_ref[...], preferred_element_type=F32)

        out_b = jnp.concatenate([ob_refs[k][...] for k in range(r // Q_BLOCK, (r + ROW_CHUNK) // Q_BLOCK)], axis=0)
        y_b = jnp.dot(out_b, wb_ref[...], preferred_element_type=F32)

        merged = (g0_ref[rows, :].astype(F32) * y_a + g1_ref[rows, :].astype(F32) * y_b
                  + g2_ref[rows, :].astype(F32) * y_c)
        o_ref[rows, :] = x_ref[rows, :] + jnp.dot(merged.astype(BF16), wo_ref[...], preferred_element_type=F32)


def _mix(proj, out_b, x2, seq, conv_a, sgu_w, sgu_b2, ln_g, ln_b, w_a, w_b, w_c, w_o, tm=512):
    m = x2.shape[0]
    n_t = seq // tm
    hb = tm // HALO

    def seg(k):
        return pl.BlockSpec((tm, D_MODEL), lambda b, i: (b * n_t + i, k))

    def halo(k):
        return pl.BlockSpec((HALO, D_MODEL), lambda b, i: (jnp.maximum((b * n_t + i) * hb - 1, 0), k))

    def full(shape):
        return pl.BlockSpec(shape, lambda b, i: (0,) * len(shape))

    def attn(k):
        return pl.BlockSpec((Q_BLOCK, D_MODEL), lambda b, i: (
            _attn_block_row(b, i * (tm // Q_BLOCK) + k, m // seq, seq // Q_BLOCK), 0))

    row = pl.BlockSpec((tm, D_MODEL), lambda b, i: (b * n_t + i, 0))
    wspec = pl.BlockSpec((D_MODEL, D_MODEL), lambda b, i: (0, 0), pipeline_mode=pl.Buffered(1))
    return pl.pallas_call(
        _mix_body,
        grid=(m // seq, n_t),
        in_specs=[
            seg(SEG_GB), seg(SEG_GC), seg(SEG_VA), halo(SEG_GC), halo(SEG_VA), seg(SEG_U), seg(SEG_V),
            seg(SEG_G0), seg(SEG_G0 + 1), seg(SEG_G0 + 2), row,
            full((CONV_W, D_MODEL)), full((SGU_GROUPS, CHUNK, CHUNK)), full((CHUNK, D_MODEL)),
            full((1, D_MODEL)), full((1, D_MODEL)), wspec, wspec, wspec, wspec,
        ] + [attn(k) for k in range(tm // Q_BLOCK)],
        out_specs=row,
        out_shape=jax.ShapeDtypeStruct((m, D_MODEL), F32),
        scratch_shapes=[pltpu.VMEM((tm + HALO, D_MODEL), F32), pltpu.VMEM((tm, D_MODEL), F32)],
        compiler_params=_params(("parallel", "arbitrary")),
        name="mix",
    )(proj, proj, proj, proj, proj, proj, proj, proj, proj, proj, x2,
      conv_a, sgu_w, sgu_b2, ln_g, ln_b, w_a, w_b, w_c, w_o, *([out_b] * (tm // Q_BLOCK)))


def _ffn_body(x_ref, xh_ref, g_ref, wg_ref, wv_ref, cw_ref, wd_ref, fg_ref, o_ref, h_ref, gt_ref, acc_ref, *,
              tiles_per_seq, final):
    i = pl.program_id(0)
    j = pl.program_id(1)
    tm = x_ref.shape[0]

    @pl.when(j == 0)
    def _():
        h_ref[0:HALO, :] = _rms(xh_ref[...], g_ref[...]).astype(BF16)
        h_ref[HALO:, :] = _rms(x_ref[...], g_ref[...]).astype(BF16)
        acc_ref[...] = jnp.zeros_like(acc_ref)

    gate = jnp.dot(h_ref[...], wg_ref[...], preferred_element_type=F32)
    rows = lax.broadcasted_iota(jnp.int32, gate.shape, 0)
    seq_start = (i % tiles_per_seq) == 0
    gt_ref[...] = jnp.where(seq_start & (rows < HALO), 0.0, gate)
    cw = cw_ref[...]
    conv = (cw[0:1] * gt_ref[pl.ds(HALO - 2, tm), :] + cw[1:2] * gt_ref[pl.ds(HALO - 1, tm), :]
            + cw[2:3] * gt_ref[pl.ds(HALO, tm), :])
    val = jnp.dot(h_ref[HALO:, :], wv_ref[...], preferred_element_type=F32)
    act = (jax.nn.gelu(conv) * val).astype(BF16)
    acc_ref[...] += jnp.dot(act, wd_ref[...], preferred_element_type=F32)

    @pl.when(j == pl.num_programs(1) - 1)
    def _():
        y = x_ref[...] + acc_ref[...]
        o_ref[...] = _rms(y, fg_ref[...]) if final else y


def _ffn(x2, seq, g, w_up, conv_w, w_down, final_g, final, tm=1024, tf=512):
    m = x2.shape[0]
    n_f = D_FF // tf
    w_up = jnp.stack([w_up[:, k * tf:(k + 1) * tf] for k in range(2 * n_f)]).astype(BF16)
    hb = tm // HALO
    return pl.pallas_call(
        functools.partial(_ffn_body, tiles_per_seq=seq // tm, final=final),
        grid=(m // tm, n_f),
        in_specs=[
            pl.BlockSpec((tm, D_MODEL), lambda i, j: (i, 0)),
            pl.BlockSpec((HALO, D_MODEL), lambda i, j: (jnp.maximum(i * hb - 1, 0), 0)),
            pl.BlockSpec((1, D_MODEL), lambda i, j: (0, 0)),
            pl.BlockSpec((None, D_MODEL, tf), lambda i, j: (j, 0, 0)),
            pl.BlockSpec((None, D_MODEL, tf), lambda i, j: (n_f + j, 0, 0)),
            pl.BlockSpec((CONV_W, tf), lambda i, j: (0, j)),
            pl.BlockSpec((tf, D_MODEL), lambda i, j: (j, 0)),
            pl.BlockSpec((1, D_MODEL), lambda i, j: (0, 0)),
        ],
        out_specs=pl.BlockSpec((tm, D_MODEL), lambda i, j: (i, 0)),
        out_shape=jax.ShapeDtypeStruct((m, D_MODEL), F32),
        scratch_shapes=[
            pltpu.VMEM((tm + HALO, D_MODEL), BF16),
            pltpu.VMEM((tm + HALO, tf), F32),
            pltpu.VMEM((tm, D_MODEL), F32),
        ],
        compiler_params=_params(("parallel", "arbitrary")),
        name="ffn",
    )(x2, x2, g, w_up, w_up, conv_w, w_down, final_g)


def _split_w_in(w_in):
    c = [0]
    for sz in [D_MODEL] * 4 + [N_KV * HD] * 6 + [3 * N_HEADS] + [D_MODEL] * 5:
        c.append(c[-1] + sz)
    starts = c[0:4] + c[11:16]
    main = jnp.stack([w_in[:, a:a + D_MODEL] for a in starts]).astype(BF16)
    kv = w_in[:, c[4]:c[10]].reshape(D_MODEL, 6, N_KV, HD)[:, jnp.array([2, 3, 4, 5, 0, 1])]
    kv = kv.transpose(0, 2, 1, 3).reshape(D_MODEL, N_KVCOL)
    gate = jnp.pad(w_in[:, c[10]:c[11]], ((0, 0), (0, LANES - 3 * N_HEADS)))
    return main, kv.astype(BF16), gate.astype(BF16)


def _mixer_layer(x2, b_, s_, tt, bc, ovt, norm_g, w_in, conv_a, cmp_pos, cmp_w1, cmp_w2, sgu_w, sgu_b,
                 sgu_norm_g, sgu_norm_b, w_br_a, w_br_b, w_br_c, w_o):
    w_main, w_kv, w_gate = _split_w_in(w_in)
    norm_g = norm_g.reshape(1, D_MODEL)
    proj = _in_proj(x2, norm_g, w_main)
    kc, vc, ksa, vst, kwa, vwt, gt = _kv_proj(x2, b_, s_, norm_g, w_kv, w_gate)

    k_cmp, v_cmp = _compress(kc, vc, cmp_pos.reshape(2, 1, CMP_LEN * HD),
                             cmp_w1.astype(BF16), cmp_w2.astype(BF16))
    out_b = _nsa(proj, k_cmp, v_cmp.transpose(0, 1, 3, 2), ksa, vst, kwa, vwt, gt, tt, bc, ovt, s_)

    sgu_b2 = jnp.broadcast_to(sgu_b.T[:, :, None], (CHUNK, SGU_GROUPS, D_MODEL // SGU_GROUPS)).reshape(CHUNK, D_MODEL)
    return _mix(proj, out_b, x2, s_, conv_a, sgu_w, sgu_b2,
                sgu_norm_g.reshape(1, D_MODEL), sgu_norm_b.reshape(1, D_MODEL),
                w_br_a.astype(BF16), w_br_b.astype(BF16), w_br_c.astype(BF16), w_o.astype(BF16))


def _overlap_t(s_):
    n_blk = s_ // SEL_BLOCK
    cmp_start = jnp.arange(LANES) * CMP_STRIDE
    cmp_end = cmp_start + CMP_LEN - 1
    blk_start = jnp.arange(n_blk) * SEL_BLOCK
    n_cmp = (s_ - CMP_LEN) // CMP_STRIDE + 1
    ov = ((cmp_start[None, :] < blk_start[:, None] + SEL_BLOCK) & (cmp_end[None, :] >= blk_start[:, None])
          & (jnp.arange(LANES)[None, :] < n_cmp))
    return ov.astype(F32)


def kernel(x, rel_bias, norm_mix, w_in, conv_a, cmp_pos, cmp_w1, cmp_w2, sgu_w, sgu_b, sgu_norm_g, sgu_norm_b,
           w_br_a, w_br_b, w_br_c, w_o, norm_ffn, ffn_w_up, ffn_conv, ffn_w_down, norm_final):
    b_, s_, _ = x.shape
    depth = w_in.shape[0]
    x2 = x.reshape(b_ * s_, D_MODEL)
    tt, bc = _bias_tiles(rel_bias, s_ // Q_BLOCK)
    ovt = _overlap_t(s_)
    for l in range(depth):
        x2 = _mixer_layer(x2, b_, s_, tt, bc, ovt, norm_mix[l], w_in[l], conv_a[l], cmp_pos[l], cmp_w1[l],
                          cmp_w2[l], sgu_w[l], sgu_b[l], sgu_norm_g[l], sgu_norm_b[l],
                          w_br_a[l], w_br_b[l], w_br_c[l], w_o[l])
        x2 = _ffn(x2, s_, norm_ffn[l].reshape(1, D_MODEL), ffn_w_up[l], ffn_conv[l], ffn_w_down[l].astype(BF16),
                  norm_final.reshape(1, D_MODEL), final=(l == depth - 1))
    return x2.reshape(b_, s_, D_MODEL)
```

```python
import functools
import math

import jax
import jax.numpy as jnp
from jax import lax
from jax.experimental import pallas as pl
from jax.experimental.pallas import tpu as pltpu

D_MODEL = 1024
HD = 64
N_HEADS = D_MODEL // HD
N_KV = 4
Q_PER_KV = N_HEADS // N_KV
SGU_GROUPS = 8
CHUNK = 128
CONV_W = 3
CMP_LEN = 32
CMP_STRIDE = 16
CMP_HID = 2 * HD
SEL_BLOCK = 64
SEL_TOP_N = 8
WINDOW = 512
Q_BLOCK = 128
D_FF = 3 * D_MODEL
N_BUCKETS = 32
MAX_EXACT = N_BUCKETS // 2
REL_MAX_DIST = 128
SCALE = HD ** -0.5
EPS = 1e-6
NEG_INF = -1e30
FORCE = 1e9
LOG2E = 1.4426950408889634

LANES = 128
SUBLANES = 8
ROW_CHUNK = 256
HALO = 16
N_SEG = 9
SEG_GB, SEG_GC, SEG_VA, SEG_Q, SEG_U, SEG_V, SEG_G0 = 0, 1, 2, 3, 4, 5, 6
N_MAIN = N_SEG * D_MODEL
N_KVCOL = 6 * N_KV * HD
KV_PAD = WINDOW
VMEM_LIMIT = 56 * 1024 * 1024

BF16 = jnp.bfloat16
F32 = jnp.float32
_NT = (((1,), (1,)), ((), ()))


def _params(sem):
    return pltpu.CompilerParams(dimension_semantics=sem, vmem_limit_bytes=VMEM_LIMIT)


def _rms(x, g):
    return x * lax.rsqrt(jnp.mean(x * x, axis=-1, keepdims=True) + EPS) * g


def _in_proj_body(x_ref, g_ref, w_ref, o_ref, h_ref, *, gelu_tiles, sigmoid_tiles):
    j = pl.program_id(1)

    @pl.when(j == 0)
    def _():
        h_ref[...] = _rms(x_ref[...], g_ref[...]).astype(BF16)

    def tile(act):
        for r in range(0, h_ref.shape[0], ROW_CHUNK):
            y = jnp.dot(h_ref[r:r + ROW_CHUNK, :], w_ref[...], preferred_element_type=F32)
            o_ref[r:r + ROW_CHUNK, :] = act(y).astype(o_ref.dtype)

    is_gelu = (j >= gelu_tiles[0]) & (j < gelu_tiles[1])
    is_sigmoid = (j >= sigmoid_tiles[0]) & (j < sigmoid_tiles[1])
    pl.when(is_gelu)(lambda: tile(jax.nn.gelu))
    pl.when(is_sigmoid)(lambda: tile(jax.nn.sigmoid))
    pl.when(jnp.logical_not(is_gelu | is_sigmoid))(lambda: tile(lambda y: y))


def _in_proj(x2, g, w_main, tm=2048):
    m = x2.shape[0]
    tn = D_MODEL
    body = functools.partial(_in_proj_body, gelu_tiles=(SEG_U, SEG_V + 1), sigmoid_tiles=(SEG_G0, SEG_G0 + 3))
    return pl.pallas_call(
        body,
        grid=(m // tm, N_SEG),
        in_specs=[
            pl.BlockSpec((tm, D_MODEL), lambda i, j: (i, 0)),
            pl.BlockSpec((1, D_MODEL), lambda i, j: (0, 0)),
            pl.BlockSpec((None, D_MODEL, tn), lambda i, j: (j, 0, 0)),
        ],
        out_specs=pl.BlockSpec((tm, tn), lambda i, j: (i, j)),
        out_shape=jax.ShapeDtypeStruct((m, N_MAIN), BF16),
        scratch_shapes=[pltpu.VMEM((tm, D_MODEL), BF16)],
        compiler_params=_params(("parallel", "arbitrary")),
        name="in_proj",
    )(x2, g, w_main)


def _kv_proj_body(x_ref, g_ref, w_ref, wg_ref, kc_ref, vc_ref, ksa_ref, vst_ref, kwa_ref, vwt_ref, gt_ref, cmp_ref):
    s_id = pl.program_id(1)
    tm = x_ref.shape[0]
    lane = lax.broadcasted_iota(jnp.int32, (tm, LANES), 1)
    ones_rows = (lax.broadcasted_iota(jnp.int32, (ONES_ROWS, tm), 0) == 0).astype(BF16)

    @pl.when(s_id == 0)
    def _():
        pad_keys = jnp.where(lane == HD + PAD_COL, 1.0, 0.0).astype(BF16)
        for g in range(N_KV):
            ksa_ref[g] = pad_keys
            kwa_ref[g] = pad_keys
            vst_ref[g] = jnp.zeros(vst_ref.shape[1:], BF16)
            vwt_ref[g] = jnp.zeros(vwt_ref.shape[1:], BF16)

    @pl.when(s_id > 0)
    def _():
        h = _rms(x_ref[...], g_ref[...]).astype(BF16)
        y = jnp.dot(h, w_ref[...], preferred_element_type=F32)
        tok = (s_id - 1) * tm + lax.broadcasted_iota(jnp.int32, (tm, LANES), 0)
        blk_id = jnp.where(lane - HD == tok // SEL_BLOCK, 1.0, 0.0)
        for g in range(N_KV):
            base = g * 6 * HD
            sel = y[:, base:base + 2 * HD]
            win = y[:, base + 2 * HD:base + 4 * HD]
            cmp = y[:, base + 4 * HD:base + 6 * HD]
            ksa_ref[g] = jnp.where(lane < HD, sel, blk_id).astype(BF16)
            kwa_ref[g] = jnp.where(lane < HD, win, 0.0).astype(BF16)
            vst_ref[g, 0:HD, :] = sel.T[HD:].astype(BF16)
            vst_ref[g, HD:, :] = ones_rows
            vwt_ref[g, 0:HD, :] = win.T[HD:].astype(BF16)
            vwt_ref[g, HD:, :] = ones_rows
            cmp_ref[...] = cmp
            for c in range(CMP_STRIDE):
                every = cmp_ref[pl.ds(c, tm // CMP_STRIDE, stride=CMP_STRIDE), :].astype(BF16)
                kc_ref[g, :, c * HD:(c + 1) * HD] = every[:, 0:HD]
                vc_ref[g, :, c * HD:(c + 1) * HD] = every[:, HD:]
        gt_ref[...] = jax.nn.sigmoid(jnp.dot(h, wg_ref[...], preferred_element_type=F32)).T


def _kv_proj(x2, b_, s_, g, w_kv, w_gate, tm=512):
    n_t = s_ // tm
    assert KV_PAD == tm

    def tok(b, s):
        return jnp.maximum(s - 1, 0)

    return pl.pallas_call(
        _kv_proj_body,
        grid=(b_, n_t + 1),
        in_specs=[
            pl.BlockSpec((tm, D_MODEL), lambda b, s: (b * n_t + tok(b, s), 0)),
            pl.BlockSpec((1, D_MODEL), lambda b, s: (0, 0)),
            pl.BlockSpec((D_MODEL, N_KVCOL), lambda b, s: (0, 0), pipeline_mode=pl.Buffered(1)),
            pl.BlockSpec((D_MODEL, LANES), lambda b, s: (0, 0), pipeline_mode=pl.Buffered(1)),
        ],
        out_specs=[
            pl.BlockSpec((None, N_KV, tm // CMP_STRIDE, CMP_STRIDE * HD), lambda b, s: (b, 0, tok(b, s), 0)),
            pl.BlockSpec((None, N_KV, tm // CMP_STRIDE, CMP_STRIDE * HD), lambda b, s: (b, 0, tok(b, s), 0)),
            pl.BlockSpec((None, N_KV, tm, 2 * HD), lambda b, s: (b, 0, s, 0)),
            pl.BlockSpec((None, N_KV, HD + ONES_ROWS, tm), lambda b, s: (b, 0, 0, s)),
            pl.BlockSpec((None, N_KV, tm, 2 * HD), lambda b, s: (b, 0, s, 0)),
            pl.BlockSpec((None, N_KV, HD + ONES_ROWS, tm), lambda b, s: (b, 0, 0, s)),
            pl.BlockSpec((None, LANES, tm), lambda b, s: (b, 0, tok(b, s))),
        ],
        out_shape=[
            jax.ShapeDtypeStruct((b_, N_KV, s_ // CMP_STRIDE, CMP_STRIDE * HD), BF16),
            jax.ShapeDtypeStruct((b_, N_KV, s_ // CMP_STRIDE, CMP_STRIDE * HD), BF16),
            jax.ShapeDtypeStruct((b_, N_KV, KV_PAD + s_, 2 * HD), BF16),
            jax.ShapeDtypeStruct((b_, N_KV, HD + ONES_ROWS, KV_PAD + s_), BF16),
            jax.ShapeDtypeStruct((b_, N_KV, KV_PAD + s_, 2 * HD), BF16),
            jax.ShapeDtypeStruct((b_, N_KV, HD + ONES_ROWS, KV_PAD + s_), BF16),
            jax.ShapeDtypeStruct((b_, LANES, s_), F32),
        ],
        scratch_shapes=[pltpu.VMEM((tm, 2 * HD), F32)],
        compiler_params=_params(("parallel", "arbitrary")),
        name="kv_proj",
    )(x2, g, w_kv, w_gate)


def _rel_bucket(dist):
    dist = jnp.maximum(dist, 0)
    log_ratio = jnp.log(jnp.maximum(dist, 1).astype(F32) / MAX_EXACT) / math.log(REL_MAX_DIST / MAX_EXACT)
    large = MAX_EXACT + (log_ratio * (N_BUCKETS - MAX_EXACT)).astype(jnp.int32)
    return jnp.where(dist < MAX_EXACT, dist, jnp.minimum(large, N_BUCKETS - 1))


def _bias_body(tab_ref, bt_ref, bc_ref, tt_ref, bco_ref, *, n_cmp):
    h = pl.program_id(0)

    def lookup(bk):
        acc = jnp.zeros(bk.shape, F32)
        for b in range(N_BUCKETS):
            acc = jnp.where(bk == b, tab_ref[b, h], acc)
        return acc

    kk = lax.broadcasted_iota(jnp.int32, (LANES, LANES), 0)
    qq = lax.broadcasted_iota(jnp.int32, (LANES, LANES), 1)
    far = lookup(bt_ref[2])
    tt_ref[0] = (lookup(bt_ref[1]) - far) * LOG2E
    tt_ref[1] = jnp.where(qq >= kk, (lookup(bt_ref[0]) - far) * LOG2E, NEG_INF)
    tt_ref[2] = jnp.where(kk > qq, 0.0, NEG_INF)
    for i in range(bc_ref.shape[0]):
        ok = (i * Q_BLOCK + qq - (kk * CMP_STRIDE + CMP_LEN - 1) >= 0) & (kk < n_cmp)
        bco_ref[i] = jnp.where(ok, lookup(bc_ref[i]), NEG_INF)


def _bias_tiles(rel_bias, n_qb):
    kk = jnp.arange(LANES)[:, None]
    qq = jnp.arange(LANES)[None, :]
    bt = jnp.stack([_rel_bucket(LANES * d + qq - kk) for d in range(3)]).astype(jnp.int32)
    cmp_end = CMP_STRIDE * jnp.arange(LANES) + CMP_LEN - 1
    t = (Q_BLOCK * jnp.arange(n_qb))[:, None, None] + qq[None]
    bc = _rel_bucket(t - cmp_end[None, :, None]).astype(jnp.int32)
    n_cmp = (n_qb * Q_BLOCK - CMP_LEN) // CMP_STRIDE + 1
    return pl.pallas_call(
        functools.partial(_bias_body, n_cmp=n_cmp),
        grid=(N_HEADS,),
        in_specs=[
            pl.BlockSpec(memory_space=pltpu.SMEM),
            pl.BlockSpec((3, LANES, LANES), lambda h: (0, 0, 0)),
            pl.BlockSpec((n_qb, LANES, LANES), lambda h: (0, 0, 0)),
        ],
        out_specs=[
            pl.BlockSpec((None, 3, LANES, LANES), lambda h: (h // Q_PER_KV, 0, 0, h % Q_PER_KV)),
            pl.BlockSpec((None, n_qb, LANES, LANES), lambda h: (h // Q_PER_KV, 0, 0, h % Q_PER_KV)),
        ],
        out_shape=[
            jax.ShapeDtypeStruct((N_KV, 3, LANES, Q_PER_KV * LANES), F32),
            jax.ShapeDtypeStruct((N_KV, n_qb, LANES, Q_PER_KV * LANES), F32),
        ],
        compiler_params=_params(("arbitrary",)),
        name="bias_tiles",
    )(rel_bias.astype(F32), bt, bc)


def _compress_body(k3_ref, v3_ref, pos_ref, w1_ref, w2_ref, ko_ref, vo_ref):
    half = CMP_STRIDE * HD
    for which, (src, dst) in enumerate(((k3_ref, ko_ref), (v3_ref, vo_ref))):
        x = src[...].astype(F32)
        pos = pos_ref[which]
        xa = (x + pos[:, :half]).astype(BF16)
        xb = (x + pos[:, half:]).astype(BF16)
        a = jnp.dot(xa, w1_ref[which, :half, :], preferred_element_type=F32)
        b = jnp.dot(xb, w1_ref[which, half:, :], preferred_element_type=F32)
        hid = jax.nn.gelu(a + pltpu.roll(b, LANES - 1, 0))
        dst[...] = jnp.dot(hid.astype(BF16), w2_ref[which], preferred_element_type=F32).astype(dst.dtype)


def _compress(k3, v3, pos, w1, w2):
    b_, g_, nm, width = k3.shape
    spec3 = pl.BlockSpec((None, None, nm, width), lambda b, g: (b, g, 0, 0))
    ospec = pl.BlockSpec((None, None, nm, HD), lambda b, g: (b, g, 0, 0))
    return pl.pallas_call(
        _compress_body,
        grid=(b_, g_),
        in_specs=[
            spec3, spec3,
            pl.BlockSpec((2, 1, CMP_LEN * HD), lambda b, g: (0, 0, 0)),
            pl.BlockSpec((2, CMP_LEN * HD, CMP_HID), lambda b, g: (0, 0, 0)),
            pl.BlockSpec((2, CMP_HID, HD), lambda b, g: (0, 0, 0)),
        ],
        out_specs=[ospec, ospec],
        out_shape=[jax.ShapeDtypeStruct((b_, g_, nm, HD), BF16)] * 2,
        compiler_params=_params(("parallel", "parallel")),
        name="compress",
    )(k3, v3, pos, w1, w2)


FAR_TILES = 2
PAD_COL = 32
PAIRS = 8
ONES_ROWS = 16


def _nsa_body(qa_ref, qb_ref, kc_ref, vct_ref, ksa_ref, vst_ref, kwa_ref, vwt_ref, gta_ref, gtb_ref,
              tt_ref, bca_ref, bcb_ref, ovt_ref, o_ref, sbuf_ref, qf_ref):
    p_id = pl.program_id(2)
    n_qb = 2 * PAIRS * pl.num_programs(2)
    nl = Q_PER_KV * LANES
    n_win = WINDOW // LANES + 1
    n_blk = ovt_ref.shape[0]
    max_chunks = sbuf_ref.shape[0] // PAIRS
    pad_tiles = KV_PAD // LANES
    j_io = lax.broadcasted_iota(jnp.int32, (n_blk, LANES), 0)
    q_io = lax.broadcasted_iota(jnp.int32, (n_blk, LANES), 1)
    col_w = lax.broadcasted_iota(jnp.int32, (nl, HD), 1)
    pad_cols = jnp.where(col_w == PAD_COL, NEG_INF, 0.0).astype(BF16)

    def compressed(qt, bc):
        q = jnp.concatenate([qt[:, r * HD:(r + 1) * HD] for r in range(Q_PER_KV)], axis=0) * SCALE
        s = lax.dot_general(kc_ref[...], q, _NT, preferred_element_type=F32) + bc
        m_c = jnp.max(s, axis=0, keepdims=True)
        p = jnp.exp(s - m_c)
        l_c = jnp.sum(p, axis=0, keepdims=True)
        p_c = p * (jnp.where(m_c > 0.5 * NEG_INF, 1.0, 0.0) / l_c)
        o_c = jnp.dot(vct_ref[...], p_c.astype(BF16), preferred_element_type=F32)
        return o_c, p_c, (q.astype(F32) * LOG2E).astype(BF16)

    def select(i, p_c, q2, which):
        p_sum = p_c[:, 0:LANES]
        for r in range(1, Q_PER_KV):
            p_sum = p_sum + p_c[:, r * LANES:(r + 1) * LANES]
        imp = jnp.dot(ovt_ref[...], p_sum, precision=lax.Precision.HIGHEST, preferred_element_type=F32)
        t_b = i * Q_BLOCK + q_io
        cur = lax.shift_right_logical(t_b, SEL_BLOCK.bit_length() - 1)
        forced = (j_io == 0) | (j_io == cur) | (j_io == cur - 1)
        valid = j_io * SEL_BLOCK <= t_b
        score = jnp.where(forced, FORCE, jnp.where(valid, imp, -FORCE))
        rank = jnp.zeros((n_blk, LANES), jnp.int32)
        for ii in range(n_blk):
            row = score[ii:ii + 1, :]
            beats = (row > score) | ((row == score) & (j_io > ii))
            rank = rank + beats.astype(jnp.int32)
        keep = (rank < min(SEL_TOP_N, n_blk)) & (score >= 0.0)

        def augment(kept):
            cols = jnp.concatenate([jnp.where(kept, 0.0, NEG_INF), jnp.full((SUBLANES, LANES), NEG_INF, F32),
                                    jnp.zeros((LANES - n_blk - SUBLANES, LANES), F32)], axis=0)
            cols_t = cols.T[:, 0:HD].astype(BF16)
            return jnp.concatenate([q2, jnp.concatenate([cols_t] * Q_PER_KV, axis=0)], axis=1)

        qf_ref[which] = augment(keep & (j_io < 2 * (i - 1)))
        return augment(keep)

    def softmax_av(tiles, vt):
        s_all = jnp.concatenate(tiles, axis=0)
        m = jnp.max(s_all, axis=0, keepdims=True)
        p = jnp.exp2(s_all - m)
        return m, jnp.dot(vt, p.astype(BF16), preferred_element_type=F32)

    def normalise(av):
        return av[0:HD] / av[HD:HD + 1]

    def logit_tiles(k_ref, qq, row0, n):
        s_all = lax.dot_general(k_ref[pl.ds(row0, n * LANES), :], qq, _NT, preferred_element_type=F32)
        return [s_all[t * LANES:(t + 1) * LANES] for t in range(n)]

    def window_logits(i, q2):
        r0 = pl.multiple_of((i + pad_tiles + 1 - n_win) * LANES, LANES)
        tw = logit_tiles(kwa_ref, jnp.concatenate([q2, pad_cols], axis=1), r0, n_win)
        tw[0] = tw[0] + tt_ref[2]
        tw[n_win - 2] = tw[n_win - 2] + tt_ref[0]
        tw[n_win - 1] = tw[n_win - 1] + tt_ref[1]
        return tw

    def window_softmax(i, tw):
        r0 = pl.multiple_of((i + pad_tiles + 1 - n_win) * LANES, LANES)
        return normalise(softmax_av(tw, vwt_ref[:, pl.ds(r0, n_win * LANES)])[1])

    def near_logits(i, q_near):
        r1 = pl.multiple_of((i + pad_tiles - 1) * LANES, LANES)
        tn = logit_tiles(ksa_ref, q_near, r1, 2)
        tn[0] = tn[0] + tt_ref[0]
        tn[1] = tn[1] + tt_ref[1]
        return tn

    def near_softmax(i, tn):
        r1 = pl.multiple_of((i + pad_tiles - 1) * LANES, LANES)
        return softmax_av(tn, vst_ref[:, pl.ds(r1, 2 * LANES)])

    def pair(u):
        v = PAIRS - 1 - u
        i_a = PAIRS * p_id + u
        i_b = n_qb - 1 - i_a
        rows_a = slice(u * Q_BLOCK, (u + 1) * Q_BLOCK)
        rows_b = slice(v * Q_BLOCK, (v + 1) * Q_BLOCK)
        oc_a, pc_a, q2_a = compressed(qa_ref[rows_a, :], bca_ref[u])
        oc_b, pc_b, q2_b = compressed(qb_ref[rows_b, :], bcb_ref[v])
        tw_a = window_logits(i_a, q2_a)
        tw_b = window_logits(i_b, q2_b)
        qn_a = select(i_a, pc_a, q2_a, 2 * u)
        ow_a = window_softmax(i_a, tw_a)
        qn_b = select(i_b, pc_b, q2_b, 2 * u + 1)
        ow_b = window_softmax(i_b, tw_b)
        tn_a = near_logits(i_a, qn_a)
        tn_b = near_logits(i_b, qn_b)
        mn_a, avn_a = near_softmax(i_a, tn_a)
        mn_b, avn_b = near_softmax(i_b, tn_b)

        n_a = (jnp.maximum(i_a - 1, 0) + FAR_TILES - 1) // FAR_TILES
        is_a, rows, mx = [], [], []
        for k in range(max_chunks):
            own = k < n_a
            c = jnp.where(own, k, k - n_a)
            r = pl.multiple_of((c * FAR_TILES + pad_tiles) * LANES, LANES)
            s_k = lax.dot_general(ksa_ref[pl.ds(r, FAR_TILES * LANES), :], qf_ref[2 * u + jnp.where(own, 0, 1)],
                                  _NT, preferred_element_type=F32)
            sbuf_ref[u * max_chunks + k] = s_k
            is_a.append(own)
            rows.append(r)
            mx.append(jnp.max(s_k, axis=0, keepdims=True))
        m_a, m_b = mn_a, mn_b
        for k in range(max_chunks):
            m_a = jnp.where(is_a[k], jnp.maximum(m_a, mx[k]), m_a)
            m_b = jnp.where(is_a[k], m_b, jnp.maximum(m_b, mx[k]))
        av_a = jnp.exp2(mn_a - m_a) * avn_a
        av_b = jnp.exp2(mn_b - m_b) * avn_b
        for k in range(max_chunks):
            p = jnp.exp2(sbuf_ref[u * max_chunks + k] - jnp.where(is_a[k], m_a, m_b))
            pv = jnp.dot(vst_ref[:, pl.ds(rows[k], FAR_TILES * LANES)], p.astype(BF16),
                         preferred_element_type=F32)
            av_a = av_a + jnp.where(is_a[k], pv, 0.0)
            av_b = av_b + jnp.where(is_a[k], 0.0, pv)

        def finish(half, gt_ref, w, o_c, o_s, o_w):
            def gate(branch):
                g_id = pl.program_id(1)
                start = pl.multiple_of(branch * N_HEADS + (g_id // 2) * SUBLANES, SUBLANES)
                tile = gt_ref[pl.ds(start, SUBLANES), :]
                rows_g = jnp.where(g_id % 2 == 0, tile[0:Q_PER_KV], tile[Q_PER_KV:])
                return jnp.concatenate([rows_g[r:r + 1, w * Q_BLOCK:(w + 1) * Q_BLOCK] for r in range(Q_PER_KV)],
                                       axis=1)

            o_t = gate(0) * o_c + gate(1) * o_s + gate(2) * o_w
            for pr in range(Q_PER_KV // 2):
                blk = jnp.concatenate([o_t[:, (2 * pr) * LANES:(2 * pr + 1) * LANES],
                                       o_t[:, (2 * pr + 1) * LANES:(2 * pr + 2) * LANES]], axis=0)
                o_ref[half, u, :, pr * LANES:(pr + 1) * LANES] = blk.T.astype(o_ref.dtype)

        finish(0, gta_ref, u, oc_a, normalise(av_a), ow_a)
        finish(1, gtb_ref, v, oc_b, normalise(av_b), ow_b)

    for u in range(PAIRS):
        pair(u)


def _nsa(proj, kc, vct, ksa, vst, kwa, vwt, gt, tt, bc, ovt, s_):
    b_, g_ = kwa.shape[:2]
    n_qb = s_ // Q_BLOCK
    n_p = n_qb // 2
    n_s = n_p // PAIRS
    rows = PAIRS * Q_BLOCK
    m = proj.shape[0]
    nl = Q_PER_KV * LANES
    qcol = SEG_Q * D_MODEL // (Q_PER_KV * HD)

    def far_chunks(i):
        return (max(i - 1, 0) + FAR_TILES - 1) // FAR_TILES

    max_chunks = far_chunks(0) + far_chunks(n_qb - 1)
    assert n_qb % (2 * PAIRS) == 0 and all(
        far_chunks(p) + far_chunks(n_qb - 1 - p) == max_chunks for p in range(n_p))

    def bg(shape):
        return pl.BlockSpec((None, None) + shape, lambda b, g, p: (b, g, 0, 0))

    def blk_a(p):
        return p

    def blk_b(p):
        return 2 * n_s - 1 - p

    def per_block(which):
        return [
            pl.BlockSpec((rows, Q_PER_KV * HD), lambda b, g, p: (b * 2 * n_s + which(p), qcol + g)),
            pl.BlockSpec((None, LANES, rows), lambda b, g, p: (b, 0, which(p))),
            pl.BlockSpec((None, PAIRS, LANES, nl), lambda b, g, p: (g, which(p), 0, 0)),
        ]

    qa, gta, bca = per_block(blk_a)
    qb, gtb, bcb = per_block(blk_b)
    out = pl.pallas_call(
        _nsa_body,
        grid=(b_, g_, n_s),
        in_specs=[
            qa, qb,
            bg((LANES, HD)), bg((HD, LANES)),
            bg((s_ + KV_PAD, 2 * HD)), bg((HD + ONES_ROWS, s_ + KV_PAD)),
            bg((s_ + KV_PAD, 2 * HD)), bg((HD + ONES_ROWS, s_ + KV_PAD)),
            gta, gtb,
            pl.BlockSpec((None, 3, LANES, nl), lambda b, g, p: (g, 0, 0, 0)),
            bca, bcb,
            pl.BlockSpec((s_ // SEL_BLOCK, LANES), lambda b, g, p: (0, 0)),
        ],
        out_specs=pl.BlockSpec((2, PAIRS, Q_BLOCK, Q_PER_KV * HD), lambda b, g, p: (0, b * n_s + p, 0, g)),
        out_shape=jax.ShapeDtypeStruct((2, b_ * n_p, Q_BLOCK, D_MODEL), BF16),
        scratch_shapes=[pltpu.VMEM((PAIRS * max_chunks, FAR_TILES * LANES, nl), F32),
                        pltpu.VMEM((2 * PAIRS, nl, 2 * HD), BF16)],
        compiler_params=_params(("parallel", "parallel", "arbitrary")),
        name="nsa",
    )(proj, proj, kc, vct, ksa, vst, kwa, vwt, gt, gt, tt, bc, bc, ovt)
    return out.reshape(m, D_MODEL)


def _attn_block_row(b, qb, n_b, n_qb):
    n_p = n_qb // 2
    upper = qb >= n_p
    return jnp.where(upper, n_b * n_p, 0) + b * n_p + jnp.where(upper, n_qb - 1 - qb, qb)


def _mix_body(gb_ref, gc_ref, va_ref, gch_ref, vah_ref, uc_ref, vc_ref, g0_ref, g1_ref, g2_ref, x_ref,
              ca_ref, sw_ref, sb_ref, lng_ref, lnb_ref, wa_ref, wb_ref, wc_ref, wo_ref, *rest):
    tm = gb_ref.shape[0]
    ob_refs = rest[:tm // Q_BLOCK]
    o_ref, xa_ref, mx_ref = rest[tm // Q_BLOCK:]
    first = pl.program_id(1) == 0

    halo = gch_ref[...].astype(F32) * vah_ref[...].astype(F32)
    xa_ref[0:HALO, :] = jnp.where(first, 0.0, halo)
    xa_ref[HALO:, :] = gc_ref[...].astype(F32) * va_ref[...].astype(F32)
    ca = ca_ref[...]
    row = lax.broadcasted_iota(jnp.int32, (CHUNK, CHUNK), 0)
    col = lax.broadcasted_iota(jnp.int32, (CHUNK, CHUNK), 1)
    sw = [jnp.where(row >= col, sw_ref[gi], 0.0).astype(BF16) for gi in range(SGU_GROUPS)]

    for r in range(0, tm, ROW_CHUNK):
        rows = slice(r, r + ROW_CHUNK)
        conv = (ca[0:1] * xa_ref[pl.ds(r + HALO - 2, ROW_CHUNK), :]
                + ca[1:2] * xa_ref[pl.ds(r + HALO - 1, ROW_CHUNK), :]
                + ca[2:3] * xa_ref[pl.ds(r + HALO, ROW_CHUNK), :])
        out_a = gb_ref[rows, :].astype(F32) * conv
        y_a = jnp.dot(out_a.astype(BF16), wa_ref[...], preferred_element_type=F32)

        v = vc_ref[rows, :].astype(F32)
        mu = jnp.mean(v, axis=-1, keepdims=True)
        vz = v - mu
        var = jnp.mean(vz * vz, axis=-1, keepdims=True)
        vn = (vz * lax.rsqrt(var + EPS) * lng_ref[...] + lnb_ref[...]).astype(BF16)
        for gi in range(SGU_GROUPS):
            cols = slice(gi * LANES, (gi + 1) * LANES)
            for c in range(0, ROW_CHUNK, CHUNK):
                mx_ref[r + c:r + c + CHUNK, cols] = (
                    jnp.dot(sw[gi], vn[c:c + CHUNK, cols], preferred_element_type=F32) + sb_ref[:, cols])
        out_c = uc_ref[rows, :].astype(F32) * mx_ref[rows, :]
        y_c = jnp.dot(out_c.astype(BF16), wc_ref[...], preferred_element_type=F32)

        out_b = jnp.concatenate([ob_refs[k][...] for k in range(r // Q_BLOCK, (r + ROW_CHUNK) // Q_BLOCK)], axis=0)
        y_b = jnp.dot(out_b, wb_ref[...], preferred_element_type=F32)

        merged = (g0_ref[rows, :].astype(F32) * y_a + g1_ref[rows, :].astype(F32) * y_b
                  + g2_ref[rows, :].astype(F32) * y_c)
        o_ref[rows, :] = x_ref[rows, :] + jnp.dot(merged.astype(BF16), wo_ref[...], preferred_element_type=F32)


def _mix(proj, out_b, x2, seq, conv_a, sgu_w, sgu_b2, ln_g, ln_b, w_a, w_b, w_c, w_o, tm=512):
    m = x2.shape[0]
    n_t = seq // tm
    hb = tm // HALO

    def seg(k):
        return pl.BlockSpec((tm, D_MODEL), lambda b, i: (b * n_t + i, k))

    def halo(k):
        return pl.BlockSpec((HALO, D_MODEL), lambda b, i: (jnp.maximum((b * n_t + i) * hb - 1, 0), k))

    def full(shape):
        return pl.BlockSpec(shape, lambda b, i: (0,) * len(shape))

    def attn(k):
        return pl.BlockSpec((Q_BLOCK, D_MODEL), lambda b, i: (
            _attn_block_row(b, i * (tm // Q_BLOCK) + k, m // seq, seq // Q_BLOCK), 0))

    row = pl.BlockSpec((tm, D_MODEL), lambda b, i: (b * n_t + i, 0))
    wspec = pl.BlockSpec((D_MODEL, D_MODEL), lambda b, i: (0, 0), pipeline_mode=pl.Buffered(1))
    return pl.pallas_call(
        _mix_body,
        grid=(m // seq, n_t),
        in_specs=[
            seg(SEG_GB), seg(SEG_GC), seg(SEG_VA), halo(SEG_GC), halo(SEG_VA), seg(SEG_U), seg(SEG_V),
            seg(SEG_G0), seg(SEG_G0 + 1), seg(SEG_G0 + 2), row,
            full((CONV_W, D_MODEL)), full((SGU_GROUPS, CHUNK, CHUNK)), full((CHUNK, D_MODEL)),
            full((1, D_MODEL)), full((1, D_MODEL)), wspec, wspec, wspec, wspec,
        ] + [attn(k) for k in range(tm // Q_BLOCK)],
        out_specs=row,
        out_shape=jax.ShapeDtypeStruct((m, D_MODEL), F32),
        scratch_shapes=[pltpu.VMEM((tm + HALO, D_MODEL), F32), pltpu.VMEM((tm, D_MODEL), F32)],
        compiler_params=_params(("parallel", "arbitrary")),
        name="mix",
    )(proj, proj, proj, proj, proj, proj, proj, proj, proj, proj, x2,
      conv_a, sgu_w, sgu_b2, ln_g, ln_b, w_a, w_b, w_c, w_o, *([out_b] * (tm // Q_BLOCK)))


def _ffn_body(x_ref, xh_ref, g_ref, wg_ref, wv_ref, cw_ref, wd_ref, fg_ref, o_ref, h_ref, gt_ref, acc_ref, *,
              tiles_per_seq, final):
    i = pl.program_id(0)
    j = pl.program_id(1)
    tm = x_ref.shape[0]

    @pl.when(j == 0)
    def _():
        h_ref[0:HALO, :] = _rms(xh_ref[...], g_ref[...]).astype(BF16)
        h_ref[HALO:, :] = _rms(x_ref[...], g_ref[...]).astype(BF16)
        acc_ref[...] = jnp.zeros_like(acc_ref)

    gate = jnp.dot(h_ref[...], wg_ref[...], preferred_element_type=F32)
    rows = lax.broadcasted_iota(jnp.int32, gate.shape, 0)
    seq_start = (i % tiles_per_seq) == 0
    gt_ref[...] = jnp.where(seq_start & (rows < HALO), 0.0, gate)
    cw = cw_ref[...]
    conv = (cw[0:1] * gt_ref[pl.ds(HALO - 2, tm), :] + cw[1:2] * gt_ref[pl.ds(HALO - 1, tm), :]
            + cw[2:3] * gt_ref[pl.ds(HALO, tm), :])
    val = jnp.dot(h_ref[HALO:, :], wv_ref[...], preferred_element_type=F32)
    act = (jax.nn.gelu(conv) * val).astype(BF16)
    acc_ref[...] += jnp.dot(act, wd_ref[...], preferred_element_type=F32)

    @pl.when(j == pl.num_programs(1) - 1)
    def _():
        y = x_ref[...] + acc_ref[...]
        o_ref[...] = _rms(y, fg_ref[...]) if final else y


def _ffn(x2, seq, g, w_up, conv_w, w_down, final_g, final, tm=1024, tf=512):
    m = x2.shape[0]
    n_f = D_FF // tf
    w_up = jnp.stack([w_up[:, k * tf:(k + 1) * tf] for k in range(2 * n_f)]).astype(BF16)
    hb = tm // HALO
    return pl.pallas_call(
        functools.partial(_ffn_body, tiles_per_seq=seq // tm, final=final),
        grid=(m // tm, n_f),
        in_specs=[
            pl.BlockSpec((tm, D_MODEL), lambda i, j: (i, 0)),
            pl.BlockSpec((HALO, D_MODEL), lambda i, j: (jnp.maximum(i * hb - 1, 0), 0)),
            pl.BlockSpec((1, D_MODEL), lambda i, j: (0, 0)),
            pl.BlockSpec((None, D_MODEL, tf), lambda i, j: (j, 0, 0)),
            pl.BlockSpec((None, D_MODEL, tf), lambda i, j: (n_f + j, 0, 0)),
            pl.BlockSpec((CONV_W, tf), lambda i, j: (0, j)),
            pl.BlockSpec((tf, D_MODEL), lambda i, j: (j, 0)),
            pl.BlockSpec((1, D_MODEL), lambda i, j: (0, 0)),
        ],
        out_specs=pl.BlockSpec((tm, D_MODEL), lambda i, j: (i, 0)),
        out_shape=jax.ShapeDtypeStruct((m, D_MODEL), F32),
        scratch_shapes=[
            pltpu.VMEM((tm + HALO, D_MODEL), BF16),
            pltpu.VMEM((tm + HALO, tf), F32),
            pltpu.VMEM((tm, D_MODEL), F32),
        ],
        compiler_params=_params(("parallel", "arbitrary")),
        name="ffn",
    )(x2, x2, g, w_up, w_up, conv_w, w_down, final_g)


def _split_w_in(w_in):
    c = [0]
    for sz in [D_MODEL] * 4 + [N_KV * HD] * 6 + [3 * N_HEADS] + [D_MODEL] * 5:
        c.append(c[-1] + sz)
    starts = c[0:4] + c[11:16]
    main = jnp.stack([w_in[:, a:a + D_MODEL] for a in starts]).astype(BF16)
    kv = w_in[:, c[4]:c[10]].reshape(D_MODEL, 6, N_KV, HD)[:, jnp.array([2, 3, 4, 5, 0, 1])]
    kv = kv.transpose(0, 2, 1, 3).reshape(D_MODEL, N_KVCOL)
    gate = jnp.pad(w_in[:, c[10]:c[11]], ((0, 0), (0, LANES - 3 * N_HEADS)))
    return main, kv.astype(BF16), gate.astype(BF16)


def _mixer_layer(x2, b_, s_, tt, bc, ovt, norm_g, w_in, conv_a, cmp_pos, cmp_w1, cmp_w2, sgu_w, sgu_b,
                 sgu_norm_g, sgu_norm_b, w_br_a, w_br_b, w_br_c, w_o):
    w_main, w_kv, w_gate = _split_w_in(w_in)
    norm_g = norm_g.reshape(1, D_MODEL)
    proj = _in_proj(x2, norm_g, w_main)
    kc, vc, ksa, vst, kwa, vwt, gt = _kv_proj(x2, b_, s_, norm_g, w_kv, w_gate)

    k_cmp, v_cmp = _compress(kc, vc, cmp_pos.reshape(2, 1, CMP_LEN * HD),
                             cmp_w1.astype(BF16), cmp_w2.astype(BF16))
    out_b = _nsa(proj, k_cmp, v_cmp.transpose(0, 1, 3, 2), ksa, vst, kwa, vwt, gt, tt, bc, ovt, s_)

    sgu_b2 = jnp.broadcast_to(sgu_b.T[:, :, None], (CHUNK, SGU_GROUPS, D_MODEL // SGU_GROUPS)).reshape(CHUNK, D_MODEL)
    return _mix(proj, out_b, x2, s_, conv_a, sgu_w, sgu_b2,
                sgu_norm_g.reshape(1, D_MODEL), sgu_norm_b.reshape(1, D_MODEL),
                w_br_a.astype(BF16), w_br_b.astype(BF16), w_br_c.astype(BF16), w_o.astype(BF16))


def _overlap_t(s_):
    n_blk = s_ // SEL_BLOCK
    cmp_start = jnp.arange(LANES) * CMP_STRIDE
    cmp_end = cmp_start + CMP_LEN - 1
    blk_start = jnp.arange(n_blk) * SEL_BLOCK
    n_cmp = (s_ - CMP_LEN) // CMP_STRIDE + 1
    ov = ((cmp_start[None, :] < blk_start[:, None] + SEL_BLOCK) & (cmp_end[None, :] >= blk_start[:, None])
          & (jnp.arange(LANES)[None, :] < n_cmp))
    return ov.astype(F32)


def kernel(x, rel_bias, norm_mix, w_in, conv_a, cmp_pos, cmp_w1, cmp_w2, sgu_w, sgu_b, sgu_norm_g, sgu_norm_b,
           w_br_a, w_br_b, w_br_c, w_o, norm_ffn, ffn_w_up, ffn_conv, ffn_w_down, norm_final):
    b_, s_, _ = x.shape
    depth = w_in.shape[0]
    x2 = x.reshape(b_ * s_, D_MODEL)
    tt, bc = _bias_tiles(rel_bias, s_ // Q_BLOCK)
    ovt = _overlap_t(s_)
    for l in range(depth):
        x2 = _mixer_layer(x2, b_, s_, tt, bc, ovt, norm_mix[l], w_in[l], conv_a[l], cmp_pos[l], cmp_w1[l],
                          cmp_w2[l], sgu_w[l], sgu_b[l], sgu_norm_g[l], sgu_norm_b[l],
                          w_br_a[l], w_br_b[l], w_br_c[l], w_o[l])
        x2 = _ffn(x2, s_, norm_ffn[l].reshape(1, D_MODEL), ffn_w_up[l], ffn_conv[l], ffn_w_down[l].astype(BF16),
                  norm_final.reshape(1, D_MODEL), final=(l == depth - 1))
    return x2.reshape(b_, s_, D_MODEL)
```

```python
import functools
import math

import jax
import jax.numpy as jnp
from jax import lax
from jax.experimental import pallas as pl
from jax.experimental.pallas import tpu as pltpu

D_MODEL = 1024
HD = 64
N_HEADS = D_MODEL // HD
N_KV = 4
Q_PER_KV = N_HEADS // N_KV
SGU_GROUPS = 8
CHUNK = 128
CONV_W = 3
CMP_LEN = 32
CMP_STRIDE = 16
CMP_HID = 2 * HD
SEL_BLOCK = 64
SEL_TOP_N = 8
WINDOW = 512
Q_BLOCK = 128
D_FF = 3 * D_MODEL
N_BUCKETS = 32
MAX_EXACT = N_BUCKETS // 2
REL_MAX_DIST = 128
SCALE = HD ** -0.5
EPS = 1e-6
NEG_INF = -1e30
FORCE = 1e9
LOG2E = 1.4426950408889634

LANES = 128
SUBLANES = 8
ROW_CHUNK = 256
HALO = 16
N_SEG = 9
SEG_GB, SEG_GC, SEG_VA, SEG_Q, SEG_U, SEG_V, SEG_G0 = 0, 1, 2, 3, 4, 5, 6
N_MAIN = N_SEG * D_MODEL
N_KVCOL = 6 * N_KV * HD
KV_PAD = WINDOW
VMEM_LIMIT = 56 * 1024 * 1024

BF16 = jnp.bfloat16
F32 = jnp.float32
_NT = (((1,), (1,)), ((), ()))


def _params(sem):
    return pltpu.CompilerParams(dimension_semantics=sem, vmem_limit_bytes=VMEM_LIMIT)


def _rms(x, g):
    return x * lax.rsqrt(jnp.mean(x * x, axis=-1, keepdims=True) + EPS) * g


def _in_proj_body(x_ref, g_ref, w_ref, o_ref, h_ref, *, gelu_tiles, sigmoid_tiles):
    j = pl.program_id(1)

    @pl.when(j == 0)
    def _():
        h_ref[...] = _rms(x_ref[...], g_ref[...]).astype(BF16)

    def tile(act):
        for r in range(0, h_ref.shape[0], ROW_CHUNK):
            y = jnp.dot(h_ref[r:r + ROW_CHUNK, :], w_ref[...], preferred_element_type=F32)
            o_ref[r:r + ROW_CHUNK, :] = act(y).astype(o_ref.dtype)

    is_gelu = (j >= gelu_tiles[0]) & (j < gelu_tiles[1])
    is_sigmoid = (j >= sigmoid_tiles[0]) & (j < sigmoid_tiles[1])
    pl.when(is_gelu)(lambda: tile(jax.nn.gelu))
    pl.when(is_sigmoid)(lambda: tile(jax.nn.sigmoid))
    pl.when(jnp.logical_not(is_gelu | is_sigmoid))(lambda: tile(lambda y: y))


def _in_proj(x2, g, w_main, tm=2048):
    m = x2.shape[0]
    tn = D_MODEL
    body = functools.partial(_in_proj_body, gelu_tiles=(SEG_U, SEG_V + 1), sigmoid_tiles=(SEG_G0, SEG_G0 + 3))
    return pl.pallas_call(
        body,
        grid=(m // tm, N_SEG),
        in_specs=[
            pl.BlockSpec((tm, D_MODEL), lambda i, j: (i, 0)),
            pl.BlockSpec((1, D_MODEL), lambda i, j: (0, 0)),
            pl.BlockSpec((None, D_MODEL, tn), lambda i, j: (j, 0, 0)),
        ],
        out_specs=pl.BlockSpec((tm, tn), lambda i, j: (i, j)),
        out_shape=jax.ShapeDtypeStruct((m, N_MAIN), BF16),
        scratch_shapes=[pltpu.VMEM((tm, D_MODEL), BF16)],
        compiler_params=_params(("parallel", "arbitrary")),
        name="in_proj",
    )(x2, g, w_main)


def _kv_proj_body(x_ref, g_ref, w_ref, wg_ref, kc_ref, vc_ref, ksa_ref, vst_ref, kwa_ref, vwt_ref, gt_ref, cmp_ref):
    s_id = pl.program_id(1)
    tm = x_ref.shape[0]
    lane = lax.broadcasted_iota(jnp.int32, (tm, LANES), 1)
    ones_rows = (lax.broadcasted_iota(jnp.int32, (ONES_ROWS, tm), 0) == 0).astype(BF16)

    @pl.when(s_id == 0)
    def _():
        pad_keys = jnp.where(lane == HD + PAD_COL, 1.0, 0.0).astype(BF16)
        for g in range(N_KV):
            ksa_ref[g] = pad_keys
            kwa_ref[g] = pad_keys
            vst_ref[g] = jnp.zeros(vst_ref.shape[1:], BF16)
            vwt_ref[g] = jnp.zeros(vwt_ref.shape[1:], BF16)

    @pl.when(s_id > 0)
    def _():
        h = _rms(x_ref[...], g_ref[...]).astype(BF16)
        y = jnp.dot(h, w_ref[...], preferred_element_type=F32)
        tok = (s_id - 1) * tm + lax.broadcasted_iota(jnp.int32, (tm, LANES), 0)
        blk_id = jnp.where(lane - HD == tok // SEL_BLOCK, 1.0, 0.0)
        for g in range(N_KV):
            base = g * 6 * HD
            sel = y[:, base:base + 2 * HD]
            win = y[:, base + 2 * HD:base + 4 * HD]
            cmp = y[:, base + 4 * HD:base + 6 * HD]
            ksa_ref[g] = jnp.where(lane < HD, sel, blk_id).astype(BF16)
            kwa_ref[g] = jnp.where(lane < HD, win, 0.0).astype(BF16)
            vst_ref[g, 0:HD, :] = sel.T[HD:].astype(BF16)
            vst_ref[g, HD:, :] = ones_rows
            vwt_ref[g, 0:HD, :] = win.T[HD:].astype(BF16)
            vwt_ref[g, HD:, :] = ones_rows
            cmp_ref[...] = cmp
            for c in range(CMP_STRIDE):
                every = cmp_ref[pl.ds(c, tm // CMP_STRIDE, stride=CMP_STRIDE), :].astype(BF16)
                kc_ref[g, :, c * HD:(c + 1) * HD] = every[:, 0:HD]
                vc_ref[g, :, c * HD:(c + 1) * HD] = every[:, HD:]
        gt_ref[...] = jax.nn.sigmoid(jnp.dot(h, wg_ref[...], preferred_element_type=F32)).T


def _kv_proj(x2, b_, s_, g, w_kv, w_gate, tm=512):
    n_t = s_ // tm
    assert KV_PAD == tm

    def tok(b, s):
        return jnp.maximum(s - 1, 0)

    return pl.pallas_call(
        _kv_proj_body,
        grid=(b_, n_t + 1),
        in_specs=[
            pl.BlockSpec((tm, D_MODEL), lambda b, s: (b * n_t + tok(b, s), 0)),
            pl.BlockSpec((1, D_MODEL), lambda b, s: (0, 0)),
            pl.BlockSpec((D_MODEL, N_KVCOL), lambda b, s: (0, 0), pipeline_mode=pl.Buffered(1)),
            pl.BlockSpec((D_MODEL, LANES), lambda b, s: (0, 0), pipeline_mode=pl.Buffered(1)),
        ],
        out_specs=[
            pl.BlockSpec((None, N_KV, tm // CMP_STRIDE, CMP_STRIDE * HD), lambda b, s: (b, 0, tok(b, s), 0)),
            pl.BlockSpec((None, N_KV, tm // CMP_STRIDE, CMP_STRIDE * HD), lambda b, s: (b, 0, tok(b, s), 0)),
            pl.BlockSpec((None, N_KV, tm, 2 * HD), lambda b, s: (b, 0, s, 0)),
            pl.BlockSpec((None, N_KV, HD + ONES_ROWS, tm), lambda b, s: (b, 0, 0, s)),
            pl.BlockSpec((None, N_KV, tm, 2 * HD), lambda b, s: (b, 0, s, 0)),
            pl.BlockSpec((None, N_KV, HD + ONES_ROWS, tm), lambda b, s: (b, 0, 0, s)),
            pl.BlockSpec((None, LANES, tm), lambda b, s: (b, 0, tok(b, s))),
        ],
        out_shape=[
            jax.ShapeDtypeStruct((b_, N_KV, s_ // CMP_STRIDE, CMP_STRIDE * HD), BF16),
            jax.ShapeDtypeStruct((b_, N_KV, s_ // CMP_STRIDE, CMP_STRIDE * HD), BF16),
            jax.ShapeDtypeStruct((b_, N_KV, KV_PAD + s_, 2 * HD), BF16),
            jax.ShapeDtypeStruct((b_, N_KV, HD + ONES_ROWS, KV_PAD + s_), BF16),
            jax.ShapeDtypeStruct((b_, N_KV, KV_PAD + s_, 2 * HD), BF16),
            jax.ShapeDtypeStruct((b_, N_KV, HD + ONES_ROWS, KV_PAD + s_), BF16),
            jax.ShapeDtypeStruct((b_, LANES, s_), F32),
        ],
        scratch_shapes=[pltpu.VMEM((tm, 2 * HD), F32)],
        compiler_params=_params(("parallel", "arbitrary")),
        name="kv_proj",
    )(x2, g, w_kv, w_gate)


def _rel_bucket(dist):
    dist = jnp.maximum(dist, 0)
    log_ratio = jnp.log(jnp.maximum(dist, 1).astype(F32) / MAX_EXACT) / math.log(REL_MAX_DIST / MAX_EXACT)
    large = MAX_EXACT + (log_ratio * (N_BUCKETS - MAX_EXACT)).astype(jnp.int32)
    return jnp.where(dist < MAX_EXACT, dist, jnp.minimum(large, N_BUCKETS - 1))


def _bias_body(tab_ref, bt_ref, bc_ref, tt_ref, bco_ref, *, n_cmp):
    h = pl.program_id(0)

    def lookup(bk):
        acc = jnp.zeros(bk.shape, F32)
        for b in range(N_BUCKETS):
            acc = jnp.where(bk == b, tab_ref[b, h], acc)
        return acc

    kk = lax.broadcasted_iota(jnp.int32, (LANES, LANES), 0)
    qq = lax.broadcasted_iota(jnp.int32, (LANES, LANES), 1)
    far = lookup(bt_ref[2])
    tt_ref[0] = (lookup(bt_ref[1]) - far) * LOG2E
    tt_ref[1] = jnp.where(qq >= kk, (lookup(bt_ref[0]) - far) * LOG2E, NEG_INF)
    tt_ref[2] = jnp.where(kk > qq, 0.0, NEG_INF)
    for i in range(bc_ref.shape[0]):
        ok = (i * Q_BLOCK + qq - (kk * CMP_STRIDE + CMP_LEN - 1) >= 0) & (kk < n_cmp)
        bco_ref[i] = jnp.where(ok, lookup(bc_ref[i]), NEG_INF)


def _bias_tiles(rel_bias, n_qb):
    kk = jnp.arange(LANES)[:, None]
    qq = jnp.arange(LANES)[None, :]
    bt = jnp.stack([_rel_bucket(LANES * d + qq - kk) for d in range(3)]).astype(jnp.int32)
    cmp_end = CMP_STRIDE * jnp.arange(LANES) + CMP_LEN - 1
    t = (Q_BLOCK * jnp.arange(n_qb))[:, None, None] + qq[None]
    bc = _rel_bucket(t - cmp_end[None, :, None]).astype(jnp.int32)
    n_cmp = (n_qb * Q_BLOCK - CMP_LEN) // CMP_STRIDE + 1
    return pl.pallas_call(
        functools.partial(_bias_body, n_cmp=n_cmp),
        grid=(N_HEADS,),
        in_specs=[
            pl.BlockSpec(memory_space=pltpu.SMEM),
            pl.BlockSpec((3, LANES, LANES), lambda h: (0, 0, 0)),
            pl.BlockSpec((n_qb, LANES, LANES), lambda h: (0, 0, 0)),
        ],
        out_specs=[
            pl.BlockSpec((None, 3, LANES, LANES), lambda h: (h // Q_PER_KV, 0, 0, h % Q_PER_KV)),
            pl.BlockSpec((None, n_qb, LANES, LANES), lambda h: (h // Q_PER_KV, 0, 0, h % Q_PER_KV)),
        ],
        out_shape=[
            jax.ShapeDtypeStruct((N_KV, 3, LANES, Q_PER_KV * LANES), F32),
            jax.ShapeDtypeStruct((N_KV, n_qb, LANES, Q_PER_KV * LANES), F32),
        ],
        compiler_params=_params(("arbitrary",)),
        name="bias_tiles",
    )(rel_bias.astype(F32), bt, bc)


def _compress_body(k3_ref, v3_ref, pos_ref, w1_ref, w2_ref, ko_ref, vo_ref):
    half = CMP_STRIDE * HD
    for which, (src, dst) in enumerate(((k3_ref, ko_ref), (v3_ref, vo_ref))):
        x = src[...].astype(F32).reshape(-1, src.shape[-1])
        pos = pos_ref[which]
        xa = (x + pos[:, :half]).astype(BF16)
        xb = (x + pos[:, half:]).astype(BF16)
        a = jnp.dot(xa, w1_ref[which, :half, :], preferred_element_type=F32)
        b = jnp.dot(xb, w1_ref[which, half:, :], preferred_element_type=F32)
        hid = jax.nn.gelu(a + pltpu.roll(b, b.shape[0] - 1, 0))
        out = jnp.dot(hid.astype(BF16), w2_ref[which], preferred_element_type=F32)
        dst[...] = out.reshape(dst.shape).astype(dst.dtype)


def _compress(k3, v3, pos, w1, w2):
    b_, g_, nm, width = k3.shape
    spec3 = pl.BlockSpec((None, g_, nm, width), lambda b: (b, 0, 0, 0))
    ospec = pl.BlockSpec((None, g_, nm, HD), lambda b: (b, 0, 0, 0))
    return pl.pallas_call(
        _compress_body,
        grid=(b_,),
        in_specs=[
            spec3, spec3,
            pl.BlockSpec((2, 1, CMP_LEN * HD), lambda b: (0, 0, 0)),
            pl.BlockSpec((2, CMP_LEN * HD, CMP_HID), lambda b: (0, 0, 0)),
            pl.BlockSpec((2, CMP_HID, HD), lambda b: (0, 0, 0)),
        ],
        out_specs=[ospec, ospec],
        out_shape=[jax.ShapeDtypeStruct((b_, g_, nm, HD), BF16)] * 2,
        compiler_params=_params(("parallel",)),
        name="compress",
    )(k3, v3, pos, w1, w2)


FAR_TILES = 2
PAD_COL = 32
PAIRS = 8
ONES_ROWS = 16


def _nsa_body(qa_ref, qb_ref, kc_ref, vct_ref, ksa_ref, vst_ref, kwa_ref, vwt_ref, gta_ref, gtb_ref,
              tt_ref, bca_ref, bcb_ref, ovt_ref, o_ref, sbuf_ref, qf_ref):
    p_id = pl.program_id(2)
    n_qb = 2 * PAIRS * pl.num_programs(2)
    nl = Q_PER_KV * LANES
    n_win = WINDOW // LANES + 1
    n_blk = ovt_ref.shape[0]
    max_chunks = sbuf_ref.shape[0] // PAIRS
    pad_tiles = KV_PAD // LANES
    j_io = lax.broadcasted_iota(jnp.int32, (n_blk, LANES), 0)
    q_io = lax.broadcasted_iota(jnp.int32, (n_blk, LANES), 1)
    col_w = lax.broadcasted_iota(jnp.int32, (nl, HD), 1)
    pad_cols = jnp.where(col_w == PAD_COL, NEG_INF, 0.0).astype(BF16)

    def compressed(qt, bc):
        q = jnp.concatenate([qt[:, r * HD:(r + 1) * HD] for r in range(Q_PER_KV)], axis=0) * SCALE
        s = lax.dot_general(kc_ref[...], q, _NT, preferred_element_type=F32) + bc
        m_c = jnp.max(s, axis=0, keepdims=True)
        p = jnp.exp(s - m_c)
        l_c = jnp.sum(p, axis=0, keepdims=True)
        p_c = p * (jnp.where(m_c > 0.5 * NEG_INF, 1.0, 0.0) / l_c)
        o_c = jnp.dot(vct_ref[...], p_c.astype(BF16), preferred_element_type=F32)
        return o_c, p_c, (q.astype(F32) * LOG2E).astype(BF16)

    def select(i, p_c, q2, which):
        p_sum = p_c[:, 0:LANES]
        for r in range(1, Q_PER_KV):
            p_sum = p_sum + p_c[:, r * LANES:(r + 1) * LANES]
        imp = jnp.dot(ovt_ref[...], p_sum, precision=lax.Precision.HIGHEST, preferred_element_type=F32)
        t_b = i * Q_BLOCK + q_io
        cur = lax.shift_right_logical(t_b, SEL_BLOCK.bit_length() - 1)
        forced = (j_io == 0) | (j_io == cur) | (j_io == cur - 1)
        valid = j_io * SEL_BLOCK <= t_b
        score = jnp.where(forced, FORCE, jnp.where(valid, imp, -FORCE))
        rank = jnp.zeros((n_blk, LANES), jnp.int32)
        for ii in range(n_blk):
            row = score[ii:ii + 1, :]
            beats = (row > score) | ((row == score) & (j_io > ii))
            rank = rank + beats.astype(jnp.int32)
        keep = (rank < min(SEL_TOP_N, n_blk)) & (score >= 0.0)

        def augment(kept):
            cols = jnp.concatenate([jnp.where(kept, 0.0, NEG_INF), jnp.full((SUBLANES, LANES), NEG_INF, F32),
                                    jnp.zeros((LANES - n_blk - SUBLANES, LANES), F32)], axis=0)
            cols_t = cols.T[:, 0:HD].astype(BF16)
            return jnp.concatenate([q2, jnp.concatenate([cols_t] * Q_PER_KV, axis=0)], axis=1)

        qf_ref[which] = augment(keep & (j_io < 2 * (i - 1)))
        return augment(keep)

    def softmax_av(tiles, vt):
        s_all = jnp.concatenate(tiles, axis=0)
        m = jnp.max(s_all, axis=0, keepdims=True)
        p = jnp.exp2(s_all - m)
        return m, jnp.dot(vt, p.astype(BF16), preferred_element_type=F32)

    def normalise(av):
        return av[0:HD] / av[HD:HD + 1]

    def logit_tiles(k_ref, qq, row0, n):
        s_all = lax.dot_general(k_ref[pl.ds(row0, n * LANES), :], qq, _NT, preferred_element_type=F32)
        return [s_all[t * LANES:(t + 1) * LANES] for t in range(n)]

    def window_logits(i, q2):
        r0 = pl.multiple_of((i + pad_tiles + 1 - n_win) * LANES, LANES)
        tw = logit_tiles(kwa_ref, jnp.concatenate([q2, pad_cols], axis=1), r0, n_win)
        tw[0] = tw[0] + tt_ref[2]
        tw[n_win - 2] = tw[n_win - 2] + tt_ref[0]
        tw[n_win - 1] = tw[n_win - 1] + tt_ref[1]
        return tw

    def window_softmax(i, tw):
        r0 = pl.multiple_of((i + pad_tiles + 1 - n_win) * LANES, LANES)
        return normalise(softmax_av(tw, vwt_ref[:, pl.ds(r0, n_win * LANES)])[1])

    def near_logits(i, q_near):
        r1 = pl.multiple_of((i + pad_tiles - 1) * LANES, LANES)
        tn = logit_tiles(ksa_ref, q_near, r1, 2)
        tn[0] = tn[0] + tt_ref[0]
        tn[1] = tn[1] + tt_ref[1]
        return tn

    def near_softmax(i, tn):
        r1 = pl.multiple_of((i + pad_tiles - 1) * LANES, LANES)
        return softmax_av(tn, vst_ref[:, pl.ds(r1, 2 * LANES)])

    def pair(u):
        v = PAIRS - 1 - u
        i_a = PAIRS * p_id + u
        i_b = n_qb - 1 - i_a
        rows_a = slice(u * Q_BLOCK, (u + 1) * Q_BLOCK)
        rows_b = slice(v * Q_BLOCK, (v + 1) * Q_BLOCK)
        oc_a, pc_a, q2_a = compressed(qa_ref[rows_a, :], bca_ref[u])
        oc_b, pc_b, q2_b = compressed(qb_ref[rows_b, :], bcb_ref[v])
        tw_a = window_logits(i_a, q2_a)
        tw_b = window_logits(i_b, q2_b)
        qn_a = select(i_a, pc_a, q2_a, 2 * u)
        ow_a = window_softmax(i_a, tw_a)
        qn_b = select(i_b, pc_b, q2_b, 2 * u + 1)
        ow_b = window_softmax(i_b, tw_b)
        tn_a = near_logits(i_a, qn_a)
        tn_b = near_logits(i_b, qn_b)
        mn_a, avn_a = near_softmax(i_a, tn_a)
        mn_b, avn_b = near_softmax(i_b, tn_b)

        n_a = (jnp.maximum(i_a - 1, 0) + FAR_TILES - 1) // FAR_TILES
        is_a, rows, mx = [], [], []
        for k in range(max_chunks):
            own = k < n_a
            c = jnp.where(own, k, k - n_a)
            r = pl.multiple_of((c * FAR_TILES + pad_tiles) * LANES, LANES)
            s_k = lax.dot_general(ksa_ref[pl.ds(r, FAR_TILES * LANES), :], qf_ref[2 * u + jnp.where(own, 0, 1)],
                                  _NT, preferred_element_type=F32)
            sbuf_ref[u * max_chunks + k] = s_k
            is_a.append(own)
            rows.append(r)
            mx.append(jnp.max(s_k, axis=0, keepdims=True))
        m_a, m_b = mn_a, mn_b
        for k in range(max_chunks):
            m_a = jnp.where(is_a[k], jnp.maximum(m_a, mx[k]), m_a)
            m_b = jnp.where(is_a[k], m_b, jnp.maximum(m_b, mx[k]))
        av_a = jnp.exp2(mn_a - m_a) * avn_a
        av_b = jnp.exp2(mn_b - m_b) * avn_b
        for k in range(max_chunks):
            p = jnp.exp2(sbuf_ref[u * max_chunks + k] - jnp.where(is_a[k], m_a, m_b))
            pv = jnp.dot(vst_ref[:, pl.ds(rows[k], FAR_TILES * LANES)], p.astype(BF16),
                         preferred_element_type=F32)
            av_a = av_a + jnp.where(is_a[k], pv, 0.0)
            av_b = av_b + jnp.where(is_a[k], 0.0, pv)

        def finish(half, gt_ref, w, o_c, o_s, o_w):
            def gate(branch):
                g_id = pl.program_id(1)
                start = pl.multiple_of(branch * N_HEADS + (g_id // 2) * SUBLANES, SUBLANES)
                tile = gt_ref[pl.ds(start, SUBLANES), :]
                rows_g = jnp.where(g_id % 2 == 0, tile[0:Q_PER_KV], tile[Q_PER_KV:])
                return jnp.concatenate([rows_g[r:r + 1, w * Q_BLOCK:(w + 1) * Q_BLOCK] for r in range(Q_PER_KV)],
                                       axis=1)

            o_t = gate(0) * o_c + gate(1) * o_s + gate(2) * o_w
            for pr in range(Q_PER_KV // 2):
                blk = jnp.concatenate([o_t[:, (2 * pr) * LANES:(2 * pr + 1) * LANES],
                                       o_t[:, (2 * pr + 1) * LANES:(2 * pr + 2) * LANES]], axis=0)
                o_ref[half, u, :, pr * LANES:(pr + 1) * LANES] = blk.T.astype(o_ref.dtype)

        finish(0, gta_ref, u, oc_a, normalise(av_a), ow_a)
        finish(1, gtb_ref, v, oc_b, normalise(av_b), ow_b)

    for u in range(PAIRS):
        pair(u)


def _nsa(proj, kc, vct, ksa, vst, kwa, vwt, gt, tt, bc, ovt, s_):
    b_, g_ = kwa.shape[:2]
    n_qb = s_ // Q_BLOCK
    n_p = n_qb // 2
    n_s = n_p // PAIRS
    rows = PAIRS * Q_BLOCK
    m = proj.shape[0]
    nl = Q_PER_KV * LANES
    qcol = SEG_Q * D_MODEL // (Q_PER_KV * HD)

    def far_chunks(i):
        return (max(i - 1, 0) + FAR_TILES - 1) // FAR_TILES

    max_chunks = far_chunks(0) + far_chunks(n_qb - 1)
    assert n_qb % (2 * PAIRS) == 0 and all(
        far_chunks(p) + far_chunks(n_qb - 1 - p) == max_chunks for p in range(n_p))

    def bg(shape):
        return pl.BlockSpec((None, None) + shape, lambda b, g, p: (b, g, 0, 0))

    def blk_a(p):
        return p

    def blk_b(p):
        return 2 * n_s - 1 - p

    def per_block(which):
        return [
            pl.BlockSpec((rows, Q_PER_KV * HD), lambda b, g, p: (b * 2 * n_s + which(p), qcol + g)),
            pl.BlockSpec((None, LANES, rows), lambda b, g, p: (b, 0, which(p))),
            pl.BlockSpec((None, PAIRS, LANES, nl), lambda b, g, p: (g, which(p), 0, 0)),
        ]

    qa, gta, bca = per_block(blk_a)
    qb, gtb, bcb = per_block(blk_b)
    out = pl.pallas_call(
        _nsa_body,
        grid=(b_, g_, n_s),
        in_specs=[
            qa, qb,
            bg((LANES, HD)), bg((HD, LANES)),
            bg((s_ + KV_PAD, 2 * HD)), bg((HD + ONES_ROWS, s_ + KV_PAD)),
            bg((s_ + KV_PAD, 2 * HD)), bg((HD + ONES_ROWS, s_ + KV_PAD)),
            gta, gtb,
            pl.BlockSpec((None, 3, LANES, nl), lambda b, g, p: (g, 0, 0, 0)),
            bca, bcb,
            pl.BlockSpec((s_ // SEL_BLOCK, LANES), lambda b, g, p: (0, 0)),
        ],
        out_specs=pl.BlockSpec((2, PAIRS, Q_BLOCK, Q_PER_KV * HD), lambda b, g, p: (0, b * n_s + p, 0, g)),
        out_shape=jax.ShapeDtypeStruct((2, b_ * n_p, Q_BLOCK, D_MODEL), BF16),
        scratch_shapes=[pltpu.VMEM((PAIRS * max_chunks, FAR_TILES * LANES, nl), F32),
                        pltpu.VMEM((2 * PAIRS, nl, 2 * HD), BF16)],
        compiler_params=_params(("parallel", "parallel", "arbitrary")),
        name="nsa",
    )(proj, proj, kc, vct, ksa, vst, kwa, vwt, gt, gt, tt, bc, bc, ovt)
    return out.reshape(m, D_MODEL)


def _attn_block_row(b, qb, n_b, n_qb):
    n_p = n_qb // 2
    upper = qb >= n_p
    return jnp.where(upper, n_b * n_p, 0) + b * n_p + jnp.where(upper, n_qb - 1 - qb, qb)


def _mix_body(gb_ref, gc_ref, va_ref, gch_ref, vah_ref, uc_ref, vc_ref, g0_ref, g1_ref, g2_ref, x_ref,
              ca_ref, sw_ref, sb_ref, lng_ref, lnb_ref, wa_ref, wb_ref, wc_ref, wo_ref, *rest):
    tm = gb_ref.shape[0]
    ob_refs = rest[:tm // Q_BLOCK]
    o_ref, xa_ref, mx_ref = rest[tm // Q_BLOCK:]
    first = pl.program_id(1) == 0

    halo = gch_ref[...].astype(F32) * vah_ref[...].astype(F32)
    xa_ref[0:HALO, :] = jnp.where(first, 0.0, halo)
    xa_ref[HALO:, :] = gc_ref[...].astype(F32) * va_ref[...].astype(F32)
    ca = ca_ref[...]
    row = lax.broadcasted_iota(jnp.int32, (CHUNK, CHUNK), 0)
    col = lax.broadcasted_iota(jnp.int32, (CHUNK, CHUNK), 1)
    sw = [jnp.where(row >= col, sw_ref[gi], 0.0).astype(BF16) for gi in range(SGU_GROUPS)]

    for r in range(0, tm, ROW_CHUNK):
        rows = slice(r, r + ROW_CHUNK)
        conv = (ca[0:1] * xa_ref[pl.ds(r + HALO - 2, ROW_CHUNK), :]
                + ca[1:2] * xa_ref[pl.ds(r + HALO - 1, ROW_CHUNK), :]
                + ca[2:3] * xa_ref[pl.ds(r + HALO, ROW_CHUNK), :])
        out_a = gb_ref[rows, :].astype(F32) * conv
        y_a = jnp.dot(out_a.astype(BF16), wa_ref[...], preferred_element_type=F32)

        v = vc_ref[rows, :].astype(F32)
        mu = jnp.mean(v, axis=-1, keepdims=True)
        vz = v - mu
        var = jnp.mean(vz * vz, axis=-1, keepdims=True)
        vn = (vz * lax.rsqrt(var + EPS) * lng_ref[...] + lnb_ref[...]).astype(BF16)
        for gi in range(SGU_GROUPS):
            cols = slice(gi * LANES, (gi + 1) * LANES)
            for c in range(0, ROW_CHUNK, CHUNK):
                mx_ref[r + c:r + c + CHUNK, cols] = (
                    jnp.dot(sw[gi], vn[c:c + CHUNK, cols], preferred_element_type=F32) + sb_ref[:, cols])
        out_c = uc_ref[rows, :].astype(F32) * mx_ref[rows, :]
        y_c = jnp.dot(out_c.astype(BF16), wc_ref[...], preferred_element_type=F32)

        out_b = jnp.concatenate([ob_refs[k][...] for k in range(r // Q_BLOCK, (r + ROW_CHUNK) // Q_BLOCK)], axis=0)
        y_b = jnp.dot(out_b, wb_ref[...], preferred_element_type=F32)

        merged = (g0_ref[rows, :].astype(F32) * y_a + g1_ref[rows, :].astype(F32) * y_b
                  + g2_ref[rows, :].astype(F32) * y_c)
        o_ref[rows, :] = x_ref[rows, :] + jnp.dot(merged.astype(BF16), wo_ref[...], preferred_element_type=F32)


def _mix(proj, out_b, x2, seq, conv_a, sgu_w, sgu_b2, ln_g, ln_b, w_a, w_b, w_c, w_o, tm=512):
    m = x2.shape[0]
    n_t = seq // tm
    hb = tm // HALO

    def seg(k):
        return pl.BlockSpec((tm, D_MODEL), lambda b, i: (b * n_t + i, k))

    def halo(k):
        return pl.BlockSpec((HALO, D_MODEL), lambda b, i: (jnp.maximum((b * n_t + i) * hb - 1, 0), k))

    def full(shape):
        return pl.BlockSpec(shape, lambda b, i: (0,) * len(shape))

    def attn(k):
        return pl.BlockSpec((Q_BLOCK, D_MODEL), lambda b, i: (
            _attn_block_row(b, i * (tm // Q_BLOCK) + k, m // seq, seq // Q_BLOCK), 0))

    row = pl.BlockSpec((tm, D_MODEL), lambda b, i: (b * n_t + i, 0))
    wspec = pl.BlockSpec((D_MODEL, D_MODEL), lambda b, i: (0, 0), pipeline_mode=pl.Buffered(1))
    return pl.pallas_call(
        _mix_body,
        grid=(m // seq, n_t),
        in_specs=[
            seg(SEG_GB), seg(SEG_GC), seg(SEG_VA), halo(SEG_GC), halo(SEG_VA), seg(SEG_U), seg(SEG_V),
            seg(SEG_G0), seg(SEG_G0 + 1), seg(SEG_G0 + 2), row,
            full((CONV_W, D_MODEL)), full((SGU_GROUPS, CHUNK, CHUNK)), full((CHUNK, D_MODEL)),
            full((1, D_MODEL)), full((1, D_MODEL)), wspec, wspec, wspec, wspec,
        ] + [attn(k) for k in range(tm // Q_BLOCK)],
        out_specs=row,
        out_shape=jax.ShapeDtypeStruct((m, D_MODEL), F32),
        scratch_shapes=[pltpu.VMEM((tm + HALO, D_MODEL), F32), pltpu.VMEM((tm, D_MODEL), F32)],
        compiler_params=_params(("parallel", "arbitrary")),
        name="mix",
    )(proj, proj, proj, proj, proj, proj, proj, proj, proj, proj, x2,
      conv_a, sgu_w, sgu_b2, ln_g, ln_b, w_a, w_b, w_c, w_o, *([out_b] * (tm // Q_BLOCK)))


def _ffn_body(x_ref, xh_ref, g_ref, wg_ref, wv_ref, cw_ref, wd_ref, fg_ref, o_ref, h_ref, gt_ref, acc_ref, *,
              tiles_per_seq, final):
    i = pl.program_id(0)
    j = pl.program_id(1)
    tm = x_ref.shape[0]

    @pl.when(j == 0)
    def _():
        h_ref[0:HALO, :] = _rms(xh_ref[...], g_ref[...]).astype(BF16)
        h_ref[HALO:, :] = _rms(x_ref[...], g_ref[...]).astype(BF16)
        acc_ref[...] = jnp.zeros_like(acc_ref)

    gate = jnp.dot(h_ref[...], wg_ref[...], preferred_element_type=F32)
    rows = lax.broadcasted_iota(jnp.int32, gate.shape, 0)
    seq_start = (i % tiles_per_seq) == 0
    gt_ref[...] = jnp.where(seq_start & (rows < HALO), 0.0, gate)
    cw = cw_ref[...]
    conv = (cw[0:1] * gt_ref[pl.ds(HALO - 2, tm), :] + cw[1:2] * gt_ref[pl.ds(HALO - 1, tm), :]
            + cw[2:3] * gt_ref[pl.ds(HALO, tm), :])
    val = jnp.dot(h_ref[HALO:, :], wv_ref[...], preferred_element_type=F32)
    act = (jax.nn.gelu(conv) * val).astype(BF16)
    acc_ref[...] += jnp.dot(act, wd_ref[...], preferred_element_type=F32)

    @pl.when(j == pl.num_programs(1) - 1)
    def _():
        y = x_ref[...] + acc_ref[...]
        o_ref[...] = _rms(y, fg_ref[...]) if final else y


def _ffn(x2, seq, g, w_up, conv_w, w_down, final_g, final, tm=1024, tf=512):
    m = x2.shape[0]
    n_f = D_FF // tf
    w_up = jnp.stack([w_up[:, k * tf:(k + 1) * tf] for k in range(2 * n_f)]).astype(BF16)
    hb = tm // HALO
    return pl.pallas_call(
        functools.partial(_ffn_body, tiles_per_seq=seq // tm, final=final),
        grid=(m // tm, n_f),
        in_specs=[
            pl.BlockSpec((tm, D_MODEL), lambda i, j: (i, 0)),
            pl.BlockSpec((HALO, D_MODEL), lambda i, j: (jnp.maximum(i * hb - 1, 0), 0)),
            pl.BlockSpec((1, D_MODEL), lambda i, j: (0, 0)),
            pl.BlockSpec((None, D_MODEL, tf), lambda i, j: (j, 0, 0)),
            pl.BlockSpec((None, D_MODEL, tf), lambda i, j: (n_f + j, 0, 0)),
            pl.BlockSpec((CONV_W, tf), lambda i, j: (0, j)),
            pl.BlockSpec((tf, D_MODEL), lambda i, j: (j, 0)),
            pl.BlockSpec((1, D_MODEL), lambda i, j: (0, 0)),
        ],
        out_specs=pl.BlockSpec((tm, D_MODEL), lambda i, j: (i, 0)),
        out_shape=jax.ShapeDtypeStruct((m, D_MODEL), F32),
        scratch_shapes=[
            pltpu.VMEM((tm + HALO, D_MODEL), BF16),
            pltpu.VMEM((tm + HALO, tf), F32),
            pltpu.VMEM((tm, D_MODEL), F32),
        ],
        compiler_params=_params(("parallel", "arbitrary")),
        name="ffn",
    )(x2, x2, g, w_up, w_up, conv_w, w_down, final_g)


def _split_w_in(w_in):
    c = [0]
    for sz in [D_MODEL] * 4 + [N_KV * HD] * 6 + [3 * N_HEADS] + [D_MODEL] * 5:
        c.append(c[-1] + sz)
    starts = c[0:4] + c[11:16]
    main = jnp.stack([w_in[:, a:a + D_MODEL] for a in starts]).astype(BF16)
    kv = w_in[:, c[4]:c[10]].reshape(D_MODEL, 6, N_KV, HD)[:, jnp.array([2, 3, 4, 5, 0, 1])]
    kv = kv.transpose(0, 2, 1, 3).reshape(D_MODEL, N_KVCOL)
    gate = jnp.pad(w_in[:, c[10]:c[11]], ((0, 0), (0, LANES - 3 * N_HEADS)))
    return main, kv.astype(BF16), gate.astype(BF16)


def _mixer_layer(x2, b_, s_, tt, bc, ovt, norm_g, w_in, conv_a, cmp_pos, cmp_w1, cmp_w2, sgu_w, sgu_b,
                 sgu_norm_g, sgu_norm_b, w_br_a, w_br_b, w_br_c, w_o):
    w_main, w_kv, w_gate = _split_w_in(w_in)
    norm_g = norm_g.reshape(1, D_MODEL)
    proj = _in_proj(x2, norm_g, w_main)
    kc, vc, ksa, vst, kwa, vwt, gt = _kv_proj(x2, b_, s_, norm_g, w_kv, w_gate)

    k_cmp, v_cmp = _compress(kc, vc, cmp_pos.reshape(2, 1, CMP_LEN * HD),
                             cmp_w1.astype(BF16), cmp_w2.astype(BF16))
    out_b = _nsa(proj, k_cmp, v_cmp.transpose(0, 1, 3, 2), ksa, vst, kwa, vwt, gt, tt, bc, ovt, s_)

    sgu_b2 = jnp.broadcast_to(sgu_b.T[:, :, None], (CHUNK, SGU_GROUPS, D_MODEL // SGU_GROUPS)).reshape(CHUNK, D_MODEL)
    return _mix(proj, out_b, x2, s_, conv_a, sgu_w, sgu_b2,
                sgu_norm_g.reshape(1, D_MODEL), sgu_norm_b.reshape(1, D_MODEL),
                w_br_a.astype(BF16), w_br_b.astype(BF16), w_br_c.astype(BF16), w_o.astype(BF16))


def _overlap_t(s_):
    n_blk = s_ // SEL_BLOCK
    cmp_start = jnp.arange(LANES) * CMP_STRIDE
    cmp_end = cmp_start + CMP_LEN - 1
    blk_start = jnp.arange(n_blk) * SEL_BLOCK
    n_cmp = (s_ - CMP_LEN) // CMP_STRIDE + 1
    ov = ((cmp_start[None, :] < blk_start[:, None] + SEL_BLOCK) & (cmp_end[None, :] >= blk_start[:, None])
          & (jnp.arange(LANES)[None, :] < n_cmp))
    return ov.astype(F32)


def kernel(x, rel_bias, norm_mix, w_in, conv_a, cmp_pos, cmp_w1, cmp_w2, sgu_w, sgu_b, sgu_norm_g, sgu_norm_b,
           w_br_a, w_br_b, w_br_c, w_o, norm_ffn, ffn_w_up, ffn_conv, ffn_w_down, norm_final):
    b_, s_, _ = x.shape
    depth = w_in.shape[0]
    x2 = x.reshape(b_ * s_, D_MODEL)
    tt, bc = _bias_tiles(rel_bias, s_ // Q_BLOCK)
    ovt = _overlap_t(s_)
    for l in range(depth):
        x2 = _mixer_layer(x2, b_, s_, tt, bc, ovt, norm_mix[l], w_in[l], conv_a[l], cmp_pos[l], cmp_w1[l],
                          cmp_w2[l], sgu_w[l], sgu_b[l], sgu_norm_g[l], sgu_norm_b[l],
                          w_br_a[l], w_br_b[l], w_br_c[l], w_o[l])
        x2 = _ffn(x2, s_, norm_ffn[l].reshape(1, D_MODEL), ffn_w_up[l], ffn_conv[l], ffn_w_down[l].astype(BF16),
                  norm_final.reshape(1, D_MODEL), final=(l == depth - 1))
    return x2.reshape(b_, s_, D_MODEL)
```

```python
import functools
import math

import jax
import jax.numpy as jnp
from jax import lax
from jax.experimental import pallas as pl
from jax.experimental.pallas import tpu as pltpu

D_MODEL = 1024
HD = 64
N_HEADS = D_MODEL // HD
N_KV = 4
Q_PER_KV = N_HEADS // N_KV
SGU_GROUPS = 8
CHUNK = 128
CONV_W = 3
CMP_LEN = 32
CMP_STRIDE = 16
CMP_HID = 2 * HD
SEL_BLOCK = 64
SEL_TOP_N = 8
WINDOW = 512
Q_BLOCK = 128
D_FF = 3 * D_MODEL
N_BUCKETS = 32
MAX_EXACT = N_BUCKETS // 2
REL_MAX_DIST = 128
SCALE = HD ** -0.5
EPS = 1e-6
NEG_INF = -1e30
FORCE = 1e9
LOG2E = 1.4426950408889634

LANES = 128
SUBLANES = 8
ROW_CHUNK = 256
HALO = 16
N_SEG = 9
SEG_GB, SEG_GC, SEG_VA, SEG_Q, SEG_U, SEG_V, SEG_G0 = 0, 1, 2, 3, 4, 5, 6
N_MAIN = N_SEG * D_MODEL
N_KVCOL = 6 * N_KV * HD
KV_PAD = WINDOW
VMEM_LIMIT = 56 * 1024 * 1024

BF16 = jnp.bfloat16
F32 = jnp.float32
_NT = (((1,), (1,)), ((), ()))


def _params(sem):
    return pltpu.CompilerParams(dimension_semantics=sem, vmem_limit_bytes=VMEM_LIMIT)


def _rms(x, g):
    return x * lax.rsqrt(jnp.mean(x * x, axis=-1, keepdims=True) + EPS) * g


def _in_proj_body(x_ref, g_ref, w_ref, o_ref, h_ref, *, gelu_tiles, sigmoid_tiles):
    j = pl.program_id(1)

    @pl.when(j == 0)
    def _():
        h_ref[...] = _rms(x_ref[...], g_ref[...]).astype(BF16)

    def tile(act):
        for r in range(0, h_ref.shape[0], ROW_CHUNK):
            y = jnp.dot(h_ref[r:r + ROW_CHUNK, :], w_ref[...], preferred_element_type=F32)
            o_ref[r:r + ROW_CHUNK, :] = act(y).astype(o_ref.dtype)

    is_gelu = (j >= gelu_tiles[0]) & (j < gelu_tiles[1])
    is_sigmoid = (j >= sigmoid_tiles[0]) & (j < sigmoid_tiles[1])
    pl.when(is_gelu)(lambda: tile(jax.nn.gelu))
    pl.when(is_sigmoid)(lambda: tile(jax.nn.sigmoid))
    pl.when(jnp.logical_not(is_gelu | is_sigmoid))(lambda: tile(lambda y: y))


def _in_proj(x2, g, w_main, tm=2048):
    m = x2.shape[0]
    tn = D_MODEL
    body = functools.partial(_in_proj_body, gelu_tiles=(SEG_U, SEG_V + 1), sigmoid_tiles=(SEG_G0, SEG_G0 + 3))
    return pl.pallas_call(
        body,
        grid=(m // tm, N_SEG),
        in_specs=[
            pl.BlockSpec((tm, D_MODEL), lambda i, j: (i, 0)),
            pl.BlockSpec((1, D_MODEL), lambda i, j: (0, 0)),
            pl.BlockSpec((None, D_MODEL, tn), lambda i, j: (j, 0, 0)),
        ],
        out_specs=pl.BlockSpec((tm, tn), lambda i, j: (i, j)),
        out_shape=jax.ShapeDtypeStruct((m, N_MAIN), BF16),
        scratch_shapes=[pltpu.VMEM((tm, D_MODEL), BF16)],
        compiler_params=_params(("parallel", "arbitrary")),
        name="in_proj",
    )(x2, g, w_main)


def _kv_proj_body(x_ref, g_ref, w_ref, wg_ref, kc_ref, vc_ref, ksa_ref, vst_ref, kwa_ref, vwt_ref, gt_ref, cmp_ref):
    s_id = pl.program_id(1)
    tm = x_ref.shape[0]
    lane = lax.broadcasted_iota(jnp.int32, (tm, LANES), 1)
    ones_rows = (lax.broadcasted_iota(jnp.int32, (ONES_ROWS, tm), 0) == 0).astype(BF16)

    @pl.when(s_id == 0)
    def _():
        pad_keys = jnp.where(lane == HD + PAD_COL, 1.0, 0.0).astype(BF16)
        for g in range(N_KV):
            ksa_ref[g] = pad_keys
            kwa_ref[g] = pad_keys
            vst_ref[g] = jnp.zeros(vst_ref.shape[1:], BF16)
            vwt_ref[g] = jnp.zeros(vwt_ref.shape[1:], BF16)

    @pl.when(s_id > 0)
    def _():
        h = _rms(x_ref[...], g_ref[...]).astype(BF16)
        y = jnp.dot(h, w_ref[...], preferred_element_type=F32)
        tok = (s_id - 1) * tm + lax.broadcasted_iota(jnp.int32, (tm, LANES), 0)
        blk_id = jnp.where(lane - HD == tok // SEL_BLOCK, 1.0, 0.0)
        for g in range(N_KV):
            base = g * 6 * HD
            sel = y[:, base:base + 2 * HD]
            win = y[:, base + 2 * HD:base + 4 * HD]
            cmp = y[:, base + 4 * HD:base + 6 * HD]
            ksa_ref[g] = jnp.where(lane < HD, sel, blk_id).astype(BF16)
            kwa_ref[g] = jnp.where(lane < HD, win, 0.0).astype(BF16)
            vst_ref[g, 0:HD, :] = sel.T[HD:].astype(BF16)
            vst_ref[g, HD:, :] = ones_rows
            vwt_ref[g, 0:HD, :] = win.T[HD:].astype(BF16)
            vwt_ref[g, HD:, :] = ones_rows
            cmp_ref[...] = cmp
            for c in range(CMP_STRIDE):
                every = cmp_ref[pl.ds(c, tm // CMP_STRIDE, stride=CMP_STRIDE), :].astype(BF16)
                kc_ref[g, :, c * HD:(c + 1) * HD] = every[:, 0:HD]
                vc_ref[g, :, c * HD:(c + 1) * HD] = every[:, HD:]
        gt_ref[...] = jax.nn.sigmoid(jnp.dot(h, wg_ref[...], preferred_element_type=F32)).T


def _kv_proj(x2, b_, s_, g, w_kv, w_gate, tm=512):
    n_t = s_ // tm
    assert KV_PAD == tm

    def tok(b, s):
        return jnp.maximum(s - 1, 0)

    return pl.pallas_call(
        _kv_proj_body,
        grid=(b_, n_t + 1),
        in_specs=[
            pl.BlockSpec((tm, D_MODEL), lambda b, s: (b * n_t + tok(b, s), 0)),
            pl.BlockSpec((1, D_MODEL), lambda b, s: (0, 0)),
            pl.BlockSpec((D_MODEL, N_KVCOL), lambda b, s: (0, 0), pipeline_mode=pl.Buffered(1)),
            pl.BlockSpec((D_MODEL, LANES), lambda b, s: (0, 0), pipeline_mode=pl.Buffered(1)),
        ],
        out_specs=[
            pl.BlockSpec((None, N_KV, tm // CMP_STRIDE, CMP_STRIDE * HD), lambda b, s: (b, 0, tok(b, s), 0)),
            pl.BlockSpec((None, N_KV, tm // CMP_STRIDE, CMP_STRIDE * HD), lambda b, s: (b, 0, tok(b, s), 0)),
            pl.BlockSpec((None, N_KV, tm, 2 * HD), lambda b, s: (b, 0, s, 0)),
            pl.BlockSpec((None, N_KV, HD + ONES_ROWS, tm), lambda b, s: (b, 0, 0, s)),
            pl.BlockSpec((None, N_KV, tm, 2 * HD), lambda b, s: (b, 0, s, 0)),
            pl.BlockSpec((None, N_KV, HD + ONES_ROWS, tm), lambda b, s: (b, 0, 0, s)),
            pl.BlockSpec((None, LANES, tm), lambda b, s: (b, 0, tok(b, s))),
        ],
        out_shape=[
            jax.ShapeDtypeStruct((b_, N_KV, s_ // CMP_STRIDE, CMP_STRIDE * HD), BF16),
            jax.ShapeDtypeStruct((b_, N_KV, s_ // CMP_STRIDE, CMP_STRIDE * HD), BF16),
            jax.ShapeDtypeStruct((b_, N_KV, KV_PAD + s_, 2 * HD), BF16),
            jax.ShapeDtypeStruct((b_, N_KV, HD + ONES_ROWS, KV_PAD + s_), BF16),
            jax.ShapeDtypeStruct((b_, N_KV, KV_PAD + s_, 2 * HD), BF16),
            jax.ShapeDtypeStruct((b_, N_KV, HD + ONES_ROWS, KV_PAD + s_), BF16),
            jax.ShapeDtypeStruct((b_, LANES, s_), F32),
        ],
        scratch_shapes=[pltpu.VMEM((tm, 2 * HD), F32)],
        compiler_params=_params(("parallel", "arbitrary")),
        name="kv_proj",
    )(x2, g, w_kv, w_gate)


def _rel_bucket(dist):
    dist = jnp.maximum(dist, 0)
    log_ratio = jnp.log(jnp.maximum(dist, 1).astype(F32) / MAX_EXACT) / math.log(REL_MAX_DIST / MAX_EXACT)
    large = MAX_EXACT + (log_ratio * (N_BUCKETS - MAX_EXACT)).astype(jnp.int32)
    return jnp.where(dist < MAX_EXACT, dist, jnp.minimum(large, N_BUCKETS - 1))


def _bias_body(tab_ref, bt_ref, bc_ref, tt_ref, bco_ref, *, n_cmp):
    h = pl.program_id(0)

    def lookup(bk):
        acc = jnp.zeros(bk.shape, F32)
        for b in range(N_BUCKETS):
            acc = jnp.where(bk == b, tab_ref[b, h], acc)
        return acc

    kk = lax.broadcasted_iota(jnp.int32, (LANES, LANES), 0)
    qq = lax.broadcasted_iota(jnp.int32, (LANES, LANES), 1)
    far = lookup(bt_ref[2])
    tt_ref[0] = (lookup(bt_ref[1]) - far) * LOG2E
    tt_ref[1] = jnp.where(qq >= kk, (lookup(bt_ref[0]) - far) * LOG2E, NEG_INF)
    tt_ref[2] = jnp.where(kk > qq, 0.0, NEG_INF)
    for i in range(bc_ref.shape[0]):
        ok = (i * Q_BLOCK + qq - (kk * CMP_STRIDE + CMP_LEN - 1) >= 0) & (kk < n_cmp)
        bco_ref[i] = jnp.where(ok, lookup(bc_ref[i]), NEG_INF)


def _bias_tiles(rel_bias, n_qb):
    kk = jnp.arange(LANES)[:, None]
    qq = jnp.arange(LANES)[None, :]
    bt = jnp.stack([_rel_bucket(LANES * d + qq - kk) for d in range(3)]).astype(jnp.int32)
    cmp_end = CMP_STRIDE * jnp.arange(LANES) + CMP_LEN - 1
    t = (Q_BLOCK * jnp.arange(n_qb))[:, None, None] + qq[None]
    bc = _rel_bucket(t - cmp_end[None, :, None]).astype(jnp.int32)
    n_cmp = (n_qb * Q_BLOCK - CMP_LEN) // CMP_STRIDE + 1
    return pl.pallas_call(
        functools.partial(_bias_body, n_cmp=n_cmp),
        grid=(N_HEADS,),
        in_specs=[
            pl.BlockSpec(memory_space=pltpu.SMEM),
            pl.BlockSpec((3, LANES, LANES), lambda h: (0, 0, 0)),
            pl.BlockSpec((n_qb, LANES, LANES), lambda h: (0, 0, 0)),
        ],
        out_specs=[
            pl.BlockSpec((None, 3, LANES, LANES), lambda h: (h // Q_PER_KV, 0, 0, h % Q_PER_KV)),
            pl.BlockSpec((None, n_qb, LANES, LANES), lambda h: (h // Q_PER_KV, 0, 0, h % Q_PER_KV)),
        ],
        out_shape=[
            jax.ShapeDtypeStruct((N_KV, 3, LANES, Q_PER_KV * LANES), F32),
            jax.ShapeDtypeStruct((N_KV, n_qb, LANES, Q_PER_KV * LANES), F32),
        ],
        compiler_params=_params(("arbitrary",)),
        name="bias_tiles",
    )(rel_bias.astype(F32), bt, bc)


def _compress_body(k3_ref, v3_ref, pos_ref, w1_ref, w2_ref, ko_ref, vo_ref):
    half = CMP_STRIDE * HD
    for which, (src, dst) in enumerate(((k3_ref, ko_ref), (v3_ref, vo_ref))):
        x = src[...].astype(F32).reshape(-1, src.shape[-1])
        pos = pos_ref[which]
        xa = (x + pos[:, :half]).astype(BF16)
        xb = (x + pos[:, half:]).astype(BF16)
        a = jnp.dot(xa, w1_ref[which, :half, :], preferred_element_type=F32)
        b = jnp.dot(xb, w1_ref[which, half:, :], preferred_element_type=F32)
        hid = jax.nn.gelu(a + pltpu.roll(b, b.shape[0] - 1, 0))
        out = jnp.dot(hid.astype(BF16), w2_ref[which], preferred_element_type=F32)
        dst[...] = out.reshape(dst.shape).astype(dst.dtype)


def _compress(k3, v3, pos, w1, w2):
    b_, g_, nm, width = k3.shape
    spec3 = pl.BlockSpec((None, g_, nm, width), lambda b: (b, 0, 0, 0))
    ospec = pl.BlockSpec((None, g_, nm, HD), lambda b: (b, 0, 0, 0))
    return pl.pallas_call(
        _compress_body,
        grid=(b_,),
        in_specs=[
            spec3, spec3,
            pl.BlockSpec((2, 1, CMP_LEN * HD), lambda b: (0, 0, 0)),
            pl.BlockSpec((2, CMP_LEN * HD, CMP_HID), lambda b: (0, 0, 0)),
            pl.BlockSpec((2, CMP_HID, HD), lambda b: (0, 0, 0)),
        ],
        out_specs=[ospec, ospec],
        out_shape=[jax.ShapeDtypeStruct((b_, g_, nm, HD), BF16)] * 2,
        compiler_params=_params(("parallel",)),
        name="compress",
    )(k3, v3, pos, w1, w2)


FAR_TILES = 2
PAD_COL = 32
PAIRS = 8
ONES_ROWS = 16


def _nsa_body(qa_ref, qb_ref, kc_ref, vct_ref, ksa_ref, vst_ref, kwa_ref, vwt_ref, gta_ref, gtb_ref,
              tt_ref, bca_ref, bcb_ref, ovt_ref, o_ref, sbuf_ref, qf_ref):
    p_id = pl.program_id(2)
    n_qb = 2 * PAIRS * pl.num_programs(2)
    nl = Q_PER_KV * LANES
    n_win = WINDOW // LANES + 1
    n_blk = ovt_ref.shape[0]
    max_chunks = sbuf_ref.shape[0] // PAIRS
    pad_tiles = KV_PAD // LANES
    j_io = lax.broadcasted_iota(jnp.int32, (n_blk, LANES), 0)
    q_io = lax.broadcasted_iota(jnp.int32, (n_blk, LANES), 1)
    col_w = lax.broadcasted_iota(jnp.int32, (nl, HD), 1)
    pad_cols = jnp.where(col_w == PAD_COL, NEG_INF, 0.0).astype(BF16)

    def compressed(qt, bc):
        q = jnp.concatenate([qt[:, r * HD:(r + 1) * HD] for r in range(Q_PER_KV)], axis=0) * SCALE
        s = lax.dot_general(kc_ref[...], q, _NT, preferred_element_type=F32) + bc
        m_c = jnp.max(s, axis=0, keepdims=True)
        p = jnp.exp(s - m_c)
        l_c = jnp.sum(p, axis=0, keepdims=True)
        p_c = p * (jnp.where(m_c > 0.5 * NEG_INF, 1.0, 0.0) / l_c)
        o_c = jnp.dot(vct_ref[...], p_c.astype(BF16), preferred_element_type=F32)
        return o_c, p_c, (q.astype(F32) * LOG2E).astype(BF16)

    def select(i, p_c, q2, which):
        p_sum = p_c[:, 0:LANES]
        for r in range(1, Q_PER_KV):
            p_sum = p_sum + p_c[:, r * LANES:(r + 1) * LANES]
        imp = jnp.dot(ovt_ref[...], p_sum, precision=lax.Precision.HIGHEST, preferred_element_type=F32)
        t_b = i * Q_BLOCK + q_io
        cur = lax.shift_right_logical(t_b, SEL_BLOCK.bit_length() - 1)
        forced = (j_io == 0) | (j_io == cur) | (j_io == cur - 1)
        valid = j_io * SEL_BLOCK <= t_b
        score = jnp.where(forced, FORCE, jnp.where(valid, imp, -FORCE))
        rank = jnp.zeros((n_blk, LANES), jnp.int32)
        for ii in range(n_blk):
            row = score[ii:ii + 1, :]
            beats = (row > score) | ((row == score) & (j_io > ii))
            rank = rank + beats.astype(jnp.int32)
        keep = (rank < min(SEL_TOP_N, n_blk)) & (score >= 0.0)

        def augment(kept):
            cols = jnp.concatenate([jnp.where(kept, 0.0, NEG_INF), jnp.full((SUBLANES, LANES), NEG_INF, F32),
                                    jnp.zeros((LANES - n_blk - SUBLANES, LANES), F32)], axis=0)
            cols_t = cols.T[:, 0:HD].astype(BF16)
            return jnp.concatenate([q2, jnp.concatenate([cols_t] * Q_PER_KV, axis=0)], axis=1)

        qf_ref[which] = augment(keep & (j_io < 2 * (i - 1)))
        return augment(keep)

    def softmax_av(tiles, vt):
        s_all = jnp.concatenate(tiles, axis=0)
        m = jnp.max(s_all, axis=0, keepdims=True)
        p = jnp.exp2(s_all - m)
        return m, jnp.dot(vt, p.astype(BF16), preferred_element_type=F32)

    def normalise(av):
        return av[0:HD] / av[HD:HD + 1]

    def logit_tiles(k_ref, qq, row0, n):
        s_all = lax.dot_general(k_ref[pl.ds(row0, n * LANES), :], qq, _NT, preferred_element_type=F32)
        return [s_all[t * LANES:(t + 1) * LANES] for t in range(n)]

    def window_logits(i, q2):
        r0 = pl.multiple_of((i + pad_tiles + 1 - n_win) * LANES, LANES)
        tw = logit_tiles(kwa_ref, jnp.concatenate([q2, pad_cols], axis=1), r0, n_win)
        tw[0] = tw[0] + tt_ref[2]
        tw[n_win - 2] = tw[n_win - 2] + tt_ref[0]
        tw[n_win - 1] = tw[n_win - 1] + tt_ref[1]
        return tw

    def window_softmax(i, tw):
        r0 = pl.multiple_of((i + pad_tiles + 1 - n_win) * LANES, LANES)
        return normalise(softmax_av(tw, vwt_ref[:, pl.ds(r0, n_win * LANES)])[1])

    def near_logits(i, q_near):
        r1 = pl.multiple_of((i + pad_tiles - 1) * LANES, LANES)
        tn = logit_tiles(ksa_ref, q_near, r1, 2)
        tn[0] = tn[0] + tt_ref[0]
        tn[1] = tn[1] + tt_ref[1]
        return tn

    def near_softmax(i, tn):
        r1 = pl.multiple_of((i + pad_tiles - 1) * LANES, LANES)
        return softmax_av(tn, vst_ref[:, pl.ds(r1, 2 * LANES)])

    def pair(u):
        v = PAIRS - 1 - u
        i_a = PAIRS * p_id + u
        i_b = n_qb - 1 - i_a
        rows_a = slice(u * Q_BLOCK, (u + 1) * Q_BLOCK)
        rows_b = slice(v * Q_BLOCK, (v + 1) * Q_BLOCK)
        oc_a, pc_a, q2_a = compressed(qa_ref[rows_a, :], bca_ref[u])
        oc_b, pc_b, q2_b = compressed(qb_ref[rows_b, :], bcb_ref[v])
        tw_a = window_logits(i_a, q2_a)
        tw_b = window_logits(i_b, q2_b)
        qn_a = select(i_a, pc_a, q2_a, 2 * u)
        ow_a = window_softmax(i_a, tw_a)
        qn_b = select(i_b, pc_b, q2_b, 2 * u + 1)
        ow_b = window_softmax(i_b, tw_b)
        tn_a = near_logits(i_a, qn_a)
        tn_b = near_logits(i_b, qn_b)
        mn_a, avn_a = near_softmax(i_a, tn_a)
        mn_b, avn_b = near_softmax(i_b, tn_b)

        n_a = (jnp.maximum(i_a - 1, 0) + FAR_TILES - 1) // FAR_TILES
        is_a, rows, mx = [], [], []
        for k in range(max_chunks):
            own = k < n_a
            c = jnp.where(own, k, k - n_a)
            r = pl.multiple_of((c * FAR_TILES + pad_tiles) * LANES, LANES)
            s_k = lax.dot_general(ksa_ref[pl.ds(r, FAR_TILES * LANES), :], qf_ref[2 * u + jnp.where(own, 0, 1)],
                                  _NT, preferred_element_type=F32)
            sbuf_ref[u * max_chunks + k] = s_k
            is_a.append(own)
            rows.append(r)
            mx.append(jnp.max(s_k, axis=0, keepdims=True))
        m_a, m_b = mn_a, mn_b
        for k in range(max_chunks):
            m_a = jnp.where(is_a[k], jnp.maximum(m_a, mx[k]), m_a)
            m_b = jnp.where(is_a[k], m_b, jnp.maximum(m_b, mx[k]))
        av_a = jnp.exp2(mn_a - m_a) * avn_a
        av_b = jnp.exp2(mn_b - m_b) * avn_b
        for k in range(max_chunks):
            p = jnp.exp2(sbuf_ref[u * max_chunks + k] - jnp.where(is_a[k], m_a, m_b))
            pv = jnp.dot(vst_ref[:, pl.ds(rows[k], FAR_TILES * LANES)], p.astype(BF16),
                         preferred_element_type=F32)
            av_a = av_a + jnp.where(is_a[k], pv, 0.0)
            av_b = av_b + jnp.where(is_a[k], 0.0, pv)

        def finish(half, gt_ref, w, o_c, o_s, o_w):
            def gate(branch):
                g_id = pl.program_id(1)
                start = pl.multiple_of(branch * N_HEADS + (g_id // 2) * SUBLANES, SUBLANES)
                tile = gt_ref[pl.ds(start, SUBLANES), :]
                rows_g = jnp.where(g_id % 2 == 0, tile[0:Q_PER_KV], tile[Q_PER_KV:])
                return jnp.concatenate([rows_g[r:r + 1, w * Q_BLOCK:(w + 1) * Q_BLOCK] for r in range(Q_PER_KV)],
                                       axis=1)

            o_t = gate(0) * o_c + gate(1) * o_s + gate(2) * o_w
            for pr in range(Q_PER_KV // 2):
                blk = jnp.concatenate([o_t[:, (2 * pr) * LANES:(2 * pr + 1) * LANES],
                                       o_t[:, (2 * pr + 1) * LANES:(2 * pr + 2) * LANES]], axis=0)
                o_ref[half, u, :, pr * LANES:(pr + 1) * LANES] = blk.T.astype(o_ref.dtype)

        finish(0, gta_ref, u, oc_a, normalise(av_a), ow_a)
        finish(1, gtb_ref, v, oc_b, normalise(av_b), ow_b)

    for u in range(PAIRS):
        pair(u)


def _nsa(proj, kc, vct, ksa, vst, kwa, vwt, gt, tt, bc, ovt, s_):
    b_, g_ = kwa.shape[:2]
    n_qb = s_ // Q_BLOCK
    n_p = n_qb // 2
    n_s = n_p // PAIRS
    rows = PAIRS * Q_BLOCK
    m = proj.shape[0]
    nl = Q_PER_KV * LANES
    qcol = SEG_Q * D_MODEL // (Q_PER_KV * HD)

    def far_chunks(i):
        return (max(i - 1, 0) + FAR_TILES - 1) // FAR_TILES

    max_chunks = far_chunks(0) + far_chunks(n_qb - 1)
    assert n_qb % (2 * PAIRS) == 0 and all(
        far_chunks(p) + far_chunks(n_qb - 1 - p) == max_chunks for p in range(n_p))

    def bg(shape):
        return pl.BlockSpec((None, None) + shape, lambda b, g, p: (b, g, 0, 0))

    def blk_a(p):
        return p

    def blk_b(p):
        return 2 * n_s - 1 - p

    def per_block(which):
        return [
            pl.BlockSpec((rows, Q_PER_KV * HD), lambda b, g, p: (b * 2 * n_s + which(p), qcol + g)),
            pl.BlockSpec((None, LANES, rows), lambda b, g, p: (b, 0, which(p))),
            pl.BlockSpec((None, PAIRS, LANES, nl), lambda b, g, p: (g, which(p), 0, 0)),
        ]

    qa, gta, bca = per_block(blk_a)
    qb, gtb, bcb = per_block(blk_b)
    out = pl.pallas_call(
        _nsa_body,
        grid=(b_, g_, n_s),
        in_specs=[
            qa, qb,
            bg((LANES, HD)), bg((HD, LANES)),
            bg((s_ + KV_PAD, 2 * HD)), bg((HD + ONES_ROWS, s_ + KV_PAD)),
            bg((s_ + KV_PAD, 2 * HD)), bg((HD + ONES_ROWS, s_ + KV_PAD)),
            gta, gtb,
            pl.BlockSpec((None, 3, LANES, nl), lambda b, g, p: (g, 0, 0, 0)),
            bca, bcb,
            pl.BlockSpec((s_ // SEL_BLOCK, LANES), lambda b, g, p: (0, 0)),
        ],
        out_specs=pl.BlockSpec((2, PAIRS, Q_BLOCK, Q_PER_KV * HD), lambda b, g, p: (0, b * n_s + p, 0, g)),
        out_shape=jax.ShapeDtypeStruct((2, b_ * n_p, Q_BLOCK, D_MODEL), BF16),
        scratch_shapes=[pltpu.VMEM((PAIRS * max_chunks, FAR_TILES * LANES, nl), F32),
                        pltpu.VMEM((2 * PAIRS, nl, 2 * HD), BF16)],
        compiler_params=_params(("parallel", "parallel", "arbitrary")),
        name="nsa",
    )(proj, proj, kc, vct, ksa, vst, kwa, vwt, gt, gt, tt, bc, bc, ovt)
    return out.reshape(m, D_MODEL)


def _attn_block_row(b, qb, n_b, n_qb):
    n_p = n_qb // 2
    upper = qb >= n_p
    return jnp.where(upper, n_b * n_p, 0) + b * n_p + jnp.where(upper, n_qb - 1 - qb, qb)


def _mix_body(gb_ref, gc_ref, va_ref, gch_ref, vah_ref, uc_ref, vc_ref, g0_ref, g1_ref, g2_ref, x_ref,
              ca_ref, sw_ref, sb_ref, lng_ref, lnb_ref, wa_ref, wb_ref, wc_ref, wo_ref, *rest):
    tm = gb_ref.shape[0]
    ob_refs = rest[:tm // Q_BLOCK]
    o_ref, xa_ref, mx_ref = rest[tm // Q_BLOCK:]
    first = pl.program_id(1) == 0

    halo = gch_ref[...].astype(F32) * vah_ref[...].astype(F32)
    xa_ref[0:HALO, :] = jnp.where(first, 0.0, halo)
    xa_ref[HALO:, :] = gc_ref[...].astype(F32) * va_ref[...].astype(F32)
    ca = ca_ref[...]
    row = lax.broadcasted_iota(jnp.int32, (CHUNK, CHUNK), 0)
    col = lax.broadcasted_iota(jnp.int32, (CHUNK, CHUNK), 1)
    sw = [jnp.where(row >= col, sw_ref[gi], 0.0).astype(BF16) for gi in range(SGU_GROUPS)]

    for r in range(0, tm, ROW_CHUNK):
        rows = slice(r, r + ROW_CHUNK)
        conv = (ca[0:1] * xa_ref[pl.ds(r + HALO - 2, ROW_CHUNK), :]
                + ca[1:2] * xa_ref[pl.ds(r + HALO - 1, ROW_CHUNK), :]
                + ca[2:3] * xa_ref[pl.ds(r + HALO, ROW_CHUNK), :])
        out_a = gb_ref[rows, :].astype(F32) * conv
        y_a = jnp.dot(out_a.astype(BF16), wa_ref[...], preferred_element_type=F32)

        v = vc_ref[rows, :].astype(F32)
        mu = jnp.mean(v, axis=-1, keepdims=True)
        vz = v - mu
        var = jnp.mean(vz * vz, axis=-1, keepdims=True)
        vn = (vz * lax.rsqrt(var + EPS) * lng_ref[...] + lnb_ref[...]).astype(BF16)
        for gi in range(SGU_GROUPS):
            cols = slice(gi * LANES, (gi + 1) * LANES)
            for c in range(0, ROW_CHUNK, CHUNK):
                mx_ref[r + c:r + c + CHUNK, cols] = (
                    jnp.dot(sw[gi], vn[c:c + CHUNK, cols], preferred_element_type=F32) + sb_ref[:, cols])
        out_c = uc_ref[rows, :].astype(F32) * mx_ref[rows, :]
        y_c = jnp.dot(out_c.astype(BF16), wc_ref[...], preferred_element_type=F32)

        out_b = jnp.concatenate([ob_refs[k][...] for k in range(r // Q_BLOCK, (r + ROW_CHUNK) // Q_BLOCK)], axis=0)
        y_b = jnp.dot(out_b, wb_ref[...], preferred_element_type=F32)

        merged = (g0_ref[rows, :].astype(F32) * y_a + g1_ref[rows, :].astype(F32) * y_b
                  + g2_ref[rows, :].astype(F32) * y_c)
        o_ref[rows, :] = x_ref[rows, :] + jnp.dot(merged.astype(BF16), wo_ref[...], preferred_element_type=F32)


def _mix(proj, out_b, x2, seq, conv_a, sgu_w, sgu_b2, ln_g, ln_b, w_a, w_b, w_c, w_o, tm=512):
    m = x2.shape[0]
    n_t = seq // tm
    hb = tm // HALO

    def seg(k):
        return pl.BlockSpec((tm, D_MODEL), lambda b, i: (b * n_t + i, k))

    def halo(k):
        return pl.BlockSpec((HALO, D_MODEL), lambda b, i: (jnp.maximum((b * n_t + i) * hb - 1, 0), k))

    def full(shape):
        return pl.BlockSpec(shape, lambda b, i: (0,) * len(shape))

    def attn(k):
        return pl.BlockSpec((Q_BLOCK, D_MODEL), lambda b, i: (
            _attn_block_row(b, i * (tm // Q_BLOCK) + k, m // seq, seq // Q_BLOCK), 0))

    row = pl.BlockSpec((tm, D_MODEL), lambda b, i: (b * n_t + i, 0))
    wspec = pl.BlockSpec((D_MODEL, D_MODEL), lambda b, i: (0, 0), pipeline_mode=pl.Buffered(1))
    return pl.pallas_call(
        _mix_body,
        grid=(m // seq, n_t),
        in_specs=[
            seg(SEG_GB), seg(SEG_GC), seg(SEG_VA), halo(SEG_GC), halo(SEG_VA), seg(SEG_U), seg(SEG_V),
            seg(SEG_G0), seg(SEG_G0 + 1), seg(SEG_G0 + 2), row,
            full((CONV_W, D_MODEL)), full((SGU_GROUPS, CHUNK, CHUNK)), full((CHUNK, D_MODEL)),
            full((1, D_MODEL)), full((1, D_MODEL)), wspec, wspec, wspec, wspec,
        ] + [attn(k) for k in range(tm // Q_BLOCK)],
        out_specs=row,
        out_shape=jax.ShapeDtypeStruct((m, D_MODEL), F32),
        scratch_shapes=[pltpu.VMEM((tm + HALO, D_MODEL), F32), pltpu.VMEM((tm, D_MODEL), F32)],
        compiler_params=_params(("parallel", "arbitrary")),
        name="mix",
    )(proj, proj, proj, proj, proj, proj, proj, proj, proj, proj, x2,
      conv_a, sgu_w, sgu_b2, ln_g, ln_b, w_a, w_b, w_c, w_o, *([out_b] * (tm // Q_BLOCK)))


def _ffn_body(x_ref, xh_ref, g_ref, wg_ref, wv_ref, cw_ref, wd_ref, fg_ref, o_ref, h_ref, gt_ref, acc_ref, *,
              tiles_per_seq, final):
    i = pl.program_id(0)
    j = pl.program_id(1)
    tm = x_ref.shape[0]

    @pl.when(j == 0)
    def _():
        h_ref[0:HALO, :] = _rms(xh_ref[...], g_ref[...]).astype(BF16)
        h_ref[HALO:, :] = _rms(x_ref[...], g_ref[...]).astype(BF16)
        acc_ref[...] = jnp.zeros_like(acc_ref)

    seq_start = (i % tiles_per_seq) == 0
    cw = cw_ref[...]
    rc = gt_ref.shape[1] - HALO
    for c in range(tm // rc):
        r = c * rc
        gate = jnp.dot(h_ref[r:r + HALO + rc, :], wg_ref[...], preferred_element_type=F32)
        if c == 0:
            rows = lax.broadcasted_iota(jnp.int32, gate.shape, 0)
            gate = jnp.where(seq_start & (rows < HALO), 0.0, gate)
        gt_ref[c] = gate
        conv = (cw[0:1] * gt_ref[c, pl.ds(HALO - 2, rc), :] + cw[1:2] * gt_ref[c, pl.ds(HALO - 1, rc), :]
                + cw[2:3] * gt_ref[c, pl.ds(HALO, rc), :])
        val = jnp.dot(h_ref[HALO + r:HALO + r + rc, :], wv_ref[...], preferred_element_type=F32)
        act = (jax.nn.gelu(conv) * val).astype(BF16)
        acc_ref[r:r + rc, :] += jnp.dot(act, wd_ref[...], preferred_element_type=F32)

    @pl.when(j == pl.num_programs(1) - 1)
    def _():
        y = x_ref[...] + acc_ref[...]
        o_ref[...] = _rms(y, fg_ref[...]) if final else y


def _ffn(x2, seq, g, w_up, conv_w, w_down, final_g, final, tm=1024, tf=512):
    m = x2.shape[0]
    n_f = D_FF // tf
    w_up = jnp.stack([w_up[:, k * tf:(k + 1) * tf] for k in range(2 * n_f)]).astype(BF16)
    hb = tm // HALO
    return pl.pallas_call(
        functools.partial(_ffn_body, tiles_per_seq=seq // tm, final=final),
        grid=(m // tm, n_f),
        in_specs=[
            pl.BlockSpec((tm, D_MODEL), lambda i, j: (i, 0)),
            pl.BlockSpec((HALO, D_MODEL), lambda i, j: (jnp.maximum(i * hb - 1, 0), 0)),
            pl.BlockSpec((1, D_MODEL), lambda i, j: (0, 0)),
            pl.BlockSpec((None, D_MODEL, tf), lambda i, j: (j, 0, 0)),
            pl.BlockSpec((None, D_MODEL, tf), lambda i, j: (n_f + j, 0, 0)),
            pl.BlockSpec((CONV_W, tf), lambda i, j: (0, j)),
            pl.BlockSpec((tf, D_MODEL), lambda i, j: (j, 0)),
            pl.BlockSpec((1, D_MODEL), lambda i, j: (0, 0)),
        ],
        out_specs=pl.BlockSpec((tm, D_MODEL), lambda i, j: (i, 0)),
        out_shape=jax.ShapeDtypeStruct((m, D_MODEL), F32),
        scratch_shapes=[
            pltpu.VMEM((tm + HALO, D_MODEL), BF16),
            pltpu.VMEM((2, tm // 2 + HALO, tf), F32),
            pltpu.VMEM((tm, D_MODEL), F32),
        ],
        compiler_params=_params(("parallel", "arbitrary")),
        name="ffn",
    )(x2, x2, g, w_up, w_up, conv_w, w_down, final_g)


def _split_w_in(w_in):
    c = [0]
    for sz in [D_MODEL] * 4 + [N_KV * HD] * 6 + [3 * N_HEADS] + [D_MODEL] * 5:
        c.append(c[-1] + sz)
    starts = c[0:4] + c[11:16]
    main = jnp.stack([w_in[:, a:a + D_MODEL] for a in starts]).astype(BF16)
    kv = w_in[:, c[4]:c[10]].reshape(D_MODEL, 6, N_KV, HD)[:, jnp.array([2, 3, 4, 5, 0, 1])]
    kv = kv.transpose(0, 2, 1, 3).reshape(D_MODEL, N_KVCOL)
    gate = jnp.pad(w_in[:, c[10]:c[11]], ((0, 0), (0, LANES - 3 * N_HEADS)))
    return main, kv.astype(BF16), gate.astype(BF16)


def _mixer_layer(x2, b_, s_, tt, bc, ovt, norm_g, w_in, conv_a, cmp_pos, cmp_w1, cmp_w2, sgu_w, sgu_b,
                 sgu_norm_g, sgu_norm_b, w_br_a, w_br_b, w_br_c, w_o):
    w_main, w_kv, w_gate = _split_w_in(w_in)
    norm_g = norm_g.reshape(1, D_MODEL)
    proj = _in_proj(x2, norm_g, w_main)
    kc, vc, ksa, vst, kwa, vwt, gt = _kv_proj(x2, b_, s_, norm_g, w_kv, w_gate)

    k_cmp, v_cmp = _compress(kc, vc, cmp_pos.reshape(2, 1, CMP_LEN * HD),
                             cmp_w1.astype(BF16), cmp_w2.astype(BF16))
    out_b = _nsa(proj, k_cmp, v_cmp.transpose(0, 1, 3, 2), ksa, vst, kwa, vwt, gt, tt, bc, ovt, s_)

    sgu_b2 = jnp.broadcast_to(sgu_b.T[:, :, None], (CHUNK, SGU_GROUPS, D_MODEL // SGU_GROUPS)).reshape(CHUNK, D_MODEL)
    return _mix(proj, out_b, x2, s_, conv_a, sgu_w, sgu_b2,
                sgu_norm_g.reshape(1, D_MODEL), sgu_norm_b.reshape(1, D_MODEL),
                w_br_a.astype(BF16), w_br_b.astype(BF16), w_br_c.astype(BF16), w_o.astype(BF16))


def _overlap_t(s_):
    n_blk = s_ // SEL_BLOCK
    cmp_start = jnp.arange(LANES) * CMP_STRIDE
    cmp_end = cmp_start + CMP_LEN - 1
    blk_start = jnp.arange(n_blk) * SEL_BLOCK
    n_cmp = (s_ - CMP_LEN) // CMP_STRIDE + 1
    ov = ((cmp_start[None, :] < blk_start[:, None] + SEL_BLOCK) & (cmp_end[None, :] >= blk_start[:, None])
          & (jnp.arange(LANES)[None, :] < n_cmp))
    return ov.astype(F32)


def kernel(x, rel_bias, norm_mix, w_in, conv_a, cmp_pos, cmp_w1, cmp_w2, sgu_w, sgu_b, sgu_norm_g, sgu_norm_b,
           w_br_a, w_br_b, w_br_c, w_o, norm_ffn, ffn_w_up, ffn_conv, ffn_w_down, norm_final):
    b_, s_, _ = x.shape
    depth = w_in.shape[0]
    x2 = x.reshape(b_ * s_, D_MODEL)
    tt, bc = _bias_tiles(rel_bias, s_ // Q_BLOCK)
    ovt = _overlap_t(s_)
    for l in range(depth):
        x2 = _mixer_layer(x2, b_, s_, tt, bc, ovt, norm_mix[l], w_in[l], conv_a[l], cmp_pos[l], cmp_w1[l],
                          cmp_w2[l], sgu_w[l], sgu_b[l], sgu_norm_g[l], sgu_norm_b[l],
                          w_br_a[l], w_br_b[l], w_br_c[l], w_o[l])
        x2 = _ffn(x2, s_, norm_ffn[l].reshape(1, D_MODEL), ffn_w_up[l], ffn_conv[l], ffn_w_down[l].astype(BF16),
                  norm_final.reshape(1, D_MODEL), final=(l == depth - 1))
    return x2.reshape(b_, s_, D_MODEL)
```
